```python
import jax
import jax.numpy as jnp
from jax import lax
import numpy as np

D_MODEL = 4096
BATCH = 4
SEQ = 2048
DEPTH = 2

SWA_HEADS = 16
SWA_KV_HEADS = 2
SWA_HEAD_DIM = 64
SWA_WINDOW = 128
SWA_BLOCK = 128
MLSTM_HEADS = 8
MLSTM_QK_DIM = 64
MLSTM_V_DIM = 128
MLSTM_CHUNK = 64
FOX_HEADS = 8
FOX_HEAD_DIM = 128
FOX_BLOCK = 128
D_FF = 4 * D_MODEL
LN_EPS = 1e-5
DN_ALPHA = (2 * DEPTH) ** 0.25
DN_BETA = (8 * DEPTH) ** -0.25

A_Q = SWA_HEADS * SWA_HEAD_DIM
A_KV = SWA_KV_HEADS * SWA_HEAD_DIM
B_QK = MLSTM_HEADS * MLSTM_QK_DIM
B_V = MLSTM_HEADS * MLSTM_V_DIM
C_W = FOX_HEADS * FOX_HEAD_DIM
SEG_WIDTHS = (A_Q, A_KV, A_KV,
              B_QK, B_QK, B_V, MLSTM_HEADS, MLSTM_HEADS, B_V,
              C_W, C_W, C_W, FOX_HEADS,
              D_MODEL, D_MODEL, D_MODEL)
VALUE_SEGS = (2, 5, 11)
D_IN = sum(SEG_WIDTHS)

kernel_name = 'hybrid_swa_mlstm_fox_deepnorm'


def split_columns(z):
    parts, off = [], 0
    for w in SEG_WIDTHS:
        parts.append(z[..., off:off + w])
        off += w
    return parts


def layer_norm(x, g, b):
    xf = x.astype(jnp.float32)
    mu = jnp.mean(xf, axis=-1, keepdims=True)
    var = jnp.mean(jnp.square(xf - mu), axis=-1, keepdims=True)
    y = (xf - mu) * lax.rsqrt(var + LN_EPS) * g.astype(jnp.float32) + b.astype(jnp.float32)
    return y.astype(x.dtype)


def alibi_slopes(n_heads):
    return 2.0 ** (-8.0 * jnp.arange(1, n_heads + 1, dtype=jnp.float32) / n_heads)


def swa_sink_attention(q, k, v, sinks):
    bsz, seq, _, dh = q.shape
    blk = SWA_BLOCK
    nb = seq // blk
    grp = SWA_HEADS // SWA_KV_HEADS
    f32 = jnp.float32
    qb = q.reshape(bsz, nb, blk, SWA_KV_HEADS, grp, dh)
    pad = jnp.zeros((bsz, blk, SWA_KV_HEADS, dh), k.dtype)
    kp = jnp.concatenate([pad, k], axis=1).reshape(bsz, nb + 1, blk, SWA_KV_HEADS, dh)
    vp = jnp.concatenate([pad.astype(v.dtype), v], axis=1).reshape(bsz, nb + 1, blk, SWA_KV_HEADS, dh)
    kb = jnp.concatenate([kp[:, :-1], kp[:, 1:]], axis=2)
    vb = jnp.concatenate([vp[:, :-1], vp[:, 1:]], axis=2)
    s = jnp.einsum('bnqhgd,bnkhd->bnhgqk', qb, kb, preferred_element_type=f32) * (dh ** -0.5)
    q_pos = jnp.arange(nb)[:, None] * blk + jnp.arange(blk)[None, :]
    k_pos = jnp.arange(nb)[:, None] * blk - blk + jnp.arange(2 * blk)[None, :]
    dist = q_pos[:, :, None] - k_pos[:, None, :]
    valid = (dist >= 0) & (dist < SWA_WINDOW) & (k_pos[:, None, :] >= 0)
    slopes = alibi_slopes(SWA_HEADS).reshape(SWA_KV_HEADS, grp)
    s = s - slopes[None, None, :, :, None, None] * dist.astype(f32)[None, :, None, None, :, :]
    s = jnp.where(valid[None, :, None, None, :, :], s, -jnp.inf)
    sink = sinks.astype(f32).reshape(SWA_KV_HEADS, grp)[None, None, :, :, None, None]
    m = jnp.maximum(jnp.max(s, axis=-1, keepdims=True), sink)
    p = jnp.exp(s - m)
    p = p / (jnp.sum(p, axis=-1, keepdims=True) + jnp.exp(sink - m))
    o = jnp.einsum('bnhgqk,bnkhd->bnqhgd', p.astype(v.dtype), vb)
    return o.reshape(bsz, seq, SWA_HEADS * dh)


def mlstm_chunkwise(q, k, v, i_pre, f_pre):
    bsz, seq, nh, dk = q.shape
    dv = v.shape[-1]
    L = MLSTM_CHUNK
    nc = seq // L
    f32 = jnp.float32

    def chunks(t):
        t = t.astype(f32).reshape((bsz, nc, L, nh) + t.shape[3:])
        return jnp.moveaxis(t, (1, 3), (0, 2))

    qc = chunks(q)
    kc = chunks(k) * (dk ** -0.5)
    vc = chunks(v)
    ic = chunks(i_pre)
    lfc = chunks(jax.nn.log_sigmoid(f_pre.astype(f32)))
    causal = jnp.tril(jnp.ones((L, L), dtype=bool))

    def step(carry, inp):
        C, n, m = carry
        qq, kk, vv, ig, lf = inp
        b = jnp.cumsum(lf, axis=-1)
        D = b[..., :, None] - b[..., None, :] + ig[..., None, :]
        D = jnp.where(causal, D, -jnp.inf)
        inter = b + m[..., None]
        m_t = jnp.maximum(inter, jnp.max(D, axis=-1))
        w_inter = jnp.exp(inter - m_t)
        S = jnp.einsum('bhtd,bhsd->bhts', qq, kk) * jnp.exp(D - m_t[..., None])
        num = w_inter[..., None] * jnp.einsum('bhtd,bhde->bhte', qq, C) + jnp.einsum('bhts,bhse->bhte', S, vv)
        den = w_inter * jnp.einsum('bhtd,bhd->bht', qq, n) + jnp.sum(S, axis=-1)
        h = num / jnp.maximum(jnp.abs(den), jnp.exp(-m_t))[..., None]
        m_new = m_t[..., -1]
        decay = jnp.exp(b[..., -1] + m - m_new)
        wk = jnp.exp(b[..., -1:] - b + ig - m_new[..., None])
        kw = kk * wk[..., None]
        C_new = decay[..., None, None] * C + jnp.einsum('bhsd,bhse->bhde', kw, vv)
        n_new = decay[..., None] * n + jnp.sum(kw, axis=2)
        return (C_new, n_new, m_new), h

    init = (jnp.zeros((bsz, nh, dk, dv), f32), jnp.zeros((bsz, nh, dk), f32), jnp.zeros((bsz, nh), f32))
    _, h = lax.scan(step, init, (qc, kc, vc, ic, lfc))
    return jnp.moveaxis(h, (0, 2), (1, 3)).reshape(bsz, seq, nh * dv)


def forgetting_attention(q, k, v, f_pre):
    bsz, seq, nh, dh = q.shape
    blk = FOX_BLOCK
    nb = seq // blk
    f32 = jnp.float32
    cum = jnp.cumsum(jax.nn.log_sigmoid(f_pre.astype(f32)), axis=1)
    cum_k = jnp.transpose(cum, (0, 2, 1))
    kh = jnp.transpose(k, (0, 2, 1, 3))
    vh = jnp.transpose(v, (0, 2, 1, 3))
    qb = jnp.transpose(q.reshape(bsz, nb, blk, nh, dh), (1, 0, 3, 2, 4))
    cq = jnp.transpose(cum.reshape(bsz, nb, blk, nh), (1, 0, 3, 2))
    k_pos = jnp.arange(seq)

    def block(args):
        qi, ci, idx = args
        q_pos = idx * blk + jnp.arange(blk)
        s = jnp.einsum('bhqd,bhkd->bhqk', qi, kh, preferred_element_type=f32) * (dh ** -0.5)
        s = s + ci[..., :, None] - cum_k[..., None, :]
        s = jnp.where(k_pos[None, :] <= q_pos[:, None], s, -jnp.inf)
        p = jax.nn.softmax(s, axis=-1)
        return jnp.einsum('bhqk,bhkd->bhqd', p.astype(v.dtype), vh)

    o = lax.map(block, (qb, cq, jnp.arange(nb)))
    return jnp.transpose(o, (1, 0, 3, 2, 4)).reshape(bsz, seq, nh * dh)


def setup_inputs(seed: int = 0) -> dict:
    key = jax.random.key(seed)
    ks = jax.random.split(key, 16)
    f32 = jnp.float32
    nrm = jax.random.normal
    x = nrm(ks[0], (BATCH, SEQ, D_MODEL), f32)
    col_scale = jnp.concatenate([jnp.full((w,), DN_BETA if i in VALUE_SEGS else 1.0, f32)
                                 for i, w in enumerate(SEG_WIDTHS)])
    w_in = nrm(ks[1], (DEPTH, D_MODEL, D_IN), f32) * (D_MODEL ** -0.5) * col_scale
    b_mlstm_i = 0.1 * nrm(ks[2], (DEPTH, MLSTM_HEADS), f32)
    b_mlstm_f = 3.0 + 0.5 * nrm(ks[3], (DEPTH, MLSTM_HEADS), f32)
    b_fox_f = 3.0 + 0.5 * nrm(ks[4], (DEPTH, FOX_HEADS), f32)
    attn_sinks = 0.5 * nrm(ks[5], (DEPTH, SWA_HEADS), f32)
    w_up_swa = nrm(ks[6], (DEPTH, A_Q, D_MODEL), f32) * (A_Q ** -0.5) * DN_BETA
    w_up_mlstm = nrm(ks[7], (DEPTH, B_V, D_MODEL), f32) * (B_V ** -0.5) * DN_BETA
    w_up_fox = nrm(ks[8], (DEPTH, C_W, D_MODEL), f32) * (C_W ** -0.5) * DN_BETA
    w_o = nrm(ks[9], (DEPTH, D_MODEL, D_MODEL), f32) * (D_MODEL ** -0.5) * DN_BETA
    ln1_g = 1.0 + 0.02 * nrm(ks[10], (DEPTH, D_MODEL), f32)
    ln1_b = 0.02 * nrm(ks[11], (DEPTH, D_MODEL), f32)
    w_ff1 = nrm(ks[12], (DEPTH, D_MODEL, D_FF), f32) * (D_MODEL ** -0.5)
    w_ff2 = nrm(ks[13], (DEPTH, D_FF, D_MODEL), f32) * (D_FF ** -0.5) * DN_BETA
    ln2_g = 1.0 + 0.02 * nrm(ks[14], (DEPTH, D_MODEL), f32)
    ln2_b = 0.02 * nrm(ks[15], (DEPTH, D_MODEL), f32)
    return {'x': x, 'w_in': w_in, 'b_mlstm_i': b_mlstm_i, 'b_mlstm_f': b_mlstm_f,
            'b_fox_f': b_fox_f, 'attn_sinks': attn_sinks, 'w_up_swa': w_up_swa,
            'w_up_mlstm': w_up_mlstm, 'w_up_fox': w_up_fox, 'w_o': w_o,
            'ln1_g': ln1_g, 'ln1_b': ln1_b, 'w_ff1': w_ff1, 'w_ff2': w_ff2,
            'ln2_g': ln2_g, 'ln2_b': ln2_b}


def reference(x, w_in, b_mlstm_i, b_mlstm_f, b_fox_f, attn_sinks, w_up_swa, w_up_mlstm,
              w_up_fox, w_o, ln1_g, ln1_b, w_ff1, w_ff2, ln2_g, ln2_b):
    bsz, seq, _ = x.shape
    f32 = jnp.float32
    for l in range(DEPTH):
        z = x @ w_in[l]
        (a_q, a_k, a_v, m_q, m_k, m_v, m_i, m_f, m_o,
         c_q, c_k, c_v, c_f, g_a, g_b, g_c) = split_columns(z)
        y_a = swa_sink_attention(a_q.reshape(bsz, seq, SWA_HEADS, SWA_HEAD_DIM),
                                 a_k.reshape(bsz, seq, SWA_KV_HEADS, SWA_HEAD_DIM),
                                 a_v.reshape(bsz, seq, SWA_KV_HEADS, SWA_HEAD_DIM),
                                 attn_sinks[l])
        h_b = mlstm_chunkwise(m_q.reshape(bsz, seq, MLSTM_HEADS, MLSTM_QK_DIM),
                              m_k.reshape(bsz, seq, MLSTM_HEADS, MLSTM_QK_DIM),
                              m_v.reshape(bsz, seq, MLSTM_HEADS, MLSTM_V_DIM),
                              m_i + b_mlstm_i[l], m_f + b_mlstm_f[l])
        y_b = (jax.nn.sigmoid(m_o.astype(f32)) * h_b).astype(x.dtype)
        y_c = forgetting_attention(c_q.reshape(bsz, seq, FOX_HEADS, FOX_HEAD_DIM),
                                   c_k.reshape(bsz, seq, FOX_HEADS, FOX_HEAD_DIM),
                                   c_v.reshape(bsz, seq, FOX_HEADS, FOX_HEAD_DIM),
                                   c_f + b_fox_f[l])
        mix = (jax.nn.sigmoid(g_a) * (y_a @ w_up_swa[l])
               + jax.nn.sigmoid(g_b) * (y_b @ w_up_mlstm[l])
               + jax.nn.sigmoid(g_c) * (y_c @ w_up_fox[l]))
        x = layer_norm(DN_ALPHA * x + mix @ w_o[l], ln1_g[l], ln1_b[l])
        hid = jnp.square(jax.nn.relu(x @ w_ff1[l]))
        x = layer_norm(DN_ALPHA * x + hid @ w_ff2[l], ln2_g[l], ln2_b[l])
    return x
```

```python
import functools

import jax
import jax.numpy as jnp
from jax import lax
from jax.experimental import pallas as pl
from jax.experimental.pallas import tpu as pltpu

D_MODEL = 4096
SWA_HEADS, SWA_KV_HEADS, SWA_HEAD_DIM, SWA_WINDOW = 16, 2, 64, 128
MLSTM_HEADS, MLSTM_QK_DIM, MLSTM_V_DIM = 8, 64, 128
FOX_HEADS, FOX_HEAD_DIM = 8, 128
D_FF = 4 * D_MODEL
LN_EPS = 1e-5
DEPTH = 2
DN_ALPHA = (2 * DEPTH) ** 0.25

A_Q = SWA_HEADS * SWA_HEAD_DIM
A_KV = SWA_KV_HEADS * SWA_HEAD_DIM
B_QK = MLSTM_HEADS * MLSTM_QK_DIM
B_V = MLSTM_HEADS * MLSTM_V_DIM
C_W = FOX_HEADS * FOX_HEAD_DIM
SEG_WIDTHS = (A_Q, A_KV, A_KV, B_QK, B_QK, B_V, MLSTM_HEADS, MLSTM_HEADS, B_V,
              C_W, C_W, C_W, FOX_HEADS, D_MODEL, D_MODEL, D_MODEL)
SEG_OFFS = tuple(sum(SEG_WIDTHS[:i]) for i in range(len(SEG_WIDTHS)))

LANES = 128
VMEM_LIMIT_CAP = 56 * 1024 * 1024

ZB_AQ = 0
ZB_MV = 1024
ZB_CQ = 2048
ZB_CK = 3072
ZB_CV = 4096
ZB_MQ = 5120
ZB_AK = 5632
ZB_AV = 5888
ZB_N = 6144
ZF_MO = 0
ZF_MK = 1024
ZF_G = 1536
ZF_N = 1664

MLSTM_L = 256
FOX_BLK = 512
SWA_BLK = 128
NEG_INF = float("-inf")


def _cparams(sem, vmem_bytes):
    return pltpu.CompilerParams(dimension_semantics=sem,
                                vmem_limit_bytes=int(min(vmem_bytes, VMEM_LIMIT_CAP)))


def _mm_body(x_ref, w_ref, o_ref, *, act):
    acc = jnp.dot(x_ref[...], w_ref[...], preferred_element_type=jnp.float32)
    if act == "relu2":
        acc = jnp.square(jnp.maximum(acc, 0.0))
    o_ref[...] = acc.astype(o_ref.dtype)


def _mm_res_body(x_ref, w_ref, r_ref, o_ref):
    acc = jnp.dot(x_ref[...], w_ref[...], preferred_element_type=jnp.float32)
    o_ref[...] = DN_ALPHA * r_ref[...] + acc


def _mm_res_k_body(x_ref, w_ref, r_ref, o_ref, acc_ref):
    k = pl.program_id(2)
    part = jnp.dot(x_ref[...], w_ref[...], preferred_element_type=jnp.float32)

    @pl.when(k == 0)
    def _():
        acc_ref[...] = part

    @pl.when(k > 0)
    def _():
        acc_ref[...] += part

    @pl.when(k == pl.num_programs(2) - 1)
    def _():
        o_ref[...] = DN_ALPHA * r_ref[...] + acc_ref[...]


def matmul(x, w, *, bm, bn, out_dtype, act=None, name):
    m, k = x.shape
    n = w.shape[1]
    bm, bn = min(bm, m), min(bn, n)
    assert m % bm == 0 and n % bn == 0
    osz = jnp.dtype(out_dtype).itemsize
    vmem = 2 * (bm * k * 2 + k * bn * 2 + bm * bn * osz) + bm * bn * 4 + (4 << 20)
    return pl.pallas_call(
        functools.partial(_mm_body, act=act),
        grid=(m // bm, n // bn),
        in_specs=[pl.BlockSpec((bm, k), lambda i, j: (i, 0)),
                  pl.BlockSpec((k, bn), lambda i, j: (0, j))],
        out_specs=pl.BlockSpec((bm, bn), lambda i, j: (i, j)),
        out_shape=jax.ShapeDtypeStruct((m, n), out_dtype),
        compiler_params=_cparams(("parallel", "arbitrary"), vmem),
        name=name,
    )(x, w)


def matmul_residual(x, w, res, *, bm, bn, bk, name):
    m, k = x.shape
    n = w.shape[1]
    bm, bn, bk = min(bm, m), min(bn, n), min(bk, k)
    assert m % bm == 0 and n % bn == 0 and k % bk == 0
    if bk == k:
        vmem = 2 * (bm * k * 2 + k * bn * 2 + 2 * bm * bn * 4) + bm * bn * 4 + (4 << 20)
        return pl.pallas_call(
            _mm_res_body,
            grid=(m // bm, n // bn),
            in_specs=[pl.BlockSpec((bm, k), lambda i, j: (i, 0)),
                      pl.BlockSpec((k, bn), lambda i, j: (0, j)),
                      pl.BlockSpec((bm, bn), lambda i, j: (i, j))],
            out_specs=pl.BlockSpec((bm, bn), lambda i, j: (i, j)),
            out_shape=jax.ShapeDtypeStruct((m, n), jnp.float32),
            compiler_params=_cparams(("parallel", "arbitrary"), vmem),
            name=name,
        )(x, w, res)
    vmem = 2 * (bm * bk * 2 + bk * bn * 2 + 2 * bm * bn * 4) + 2 * bm * bn * 4 + (4 << 20)
    return pl.pallas_call(
        _mm_res_k_body,
        grid=(m // bm, n // bn, k // bk),
        in_specs=[pl.BlockSpec((bm, bk), lambda i, j, kk: (i, kk)),
                  pl.BlockSpec((bk, bn), lambda i, j, kk: (kk, j)),
                  pl.BlockSpec((bm, bn), lambda i, j, kk: (i, j))],
        out_specs=pl.BlockSpec((bm, bn), lambda i, j, kk: (i, j)),
        out_shape=jax.ShapeDtypeStruct((m, n), jnp.float32),
        scratch_shapes=[pltpu.VMEM((bm, bn), jnp.float32)],
        compiler_params=_cparams(("parallel", "arbitrary", "arbitrary"), vmem),
        name=name,
    )(x, w, res)


def _merge_body(x_ref, wg_ref, y_ref, wu_ref, o_ref, acc_ref):
    br = pl.program_id(2)
    g = jnp.dot(x_ref[...], wg_ref[0], preferred_element_type=jnp.float32)
    u = jnp.dot(y_ref[0], wu_ref[0], preferred_element_type=jnp.float32)
    contrib = u / (1.0 + jnp.exp(-g))

    @pl.when(br == 0)
    def _():
        acc_ref[...] = contrib

    @pl.when(br > 0)
    def _():
        acc_ref[...] += contrib

    @pl.when(br == pl.num_programs(2) - 1)
    def _():
        o_ref[...] = acc_ref[...].astype(o_ref.dtype)


def gated_merge(xb, wg, y, wu, *, bm, bn):
    t, d = xb.shape
    nbr, _, kin = y.shape
    bm, bn = min(bm, t), min(bn, d)
    vmem = 2 * (bm * d * 2 + d * bn * 2 + bm * kin * 2 + kin * bn * 2 + bm * bn * 2) \
        + 4 * bm * bn * 4 + (4 << 20)
    return pl.pallas_call(
        _merge_body,
        grid=(t // bm, d // bn, nbr),
        in_specs=[pl.BlockSpec((bm, d), lambda i, j, b: (i, 0)),
                  pl.BlockSpec((1, d, bn), lambda i, j, b: (b, 0, j)),
                  pl.BlockSpec((1, bm, kin), lambda i, j, b: (b, i, 0)),
                  pl.BlockSpec((1, kin, bn), lambda i, j, b: (b, 0, j))],
        out_specs=pl.BlockSpec((bm, bn), lambda i, j, b: (i, j)),
        out_shape=jax.ShapeDtypeStruct((t, d), jnp.bfloat16),
        scratch_shapes=[pltpu.VMEM((bm, bn), jnp.float32)],
        compiler_params=_cparams(("parallel", "arbitrary", "arbitrary"), vmem),
        name="gated_merge",
    )(xb, wg, y, wu)


def _ln_body(r_ref, g_ref, b_ref, of_ref, ob_ref):
    r = r_ref[...]
    mu = jnp.mean(r, axis=-1, keepdims=True)
    xc = r - mu
    var = jnp.mean(xc * xc, axis=-1, keepdims=True)
    y = xc * lax.rsqrt(var + LN_EPS) * g_ref[...] + b_ref[...]
    of_ref[...] = y
    ob_ref[...] = y.astype(jnp.bfloat16)


def layer_norm(r, g, b, *, bm=256):
    t, d = r.shape
    bm = min(bm, t)
    vmem = 2 * (bm * d * 4 * 2 + bm * d * 2) + 4 * bm * d * 4 + (4 << 20)
    return pl.pallas_call(
        _ln_body,
        grid=(t // bm,),
        in_specs=[pl.BlockSpec((bm, d), lambda i: (i, 0)),
                  pl.BlockSpec((1, d), lambda i: (0, 0)),
                  pl.BlockSpec((1, d), lambda i: (0, 0))],
        out_specs=[pl.BlockSpec((bm, d), lambda i: (i, 0)),
                   pl.BlockSpec((bm, d), lambda i: (i, 0))],
        out_shape=[jax.ShapeDtypeStruct((t, d), jnp.float32),
                   jax.ShapeDtypeStruct((t, d), jnp.bfloat16)],
        compiler_params=_cparams(("parallel",), vmem),
        name="layer_norm",
    )(r, g.reshape(1, d), b.reshape(1, d))


def _swa_body(sink_ref, q_ref, kp_ref, kc_ref, vp_ref, vc_ref, o_ref):
    n = pl.program_id(1)
    blk = SWA_BLK
    half = SWA_HEAD_DIM
    scale = SWA_HEAD_DIM ** -0.5
    qi = lax.broadcasted_iota(jnp.int32, (blk, 2 * blk), 0)
    kj = lax.broadcasted_iota(jnp.int32, (blk, 2 * blk), 1)
    dist = qi + blk - kj
    valid = (dist >= 0) & (dist < SWA_WINDOW) & ((kj >= blk) | (n > 0))
    distf = dist.astype(jnp.float32)
    lane = lax.broadcasted_iota(jnp.int32, (2 * blk, LANES), 1)
    lo = lane < half
    group = SWA_HEADS // SWA_KV_HEADS
    for g in range(SWA_KV_HEADS):
        kd = jnp.concatenate([kp_ref[:, g * LANES:(g + 1) * LANES],
                              kc_ref[:, g * LANES:(g + 1) * LANES]], axis=0)
        vd = jnp.concatenate([vp_ref[:, g * LANES:(g + 1) * LANES],
                              vc_ref[:, g * LANES:(g + 1) * LANES]], axis=0)
        zero = jnp.zeros_like(kd)
        kz = jnp.concatenate([jnp.where(lo, kd, zero), jnp.where(lo, zero, kd)], axis=0)
        vz = jnp.concatenate([jnp.where(lo, vd, zero), jnp.where(lo, zero, vd)], axis=0)
        for pp in range(group // 2):
            p = g * (group // 2) + pp
            qp = q_ref[:, p * LANES:(p + 1) * LANES]
            s2 = lax.dot_general(qp, kz, (((1,), (1,)), ((), ())),
                                 preferred_element_type=jnp.float32)
            probs = []
            for e in range(2):
                h = 2 * p + e
                slope = 2.0 ** (-8.0 * (h + 1) / SWA_HEADS)
                sink = sink_ref[h]
                s = s2[:, e * 2 * blk:(e + 1) * 2 * blk] * scale - slope * distf
                s = jnp.where(valid, s, NEG_INF)
                m = jnp.maximum(jnp.max(s, axis=-1, keepdims=True), sink)
                pe = jnp.exp(s - m)
                den = jnp.sum(pe, axis=-1, keepdims=True) + jnp.exp(sink - m)
                probs.append((pe / den).astype(jnp.bfloat16))
            p2 = jnp.concatenate(probs, axis=1)
            o_ref[:, p * LANES:(p + 1) * LANES] = jnp.dot(
                p2, vz, preferred_element_type=jnp.float32).astype(o_ref.dtype)


def swa_attention(zb, sinks, bsz, seq):
    blk = SWA_BLK
    nb = seq // blk
    t = bsz * seq
    kcol, vcol = ZB_AK // (2 * LANES), ZB_AV // (2 * LANES)

    def cur(b, n):
        return b * nb + n

    def prev(b, n):
        return b * nb + jnp.maximum(n - 1, 0)

    return pl.pallas_call(
        _swa_body,
        grid=(bsz, nb),
        in_specs=[pl.BlockSpec(memory_space=pltpu.SMEM),
                  pl.BlockSpec((blk, A_Q), lambda b, n: (cur(b, n), ZB_AQ // A_Q)),
                  pl.BlockSpec((blk, 2 * LANES), lambda b, n: (prev(b, n), kcol)),
                  pl.BlockSpec((blk, 2 * LANES), lambda b, n: (cur(b, n), kcol)),
                  pl.BlockSpec((blk, 2 * LANES), lambda b, n: (prev(b, n), vcol)),
                  pl.BlockSpec((blk, 2 * LANES), lambda b, n: (cur(b, n), vcol))],
        out_specs=pl.BlockSpec((blk, A_Q), lambda b, n: (cur(b, n), 0)),
        out_shape=jax.ShapeDtypeStruct((t, A_Q), jnp.bfloat16),
        compiler_params=_cparams(("parallel", "arbitrary"), 32 << 20),
        name="swa_attention",
    )(sinks, zb, zb, zb, zb, zb)


def _mlstm_body(bias_ref, q_ref, k_ref, v_ref, og_ref, g_ref, y_ref, cum_ref,
                c_scr, n_scr, m_scr, carry_scr):
    c = pl.program_id(1)
    L = MLSTM_L
    dk, dv = MLSTM_QK_DIM, MLSTM_V_DIM

    @pl.when(c == 0)
    def _():
        c_scr[...] = jnp.zeros_like(c_scr)
        n_scr[...] = jnp.zeros_like(n_scr)
        m_scr[...] = jnp.zeros_like(m_scr)
        carry_scr[...] = jnp.zeros_like(carry_scr)

    a = g_ref[...] + bias_ref[...]
    lf = jnp.minimum(a, 0.0) - jnp.log1p(jnp.exp(-jnp.abs(a)))
    row = lax.broadcasted_iota(jnp.int32, (L, L), 0)
    col = lax.broadcasted_iota(jnp.int32, (L, L), 1)
    causal = row >= col
    tri = causal.astype(jnp.float32)
    b_all = jnp.dot(tri, lf, preferred_element_type=jnp.float32,
                    precision=lax.Precision.HIGHEST)
    cum = b_all + carry_scr[0:1, :]
    carry_scr[0:1, :] = cum[L - 1:L, :]
    cum_ref[0] = cum.T
    a_t = a.T
    b_t = b_all.T

    lane = lax.broadcasted_iota(jnp.int32, (L, LANES), 1)
    lo = lane < dk
    crow = lax.broadcasted_iota(jnp.int32, (2 * dk, 2 * dv), 0)
    ccol = lax.broadcasted_iota(jnp.int32, (2 * dk, 2 * dv), 1)
    crow_lo = crow < dk
    blockdiag = crow_lo == (ccol < dv)
    nlane_lo = lax.broadcasted_iota(jnp.int32, (1, LANES), 1) < dk

    for p in range(MLSTM_HEADS // 2):
        qp = q_ref[:, p * LANES:(p + 1) * LANES]
        kp = k_ref[:, p * LANES:(p + 1) * LANES] * (dk ** -0.5)
        kpb = kp.astype(jnp.bfloat16)
        zero = jnp.zeros_like(kpb)
        kz = jnp.concatenate([jnp.where(lo, kpb, zero), jnp.where(lo, zero, kpb)], axis=0)
        sc = lax.dot_general(qp, kz, (((1,), (1,)), ((), ())),
                             preferred_element_type=jnp.float32)
        cp = c_scr[p]
        qc = jnp.dot(qp, cp.astype(jnp.bfloat16), preferred_element_type=jnp.float32)
        n_row = n_scr[p:p + 1, :]
        qn_prod = qp.astype(jnp.float32) * n_row
        wk_cols, decays = [], []
        for e in range(2):
            h = 2 * p + e
            b_col = b_all[:, 8 + h:9 + h]
            b_row = b_t[8 + h:9 + h, :]
            ig_row = a_t[h:h + 1, :]
            ig_col = a[:, h:h + 1]
            m_prev = m_scr[h:h + 1, 0:1]
            dmat = jnp.where(causal, b_col - b_row + ig_row, NEG_INF)
            inter = b_col + m_prev
            m_t = jnp.maximum(inter, jnp.max(dmat, axis=-1, keepdims=True))
            w_inter = jnp.exp(inter - m_t)
            smat = sc[:, e * L:(e + 1) * L] * jnp.exp(dmat - m_t)
            qn = jnp.sum(jnp.where(lo == (e == 0), qn_prod, 0.0), axis=-1, keepdims=True)
            num = w_inter * qc[:, e * dv:(e + 1) * dv] + jnp.dot(
                smat.astype(jnp.bfloat16), v_ref[:, h * dv:(h + 1) * dv],
                preferred_element_type=jnp.float32)
            den = w_inter * qn + jnp.sum(smat, axis=-1, keepdims=True)
            hh = num / jnp.maximum(jnp.abs(den), jnp.exp(-m_t))
            og = og_ref[:, h * dv:(h + 1) * dv]
            y_ref[:, h * dv:(h + 1) * dv] = (hh / (1.0 + jnp.exp(-og))).astype(y_ref.dtype)
            m_new = m_t[L - 1:L, :]
            b_last = b_col[L - 1:L, :]
            decays.append(jnp.exp(b_last + m_prev - m_new))
            wk_cols.append(jnp.exp(b_last - b_col + ig_col - m_new))
            m_scr[h:h + 1, :] = jnp.broadcast_to(m_new, (1, LANES))
        kw = kp * jnp.where(lo, wk_cols[0], wk_cols[1])
        n_scr[p:p + 1, :] = (jnp.where(nlane_lo, decays[0], decays[1]) * n_row
                             + jnp.sum(kw, axis=0, keepdims=True))
        upd = jnp.dot(kw.T.astype(jnp.bfloat16), v_ref[:, 2 * p * dv:(2 * p + 2) * dv],
                      preferred_element_type=jnp.float32)
        c_scr[p] = jnp.where(crow_lo, decays[0], decays[1]) * cp + jnp.where(blockdiag, upd, 0.0)


def mlstm(zb, zf, bias_row, bsz, seq):
    L = MLSTM_L
    nc = seq // L
    t = bsz * seq

    def rows(b, c):
        return b * nc + c

    return pl.pallas_call(
        _mlstm_body,
        grid=(bsz, nc),
        in_specs=[pl.BlockSpec((1, LANES), lambda b, c: (0, 0)),
                  pl.BlockSpec((L, B_QK), lambda b, c: (rows(b, c), ZB_MQ // B_QK)),
                  pl.BlockSpec((L, B_QK), lambda b, c: (rows(b, c), ZF_MK // B_QK)),
                  pl.BlockSpec((L, B_V), lambda b, c: (rows(b, c), ZB_MV // B_V)),
                  pl.BlockSpec((L, B_V), lambda b, c: (rows(b, c), ZF_MO // B_V)),
                  pl.BlockSpec((L, LANES), lambda b, c: (rows(b, c), ZF_G // LANES))],
        out_specs=[pl.BlockSpec((L, B_V), lambda b, c: (rows(b, c), 0)),
                   pl.BlockSpec((1, LANES, L), lambda b, c: (b, 0, c))],
        out_shape=[jax.ShapeDtypeStruct((t, B_V), jnp.bfloat16),
                   jax.ShapeDtypeStruct((bsz, LANES, seq), jnp.float32)],
        scratch_shapes=[pltpu.VMEM((MLSTM_HEADS // 2, 2 * MLSTM_QK_DIM, 2 * MLSTM_V_DIM), jnp.float32),
                        pltpu.VMEM((8, LANES), jnp.float32),
                        pltpu.VMEM((8, LANES), jnp.float32),
                        pltpu.VMEM((8, LANES), jnp.float32)],
        compiler_params=_cparams(("arbitrary", "arbitrary"), 32 << 20),
        name="mlstm",
    )(bias_row, zb, zf, zb, zf, zf)


def _fox_body(q_ref, k_ref, v_ref, ck_ref, o_ref):
    h = pl.program_id(1)
    qi = pl.program_id(2)
    blk = FOX_BLK
    scale = FOX_HEAD_DIM ** -0.5
    q = q_ref[...]

    def scores(j):
        start = pl.multiple_of(j * blk, blk)
        kj = k_ref[pl.ds(start, blk), :]
        s = lax.dot_general(q, kj, (((1,), (1,)), ((), ())), preferred_element_type=jnp.float32)
        return s * scale - ck_ref[0, h, pl.ds(j, 1), :], start

    def update(s, start, m, l, acc):
        m_new = jnp.maximum(m, jnp.max(s, axis=-1, keepdims=True))
        alpha = jnp.exp(m - m_new)
        p = jnp.exp(s - m_new)
        l = alpha * l + jnp.sum(p, axis=-1, keepdims=True)
        acc = alpha * acc + jnp.dot(p.astype(jnp.bfloat16), v_ref[pl.ds(start, blk), :],
                                    preferred_element_type=jnp.float32)
        return m_new, l, acc

    def step(j, carry):
        s, start = scores(j)
        return update(s, start, *carry)

    init = (jnp.full((blk, 1), NEG_INF, jnp.float32), jnp.zeros((blk, 1), jnp.float32),
            jnp.zeros((blk, FOX_HEAD_DIM), jnp.float32))
    carry = lax.fori_loop(0, qi, step, init)
    s, start = scores(qi)
    row = lax.broadcasted_iota(jnp.int32, (blk, blk), 0)
    colm = lax.broadcasted_iota(jnp.int32, (blk, blk), 1)
    s = jnp.where(colm <= row, s, NEG_INF)
    _, l, acc = update(s, start, *carry)
    o_ref[...] = (acc / l).astype(o_ref.dtype)


def fox_attention(zb, cum_t, bsz, seq):
    blk = min(FOX_BLK, seq)
    assert blk == FOX_BLK
    nq = seq // blk
    t = bsz * seq
    ck = cum_t.reshape(bsz, LANES, nq, blk)
    return pl.pallas_call(
        _fox_body,
        grid=(bsz, FOX_HEADS, nq),
        in_specs=[pl.BlockSpec((blk, LANES), lambda b, h, i: (b * nq + i, ZB_CQ // LANES + h)),
                  pl.BlockSpec((seq, LANES), lambda b, h, i: (b, ZB_CK // LANES + h)),
                  pl.BlockSpec((seq, LANES), lambda b, h, i: (b, ZB_CV // LANES + h)),
                  pl.BlockSpec((1, 8, nq, blk), lambda b, h, i: (b, 2, 0, 0))],
        out_specs=pl.BlockSpec((blk, LANES), lambda b, h, i: (b * nq + i, h)),
        out_shape=jax.ShapeDtypeStruct((t, C_W), jnp.bfloat16),
        compiler_params=_cparams(("parallel", "arbitrary", "arbitrary"), 32 << 20),
        name="fox_attention",
    )(zb, zb, zb, ck)


def _seg(w, i):
    return w[:, SEG_OFFS[i]:SEG_OFFS[i] + SEG_WIDTHS[i]]


def _dup_halves(w):
    k, n = w.shape
    w = w.reshape(k, n // SWA_HEAD_DIM, 1, SWA_HEAD_DIM)
    return jnp.broadcast_to(w, (k, n // SWA_HEAD_DIM, 2, SWA_HEAD_DIM)).reshape(k, 2 * n)


def _prep_layer(w_in, w_up_swa, w_up_mlstm, w_up_fox, w_o, w_ff1, w_ff2,
                b_mlstm_i, b_mlstm_f, b_fox_f):
    bf = jnp.bfloat16
    a_q, a_k, a_v, m_q, m_k, m_v, m_i, m_f, m_o, c_q, c_k, c_v, c_f, g_a, g_b, g_c = (
        _seg(w_in, i) for i in range(len(SEG_WIDTHS)))
    w_zb = jnp.concatenate([a_q, m_v, c_q, c_k, c_v, m_q, _dup_halves(a_k), _dup_halves(a_v)],
                           axis=1).astype(bf)
    gate_pad = jnp.zeros((D_MODEL, LANES - 3 * MLSTM_HEADS), w_in.dtype)
    w_zf = jnp.concatenate([m_o, m_k, m_i, m_f, c_f, gate_pad], axis=1).astype(bf)
    w_g = jnp.stack([g_a, g_b, g_c]).astype(bf)
    w_up = jnp.stack([w_up_swa, w_up_mlstm, w_up_fox]).astype(bf)
    bias_row = jnp.concatenate([b_mlstm_i, b_mlstm_f, b_fox_f,
                                jnp.zeros((LANES - 3 * MLSTM_HEADS,), jnp.float32)]).reshape(1, LANES)
    return w_zb, w_zf, w_g, w_up, w_o.astype(bf), w_ff1.astype(bf), w_ff2.astype(bf), bias_row


def kernel(x, w_in, b_mlstm_i, b_mlstm_f, b_fox_f, attn_sinks, w_up_swa, w_up_mlstm, w_up_fox,
           w_o, ln1_g, ln1_b, w_ff1, w_ff2, ln2_g, ln2_b):
    bsz, seq, d = x.shape
    t = bsz * seq
    assert d == D_MODEL and seq % FOX_BLK == 0 and seq % MLSTM_L == 0
    xf = x.reshape(t, d).astype(jnp.float32)
    xb = xf.astype(jnp.bfloat16)
    for l in range(DEPTH):
        w_zb, w_zf, w_g, w_up, w_ob, w_1b, w_2b, bias_row = _prep_layer(
            w_in[l], w_up_swa[l], w_up_mlstm[l], w_up_fox[l], w_o[l], w_ff1[l], w_ff2[l],
            b_mlstm_i[l], b_mlstm_f[l], b_fox_f[l])
        zb = matmul(xb, w_zb, bm=1024, bn=1024, out_dtype=jnp.bfloat16, name="in_proj_bf16")
        zf = matmul(xb, w_zf, bm=512, bn=ZF_N, out_dtype=jnp.float32, name="in_proj_f32")
        y_a = swa_attention(zb, attn_sinks[l].astype(jnp.float32), bsz, seq)
        y_b, cum_t = mlstm(zb, zf, bias_row, bsz, seq)
        y_c = fox_attention(zb, cum_t, bsz, seq)
        mix = gated_merge(xb, w_g, jnp.stack([y_a, y_b, y_c]), w_up, bm=1024, bn=512)
        r1 = matmul_residual(mix, w_ob, xf, bm=1024, bn=512, bk=D_MODEL, name="out_proj")
        xf, xb = layer_norm(r1, ln1_g[l], ln1_b[l])
        hid = matmul(xb, w_1b, bm=1024, bn=1024, out_dtype=jnp.bfloat16, act="relu2", name="ff1")
        r2 = matmul_residual(hid, w_2b, xf, bm=1024, bn=1024, bk=2048, name="ff2")
        xf, xb = layer_norm(r2, ln2_g[l], ln2_b[l])
    return xf.reshape(bsz, seq, d).astype(x.dtype)
```

```python
import functools

import jax
import jax.numpy as jnp
from jax import lax
from jax.experimental import pallas as pl
from jax.experimental.pallas import tpu as pltpu

D_MODEL = 4096
SWA_HEADS, SWA_KV_HEADS, SWA_HEAD_DIM, SWA_WINDOW = 16, 2, 64, 128
MLSTM_HEADS, MLSTM_QK_DIM, MLSTM_V_DIM = 8, 64, 128
FOX_HEADS, FOX_HEAD_DIM = 8, 128
D_FF = 4 * D_MODEL
LN_EPS = 1e-5
DEPTH = 2
DN_ALPHA = (2 * DEPTH) ** 0.25

A_Q = SWA_HEADS * SWA_HEAD_DIM
A_KV = SWA_KV_HEADS * SWA_HEAD_DIM
B_QK = MLSTM_HEADS * MLSTM_QK_DIM
B_V = MLSTM_HEADS * MLSTM_V_DIM
C_W = FOX_HEADS * FOX_HEAD_DIM
SEG_WIDTHS = (A_Q, A_KV, A_KV, B_QK, B_QK, B_V, MLSTM_HEADS, MLSTM_HEADS, B_V,
              C_W, C_W, C_W, FOX_HEADS, D_MODEL, D_MODEL, D_MODEL)
SEG_OFFS = tuple(sum(SEG_WIDTHS[:i]) for i in range(len(SEG_WIDTHS)))

LANES = 128
VMEM_LIMIT_CAP = 56 * 1024 * 1024

ZB_AQ = 0
ZB_MV = 1024
ZB_CQ = 2048
ZB_CK = 3072
ZB_CV = 4096
ZB_MQ = 5120
ZB_AK = 5632
ZB_AV = 5888
ZB_N = 6144
ZF_MO = 0
ZF_MK = 1024
ZF_G = 1536
ZF_N = 1664
WT = 512
WA_ZF = 0
WA_ZB = 2048
WA_G = WA_ZB + ZB_N
WA_N = WA_G + 3 * D_MODEL

MLSTM_L = 256
FOX_BLK = 512
SWA_BLK = 128
NEG_INF = float("-inf")
LOG2E = 1.4426950408889634


def _cparams(sem, vmem_bytes):
    return pltpu.CompilerParams(dimension_semantics=sem,
                                vmem_limit_bytes=int(min(vmem_bytes, VMEM_LIMIT_CAP)))


_NT = (((1,), (1,)), ((), ()))


def _mm_body(x_ref, w_ref, o_ref, *, act, w_t):
    if w_t:
        acc = lax.dot_general(x_ref[...], w_ref[...], _NT, preferred_element_type=jnp.float32)
    else:
        acc = jnp.dot(x_ref[...], w_ref[...], preferred_element_type=jnp.float32)
    if act == "relu2":
        acc = jnp.square(jnp.maximum(acc, 0.0))
    o_ref[...] = acc.astype(o_ref.dtype)


def _mm_res_body(x_ref, w_ref, r_ref, o_ref):
    acc = jnp.dot(x_ref[...], w_ref[...], preferred_element_type=jnp.float32)
    o_ref[...] = DN_ALPHA * r_ref[...] + acc


def _mm_res_k_body(x_ref, w_ref, r_ref, o_ref, acc_ref):
    k = pl.program_id(2)
    part = jnp.dot(x_ref[...], w_ref[...], preferred_element_type=jnp.float32)

    @pl.when(k == 0)
    def _():
        acc_ref[...] = part

    @pl.when(k > 0)
    def _():
        acc_ref[...] += part

    @pl.when(k == pl.num_programs(2) - 1)
    def _():
        o_ref[...] = DN_ALPHA * r_ref[...] + acc_ref[...]


def matmul(x, w, *, bm, bn, out_dtype, act=None, name, n=None, w_row=None):
    m, k = x.shape
    w_t = w_row is not None
    n = w.shape[1] if n is None else n
    bm, bn = min(bm, m), min(bn, n)
    assert m % bm == 0 and n % bn == 0
    osz = jnp.dtype(out_dtype).itemsize
    vmem = 2 * (bm * k * 2 + k * bn * 2 + bm * bn * osz) + bm * bn * 4 + (4 << 20)
    if w_t:
        assert w_row % bn == 0
        off = w_row // bn
        w_spec = pl.BlockSpec((bn, k), lambda i, j: (off + j, 0))
    else:
        w_spec = pl.BlockSpec((k, bn), lambda i, j: (0, j))
    return pl.pallas_call(
        functools.partial(_mm_body, act=act, w_t=w_t),
        grid=(m // bm, n // bn),
        in_specs=[pl.BlockSpec((bm, k), lambda i, j: (i, 0)), w_spec],
        out_specs=pl.BlockSpec((bm, bn), lambda i, j: (i, j)),
        out_shape=jax.ShapeDtypeStruct((m, n), out_dtype),
        compiler_params=_cparams(("parallel", "arbitrary"), vmem),
        name=name,
    )(x, w)


def matmul_residual(x, w, res, *, bm, bn, bk, name):
    m, k = x.shape
    n = w.shape[1]
    bm, bn, bk = min(bm, m), min(bn, n), min(bk, k)
    assert m % bm == 0 and n % bn == 0 and k % bk == 0
    if bk == k:
        vmem = 2 * (bm * k * 2 + k * bn * 2 + 2 * bm * bn * 4) + bm * bn * 4 + (4 << 20)
        return pl.pallas_call(
            _mm_res_body,
            grid=(m // bm, n // bn),
            in_specs=[pl.BlockSpec((bm, k), lambda i, j: (i, 0)),
                      pl.BlockSpec((k, bn), lambda i, j: (0, j)),
                      pl.BlockSpec((bm, bn), lambda i, j: (i, j))],
            out_specs=pl.BlockSpec((bm, bn), lambda i, j: (i, j)),
            out_shape=jax.ShapeDtypeStruct((m, n), jnp.float32),
            compiler_params=_cparams(("parallel", "arbitrary"), vmem),
            name=name,
        )(x, w, res)
    vmem = 2 * (bm * bk * 2 + bk * bn * 2 + 2 * bm * bn * 4) + 2 * bm * bn * 4 + (4 << 20)
    return pl.pallas_call(
        _mm_res_k_body,
        grid=(m // bm, n // bn, k // bk),
        in_specs=[pl.BlockSpec((bm, bk), lambda i, j, kk: (i, kk)),
                  pl.BlockSpec((bk, bn), lambda i, j, kk: (kk, j)),
                  pl.BlockSpec((bm, bn), lambda i, j, kk: (i, j))],
        out_specs=pl.BlockSpec((bm, bn), lambda i, j, kk: (i, j)),
        out_shape=jax.ShapeDtypeStruct((m, n), jnp.float32),
        scratch_shapes=[pltpu.VMEM((bm, bn), jnp.float32)],
        compiler_params=_cparams(("parallel", "arbitrary", "arbitrary"), vmem),
        name=name,
    )(x, w, res)


def _merge_body(x_ref, wg_ref, y_ref, wu_ref, o_ref, acc_ref):
    br = pl.program_id(2)
    g = lax.dot_general(x_ref[...], wg_ref[...], _NT, preferred_element_type=jnp.float32)
    u = jnp.dot(y_ref[0], wu_ref[0], preferred_element_type=jnp.float32)
    contrib = u / (1.0 + jnp.exp(-g))

    @pl.when(br == 0)
    def _():
        acc_ref[...] = contrib

    @pl.when(br > 0)
    def _():
        acc_ref[...] += contrib

    @pl.when(br == pl.num_programs(2) - 1)
    def _():
        o_ref[...] = acc_ref[...].astype(o_ref.dtype)


def gated_merge(xb, w_all, y, wu, *, bm, bn):
    t, d = xb.shape
    nbr, _, kin = y.shape
    bm, bn = min(bm, t), min(bn, d)
    assert WA_G % bn == 0 and d % bn == 0
    goff, gstride = WA_G // bn, d // bn
    vmem = 2 * (bm * d * 2 + d * bn * 2 + bm * kin * 2 + kin * bn * 2 + bm * bn * 2) \
        + 4 * bm * bn * 4 + (4 << 20)
    return pl.pallas_call(
        _merge_body,
        grid=(t // bm, d // bn, nbr),
        in_specs=[pl.BlockSpec((bm, d), lambda i, j, b: (i, 0)),
                  pl.BlockSpec((bn, d), lambda i, j, b: (goff + b * gstride + j, 0)),
                  pl.BlockSpec((1, bm, kin), lambda i, j, b: (b, i, 0)),
                  pl.BlockSpec((1, kin, bn), lambda i, j, b: (b, 0, j))],
        out_specs=pl.BlockSpec((bm, bn), lambda i, j, b: (i, j)),
        out_shape=jax.ShapeDtypeStruct((t, d), jnp.bfloat16),
        scratch_shapes=[pltpu.VMEM((bm, bn), jnp.float32)],
        compiler_params=_cparams(("parallel", "arbitrary", "arbitrary"), vmem),
        name="gated_merge",
    )(xb, w_all, y, wu)


def _ln_body(r_ref, g_ref, b_ref, of_ref, ob_ref):
    r = r_ref[...]
    mu = jnp.mean(r, axis=-1, keepdims=True)
    xc = r - mu
    var = jnp.mean(xc * xc, axis=-1, keepdims=True)
    y = xc * lax.rsqrt(var + LN_EPS) * g_ref[...] + b_ref[...]
    of_ref[...] = y
    ob_ref[...] = y.astype(jnp.bfloat16)


def layer_norm(r, g, b, *, bm=256):
    t, d = r.shape
    bm = min(bm, t)
    vmem = 2 * (bm * d * 4 * 2 + bm * d * 2) + 4 * bm * d * 4 + (4 << 20)
    return pl.pallas_call(
        _ln_body,
        grid=(t // bm,),
        in_specs=[pl.BlockSpec((bm, d), lambda i: (i, 0)),
                  pl.BlockSpec((1, d), lambda i: (0, 0)),
                  pl.BlockSpec((1, d), lambda i: (0, 0))],
        out_specs=[pl.BlockSpec((bm, d), lambda i: (i, 0)),
                   pl.BlockSpec((bm, d), lambda i: (i, 0))],
        out_shape=[jax.ShapeDtypeStruct((t, d), jnp.float32),
                   jax.ShapeDtypeStruct((t, d), jnp.bfloat16)],
        compiler_params=_cparams(("parallel",), vmem),
        name="layer_norm",
    )(r, g.reshape(1, d), b.reshape(1, d))


def _swa_body(sink_ref, bias_ref, q_ref, kp_ref, kc_ref, vp_ref, vc_ref, o_ref):
    blk = SWA_BLK
    c2 = SWA_HEAD_DIM ** -0.5 * LOG2E
    lo = lax.broadcasted_iota(jnp.int32, (2 * blk, LANES), 1) < SWA_HEAD_DIM
    group = SWA_HEADS // SWA_KV_HEADS
    pairs = SWA_HEADS // 2
    kz, vz = [], []
    for g in range(SWA_KV_HEADS):
        kd = jnp.concatenate([kp_ref[:, g * LANES:(g + 1) * LANES],
                              kc_ref[:, g * LANES:(g + 1) * LANES]], axis=0)
        vd = jnp.concatenate([vp_ref[:, g * LANES:(g + 1) * LANES],
                              vc_ref[:, g * LANES:(g + 1) * LANES]], axis=0)
        zero = jnp.zeros_like(kd)
        kz.append(jnp.concatenate([jnp.where(lo, kd, zero), jnp.where(lo, zero, kd)], axis=0))
        vz.append(jnp.concatenate([jnp.where(lo, vd, zero), jnp.where(lo, zero, vd)], axis=0))
    s2 = [lax.dot_general(q_ref[:, p * LANES:(p + 1) * LANES], kz[p // (group // 2)],
                          (((1,), (1,)), ((), ())), preferred_element_type=jnp.float32)
          for p in range(pairs)]
    for p in range(pairs):
        probs = []
        for e in range(2):
            h = 2 * p + e
            sink = sink_ref[h] * LOG2E
            s = s2[p][:, e * 2 * blk:(e + 1) * 2 * blk] * c2 + bias_ref[h]
            m = jnp.maximum(jnp.max(s, axis=-1, keepdims=True), sink)
            pe = jnp.exp2(s - m)
            den = jnp.sum(pe, axis=-1, keepdims=True) + jnp.exp2(sink - m)
            probs.append((pe * (1.0 / den)).astype(jnp.bfloat16))
        p2 = jnp.concatenate(probs, axis=1)
        o_ref[:, p * LANES:(p + 1) * LANES] = jnp.dot(
            p2, vz[p // (group // 2)], preferred_element_type=jnp.float32).astype(o_ref.dtype)


def _swa_bias_table():
    blk = SWA_BLK
    qi = lax.broadcasted_iota(jnp.int32, (blk, 2 * blk), 0)
    kj = lax.broadcasted_iota(jnp.int32, (blk, 2 * blk), 1)
    dist = qi + blk - kj
    window = (dist >= 0) & (dist < SWA_WINDOW)
    slopes = 2.0 ** (-8.0 * jnp.arange(1, SWA_HEADS + 1, dtype=jnp.float32) / SWA_HEADS)
    bias = -(LOG2E * slopes)[:, None, None] * dist.astype(jnp.float32)[None]
    later = jnp.where(window[None], bias, NEG_INF)
    first = jnp.where((window & (kj >= blk))[None], bias, NEG_INF)
    return jnp.stack([first, later])


def swa_attention(zb, sinks, bsz, seq):
    blk = SWA_BLK
    nb = seq // blk
    t = bsz * seq
    kcol, vcol = ZB_AK // (2 * LANES), ZB_AV // (2 * LANES)

    def cur(b, n):
        return b * nb + n

    def prev(b, n):
        return b * nb + jnp.maximum(n - 1, 0)

    return pl.pallas_call(
        _swa_body,
        grid=(bsz, nb),
        in_specs=[pl.BlockSpec(memory_space=pltpu.SMEM),
                  pl.BlockSpec((None, SWA_HEADS, blk, 2 * blk),
                               lambda b, n: (jnp.minimum(n, 1), 0, 0, 0)),
                  pl.BlockSpec((blk, A_Q), lambda b, n: (cur(b, n), ZB_AQ // A_Q)),
                  pl.BlockSpec((blk, 2 * LANES), lambda b, n: (prev(b, n), kcol)),
                  pl.BlockSpec((blk, 2 * LANES), lambda b, n: (cur(b, n), kcol)),
                  pl.BlockSpec((blk, 2 * LANES), lambda b, n: (prev(b, n), vcol)),
                  pl.BlockSpec((blk, 2 * LANES), lambda b, n: (cur(b, n), vcol))],
        out_specs=pl.BlockSpec((None, blk, A_Q), lambda b, n: (0, cur(b, n), 0)),
        out_shape=jax.ShapeDtypeStruct((3, t, A_Q), jnp.bfloat16),
        compiler_params=_cparams(("parallel", "arbitrary"), 32 << 20),
        name="swa_attention",
    )(sinks, _swa_bias_table(), zb, zb, zb, zb, zb)


def _mlstm_body(bias_ref, q_ref, k_ref, v_ref, og_ref, g_ref, ybuf_ref, y_ref, cum_ref,
                c_scr, n_scr, m_scr, carry_scr):
    del ybuf_ref
    c = pl.program_id(1)
    L = MLSTM_L
    dk, dv = MLSTM_QK_DIM, MLSTM_V_DIM

    @pl.when(c == 0)
    def _():
        c_scr[...] = jnp.zeros_like(c_scr)
        n_scr[...] = jnp.zeros_like(n_scr)
        m_scr[...] = jnp.zeros_like(m_scr)
        carry_scr[...] = jnp.zeros_like(carry_scr)

    a = g_ref[...] + bias_ref[...]
    lf = jnp.minimum(a, 0.0) - jnp.log1p(jnp.exp(-jnp.abs(a)))
    row = lax.broadcasted_iota(jnp.int32, (L, L), 0)
    col = lax.broadcasted_iota(jnp.int32, (L, L), 1)
    causal = row >= col
    tri = causal.astype(jnp.float32)
    b_all = jnp.dot(tri, lf, preferred_element_type=jnp.float32,
                    precision=lax.Precision.HIGHEST)
    cum = b_all + carry_scr[0:1, :]
    carry_scr[0:1, :] = cum[L - 1:L, :]
    cum_ref[0] = cum.T
    a_t = a.T
    b_t = b_all.T

    lane = lax.broadcasted_iota(jnp.int32, (L, LANES), 1)
    lo = lane < dk
    crow = lax.broadcasted_iota(jnp.int32, (2 * dk, 2 * dv), 0)
    ccol = lax.broadcasted_iota(jnp.int32, (2 * dk, 2 * dv), 1)
    crow_lo = crow < dk
    blockdiag = crow_lo == (ccol < dv)
    nlane_lo = lax.broadcasted_iota(jnp.int32, (1, LANES), 1) < dk

    qps, kps, scs, cps, qcs = [], [], [], [], []
    for p in range(MLSTM_HEADS // 2):
        qp = q_ref[:, p * LANES:(p + 1) * LANES]
        kp = k_ref[:, p * LANES:(p + 1) * LANES] * (dk ** -0.5)
        kpb = kp.astype(jnp.bfloat16)
        zero = jnp.zeros_like(kpb)
        kz = jnp.concatenate([jnp.where(lo, kpb, zero), jnp.where(lo, zero, kpb)], axis=0)
        scs.append(lax.dot_general(qp, kz, (((1,), (1,)), ((), ())),
                                   preferred_element_type=jnp.float32))
        cp = c_scr[p]
        qcs.append(jnp.dot(qp, cp.astype(jnp.bfloat16), preferred_element_type=jnp.float32))
        qps.append(qp); kps.append(kp); cps.append(cp)

    for p in range(MLSTM_HEADS // 2):
        qp, kp, sc, cp, qc = qps[p], kps[p], scs[p], cps[p], qcs[p]
        n_row = n_scr[p:p + 1, :]
        qn_prod = qp.astype(jnp.float32) * n_row
        wk_cols, decays = [], []
        for e in range(2):
            h = 2 * p + e
            b_col = b_all[:, 8 + h:9 + h]
            b_row = b_t[8 + h:9 + h, :]
            ig_row = a_t[h:h + 1, :]
            ig_col = a[:, h:h + 1]
            m_prev = m_scr[h:h + 1, 0:1]
            dmat = jnp.where(causal, b_col - b_row + ig_row, NEG_INF)
            inter = b_col + m_prev
            m_t = jnp.maximum(inter, jnp.max(dmat, axis=-1, keepdims=True))
            w_inter = jnp.exp(inter - m_t)
            smat = sc[:, e * L:(e + 1) * L] * jnp.exp(dmat - m_t)
            qn = jnp.sum(jnp.where(lo == (e == 0), qn_prod, 0.0), axis=-1, keepdims=True)
            num = w_inter * qc[:, e * dv:(e + 1) * dv] + jnp.dot(
                smat.astype(jnp.bfloat16), v_ref[:, h * dv:(h + 1) * dv],
                preferred_element_type=jnp.float32)
            den = w_inter * qn + jnp.sum(smat, axis=-1, keepdims=True)
            hh = num / jnp.maximum(jnp.abs(den), jnp.exp(-m_t))
            og = og_ref[:, h * dv:(h + 1) * dv]
            y_ref[:, h * dv:(h + 1) * dv] = (hh / (1.0 + jnp.exp(-og))).astype(y_ref.dtype)
            m_new = m_t[L - 1:L, :]
            b_last = b_col[L - 1:L, :]
            decays.append(jnp.exp(b_last + m_prev - m_new))
            wk_cols.append(jnp.exp(b_last - b_col + ig_col - m_new))
            m_scr[h:h + 1, :] = jnp.broadcast_to(m_new, (1, LANES))
        kw = kp * jnp.where(lo, wk_cols[0], wk_cols[1])
        n_scr[p:p + 1, :] = (jnp.where(nlane_lo, decays[0], decays[1]) * n_row
                             + jnp.sum(kw, axis=0, keepdims=True))
        upd = jnp.dot(kw.T.astype(jnp.bfloat16), v_ref[:, 2 * p * dv:(2 * p + 2) * dv],
                      preferred_element_type=jnp.float32)
        c_scr[p] = jnp.where(crow_lo, decays[0], decays[1]) * cp + jnp.where(blockdiag, upd, 0.0)


def mlstm(zb, zf, bias_row, ybuf, bsz, seq):
    L = MLSTM_L
    nc = seq // L

    def rows(b, c):
        return b * nc + c

    return pl.pallas_call(
        _mlstm_body,
        grid=(bsz, nc),
        in_specs=[pl.BlockSpec((1, LANES), lambda b, c: (0, 0)),
                  pl.BlockSpec((L, B_QK), lambda b, c: (rows(b, c), ZB_MQ // B_QK)),
                  pl.BlockSpec((L, B_QK), lambda b, c: (rows(b, c), ZF_MK // B_QK)),
                  pl.BlockSpec((L, B_V), lambda b, c: (rows(b, c), ZB_MV // B_V)),
                  pl.BlockSpec((L, B_V), lambda b, c: (rows(b, c), ZF_MO // B_V)),
                  pl.BlockSpec((L, LANES), lambda b, c: (rows(b, c), ZF_G // LANES)),
                  pl.BlockSpec(memory_space=pl.ANY)],
        out_specs=[pl.BlockSpec((None, L, B_V), lambda b, c: (1, rows(b, c), 0)),
                   pl.BlockSpec((1, LANES, L), lambda b, c: (b, 0, c))],
        out_shape=[jax.ShapeDtypeStruct(ybuf.shape, ybuf.dtype),
                   jax.ShapeDtypeStruct((bsz, LANES, seq), jnp.float32)],
        input_output_aliases={6: 0},
        scratch_shapes=[pltpu.VMEM((MLSTM_HEADS // 2, 2 * MLSTM_QK_DIM, 2 * MLSTM_V_DIM), jnp.float32),
                        pltpu.VMEM((8, LANES), jnp.float32),
                        pltpu.VMEM((8, LANES), jnp.float32),
                        pltpu.VMEM((8, LANES), jnp.float32)],
        compiler_params=_cparams(("arbitrary", "arbitrary"), 32 << 20),
        name="mlstm",
    )(bias_row, zb, zf, zb, zf, zf, ybuf)


def _fox_body(q_ref, k_ref, v_ref, ck_ref, ybuf_ref, o_ref):
    del ybuf_ref
    hp = pl.program_id(1)
    qi = pl.program_id(2)
    blk = FOX_BLK
    dh = FOX_HEAD_DIM
    c2 = FOX_HEAD_DIM ** -0.5 * LOG2E

    def scores(j, e):
        start = pl.multiple_of(j * blk, blk)
        s = lax.dot_general(q_ref[:, e * dh:(e + 1) * dh], k_ref[pl.ds(start, blk), e * dh:(e + 1) * dh],
                            (((1,), (1,)), ((), ())), preferred_element_type=jnp.float32)
        return s * c2 - ck_ref[0, 2 * hp + e, pl.ds(j, 1), :] * LOG2E

    def update(s, j, e, m, l, acc):
        start = pl.multiple_of(j * blk, blk)
        m_new = jnp.maximum(m, jnp.max(s, axis=-1, keepdims=True))
        alpha = jnp.exp2(m - m_new)
        p = jnp.exp2(s - m_new)
        l = alpha * l + jnp.sum(p, axis=-1, keepdims=True)
        acc = alpha * acc + jnp.dot(p.astype(jnp.bfloat16),
                                    v_ref[pl.ds(start, blk), e * dh:(e + 1) * dh],
                                    preferred_element_type=jnp.float32)
        return m_new, l, acc

    def step(j, carry):
        s = [scores(j, e) for e in range(2)]
        return tuple(update(s[e], j, e, *carry[e]) for e in range(2))

    init1 = (jnp.full((blk, 1), NEG_INF, jnp.float32), jnp.zeros((blk, 1), jnp.float32),
             jnp.zeros((blk, dh), jnp.float32))
    carry = lax.fori_loop(0, qi, step, (init1, init1))
    row = lax.broadcasted_iota(jnp.int32, (blk, blk), 0)
    colm = lax.broadcasted_iota(jnp.int32, (blk, blk), 1)
    s = [jnp.where(colm <= row, scores(qi, e), NEG_INF) for e in range(2)]
    for e in range(2):
        _, l, acc = update(s[e], qi, e, *carry[e])
        o_ref[:, e * dh:(e + 1) * dh] = (acc / l).astype(o_ref.dtype)


def fox_attention(zb, cum_t, ybuf, bsz, seq):
    blk = FOX_BLK
    nq = seq // blk
    pw = 2 * FOX_HEAD_DIM
    ck = cum_t.reshape(bsz, LANES, nq, blk)
    return pl.pallas_call(
        _fox_body,
        grid=(bsz, FOX_HEADS // 2, nq),
        in_specs=[pl.BlockSpec((blk, pw), lambda b, h, i: (b * nq + i, ZB_CQ // pw + h)),
                  pl.BlockSpec((seq, pw), lambda b, h, i: (b, ZB_CK // pw + h)),
                  pl.BlockSpec((seq, pw), lambda b, h, i: (b, ZB_CV // pw + h)),
                  pl.BlockSpec((1, 8, nq, blk), lambda b, h, i: (b, 2, 0, 0)),
                  pl.BlockSpec(memory_space=pl.ANY)],
        out_specs=pl.BlockSpec((None, blk, pw), lambda b, h, i: (2, b * nq + i, h)),
        out_shape=jax.ShapeDtypeStruct(ybuf.shape, ybuf.dtype),
        input_output_aliases={4: 0},
        compiler_params=_cparams(("parallel", "arbitrary", "arbitrary"), 32 << 20),
        name="fox_attention",
    )(zb, zb, zb, ck, ybuf)


RG_COPY, RG_DUP, RG_GATES = 0, 1, 2
SUBLANES = 8
_SEG = dict(a_q=0, a_k=1, a_v=2, m_q=3, m_k=4, m_v=5, m_i=6, m_f=7, m_o=8,
            c_q=9, c_k=10, c_v=11, c_f=12, g_a=13, g_b=14, g_c=15)


def _regroup_table():
    tiles = []

    def copy(name):
        off, width = SEG_OFFS[_SEG[name]], SEG_WIDTHS[_SEG[name]]
        assert width % WT == 0 and off % SUBLANES == 0
        tiles.extend((off + k * WT, RG_COPY) for k in range(width // WT))

    copy("m_o"); copy("m_k")
    m_i, m_f, c_f = (SEG_OFFS[_SEG[n]] for n in ("m_i", "m_f", "c_f"))
    assert m_f == m_i + MLSTM_HEADS and m_i % SUBLANES == 0 and c_f % SUBLANES == 0
    tiles.append((m_i, RG_GATES))
    assert len(tiles) * WT == WA_ZB
    for name in ("a_q", "m_v", "c_q", "c_k", "c_v", "m_q"):
        copy(name)
    a_k, a_v = SEG_OFFS[_SEG["a_k"]], SEG_OFFS[_SEG["a_v"]]
    assert a_v == a_k + A_KV and a_k % SUBLANES == 0 and 4 * A_KV == WT
    tiles.append((a_k, RG_DUP))
    assert len(tiles) * WT == WA_G
    for name in ("g_a", "g_b", "g_c"):
        copy(name)
    assert len(tiles) * WT == WA_N
    return tiles, c_f


def _regroup_body(base_ref, mode_ref, w_ref, aux_ref, o_ref):
    del base_ref
    mode = mode_ref[pl.program_id(0)]
    bf = jnp.bfloat16
    hd = SWA_HEAD_DIM

    @pl.when(mode == RG_COPY)
    def _():
        o_ref[...] = w_ref[0].astype(bf)

    @pl.when(mode == RG_DUP)
    def _():
        for i in range(2 * SWA_KV_HEADS):
            head = w_ref[0, i * hd:(i + 1) * hd, :].astype(bf)
            o_ref[2 * i * hd:(2 * i + 1) * hd, :] = head
            o_ref[(2 * i + 1) * hd:(2 * i + 2) * hd, :] = head

    @pl.when(mode == RG_GATES)
    def _():
        nh = MLSTM_HEADS
        o_ref[0:2 * nh, :] = w_ref[0, 0:2 * nh, :].astype(bf)
        o_ref[2 * nh:3 * nh, :] = aux_ref[0].astype(bf)
        o_ref[3 * nh:WT, :] = jnp.zeros((WT - 3 * nh, o_ref.shape[1]), bf)


def regroup_w_in(w_in, l):
    tiles, c_f = _regroup_table()
    base = jnp.asarray([b for b, _ in tiles], jnp.int32)
    mode = jnp.asarray([m for _, m in tiles], jnp.int32)
    wt = jnp.swapaxes(w_in, 1, 2)
    k = wt.shape[2]
    el = pl.Element
    in_specs = [pl.BlockSpec((el(1), el(WT), el(k)),
                             lambda t, base, mode: (l, pl.multiple_of(base[t], SUBLANES), 0)),
                pl.BlockSpec((el(1), el(FOX_HEADS), el(k)), lambda t, base, mode: (l, c_f, 0))]
    vmem = 2 * (WT * k * 4 + WT * k * 2) + (12 << 20)
    return pl.pallas_call(
        _regroup_body,
        grid_spec=pltpu.PrefetchScalarGridSpec(
            num_scalar_prefetch=2, grid=(len(tiles),), in_specs=in_specs,
            out_specs=pl.BlockSpec((WT, k), lambda t, base, mode: (t, 0))),
        out_shape=jax.ShapeDtypeStruct((WA_N, k), jnp.bfloat16),
        compiler_params=_cparams(("arbitrary",), vmem),
        name="regroup_w_in",
    )(base, mode, wt, wt)


def _cast_body(w_ref, o_ref):
    o_ref[...] = w_ref[...].astype(o_ref.dtype)


def cast_bf16(w, l=None, *, br=1024, bc=2048):
    r, c = w.shape[-2:]
    br, bc = min(br, r), min(bc, c)
    assert r % br == 0 and c % bc == 0
    if l is None:
        in_spec = pl.BlockSpec((br, bc), lambda i, j: (i, j))
    else:
        in_spec = pl.BlockSpec((None, br, bc), lambda i, j: (l, i, j))
    return pl.pallas_call(
        _cast_body,
        grid=(r // br, c // bc),
        in_specs=[in_spec],
        out_specs=pl.BlockSpec((br, bc), lambda i, j: (i, j)),
        out_shape=jax.ShapeDtypeStruct((r, c), jnp.bfloat16),
        compiler_params=_cparams(("parallel", "parallel"), 2 * br * bc * 6 + (8 << 20)),
        name="cast_bf16",
    )(w)


def _cast3_body(a_ref, b_ref, c_ref, o_ref):
    br = pl.program_id(0)
    for idx, ref in enumerate((a_ref, b_ref, c_ref)):
        @pl.when(br == idx)
        def _(ref=ref):
            o_ref[...] = ref[...].astype(o_ref.dtype)


def cast_stack3(a, b, c, l, *, bc=1024):
    r, cols = a.shape[-2:]
    assert cols % bc == 0

    def spec(idx):
        return pl.BlockSpec((None, r, bc), lambda s, j: (l, 0, jnp.where(s == idx, j, 0)))

    return pl.pallas_call(
        _cast3_body,
        grid=(3, cols // bc),
        in_specs=[spec(0), spec(1), spec(2)],
        out_specs=pl.BlockSpec((None, r, bc), lambda s, j: (s, 0, j)),
        out_shape=jax.ShapeDtypeStruct((3, r, cols), jnp.bfloat16),
        compiler_params=_cparams(("arbitrary", "arbitrary"), 2 * r * bc * 14 + (8 << 20)),
        name="cast_stack3",
    )(a, b, c)


def kernel(x, w_in, b_mlstm_i, b_mlstm_f, b_fox_f, attn_sinks, w_up_swa, w_up_mlstm, w_up_fox,
           w_o, ln1_g, ln1_b, w_ff1, w_ff2, ln2_g, ln2_b):
    bsz, seq, d = x.shape
    t = bsz * seq
    assert d == D_MODEL and seq % FOX_BLK == 0 and seq % MLSTM_L == 0
    xf = x.reshape(t, d).astype(jnp.float32)
    xb = cast_bf16(xf)
    gate_pad = jnp.zeros((LANES - 3 * MLSTM_HEADS,), jnp.float32)
    for l in range(DEPTH):
        w_all = regroup_w_in(w_in, l)
        w_up = cast_stack3(w_up_swa, w_up_mlstm, w_up_fox, l)
        w_ob, w_1b, w_2b = cast_bf16(w_o, l), cast_bf16(w_ff1, l), cast_bf16(w_ff2, l)
        bias_row = jnp.concatenate([b_mlstm_i[l], b_mlstm_f[l], b_fox_f[l], gate_pad]).reshape(1, LANES)
        zf = matmul(xb, w_all, n=ZF_N, w_row=WA_ZF, bm=512, bn=ZF_N, out_dtype=jnp.float32,
                    name="in_proj_f32")
        zb = matmul(xb, w_all, n=ZB_N, w_row=WA_ZB, bm=1024, bn=1024, out_dtype=jnp.bfloat16,
                    name="in_proj_bf16")
        ybuf = swa_attention(zb, attn_sinks[l].astype(jnp.float32), bsz, seq)
        ybuf, cum_t = mlstm(zb, zf, bias_row.astype(jnp.float32), ybuf, bsz, seq)
        ybuf = fox_attention(zb, cum_t, ybuf, bsz, seq)
        mix = gated_merge(xb, w_all, ybuf, w_up, bm=1024, bn=512)
        r1 = matmul_residual(mix, w_ob, xf, bm=1024, bn=512, bk=D_MODEL, name="out_proj")
        xf, xb = layer_norm(r1, ln1_g[l], ln1_b[l])
        hid = matmul(xb, w_1b, bm=1024, bn=1024, out_dtype=jnp.bfloat16, act="relu2", name="ff1")
        r2 = matmul_residual(hid, w_2b, xf, bm=1024, bn=1024, bk=2048, name="ff2")
        xf, xb = layer_norm(r2, ln2_g[l], ln2_b[l])
    return xf.reshape(bsz, seq, d).astype(x.dtype)
```

```python
import functools

import jax
import jax.numpy as jnp
from jax import lax
from jax.experimental import pallas as pl
from jax.experimental.pallas import tpu as pltpu

D_MODEL = 4096
SWA_HEADS, SWA_KV_HEADS, SWA_HEAD_DIM, SWA_WINDOW = 16, 2, 64, 128
MLSTM_HEADS, MLSTM_QK_DIM, MLSTM_V_DIM = 8, 64, 128
FOX_HEADS, FOX_HEAD_DIM = 8, 128
D_FF = 4 * D_MODEL
LN_EPS = 1e-5
DEPTH = 2
DN_ALPHA = (2 * DEPTH) ** 0.25

A_Q = SWA_HEADS * SWA_HEAD_DIM
A_KV = SWA_KV_HEADS * SWA_HEAD_DIM
B_QK = MLSTM_HEADS * MLSTM_QK_DIM
B_V = MLSTM_HEADS * MLSTM_V_DIM
C_W = FOX_HEADS * FOX_HEAD_DIM
SEG_WIDTHS = (A_Q, A_KV, A_KV, B_QK, B_QK, B_V, MLSTM_HEADS, MLSTM_HEADS, B_V,
              C_W, C_W, C_W, FOX_HEADS, D_MODEL, D_MODEL, D_MODEL)
SEG_OFFS = tuple(sum(SEG_WIDTHS[:i]) for i in range(len(SEG_WIDTHS)))

LANES = 128
VMEM_LIMIT_CAP = 56 * 1024 * 1024

ZB_AQ = 0
ZB_MV = 1024
ZB_CQ = 2048
ZB_CK = 3072
ZB_CV = 4096
ZB_MQ = 5120
ZB_AK = 5632
ZB_AV = 5888
ZB_N = 6144
ZF_MO = 0
ZF_MK = 1024
ZF_G = 1536
ZF_N = 1664
WT = 512
WA_ZF = 0
WA_ZB = 2048
WA_G = WA_ZB + ZB_N
WA_N = WA_G + 3 * D_MODEL

MLSTM_L = 256
FOX_BLK = 512
SWA_BLK = 128
NEG_INF = float("-inf")
LOG2E = 1.4426950408889634


def _cparams(sem, vmem_bytes):
    return pltpu.CompilerParams(dimension_semantics=sem,
                                vmem_limit_bytes=int(min(vmem_bytes, VMEM_LIMIT_CAP)))


_NT = (((1,), (1,)), ((), ()))


def _mm_body(x_ref, w_ref, o_ref, *, act, w_t):
    if w_t:
        acc = lax.dot_general(x_ref[...], w_ref[...], _NT, preferred_element_type=jnp.float32)
    else:
        acc = jnp.dot(x_ref[...], w_ref[...], preferred_element_type=jnp.float32)
    if act == "relu2":
        acc = jnp.square(jnp.maximum(acc, 0.0))
    o_ref[...] = acc.astype(o_ref.dtype)


def _mm_res_body(x_ref, w_ref, r_ref, o_ref):
    acc = jnp.dot(x_ref[...], w_ref[...], preferred_element_type=jnp.float32)
    o_ref[...] = DN_ALPHA * r_ref[...] + acc


def _mm_res_k_body(x_ref, w_ref, r_ref, o_ref):
    k = pl.program_id(2)

    @pl.when(k == 0)
    def _():
        o_ref[...] = DN_ALPHA * r_ref[...] + jnp.dot(x_ref[...], w_ref[...],
                                                     preferred_element_type=jnp.float32)

    @pl.when(k > 0)
    def _():
        o_ref[...] += jnp.dot(x_ref[...], w_ref[...], preferred_element_type=jnp.float32)


def matmul(x, w, *, bm, bn, out_dtype, act=None, name, n=None, w_row=None):
    m, k = x.shape
    w_t = w_row is not None
    n = w.shape[1] if n is None else n
    bm, bn = min(bm, m), min(bn, n)
    assert m % bm == 0 and n % bn == 0
    osz = jnp.dtype(out_dtype).itemsize
    vmem = 2 * (bm * k * 2 + k * bn * 2 + bm * bn * osz) + bm * bn * 4 + (4 << 20)
    if w_t:
        assert w_row % bn == 0
        off = w_row // bn
        w_spec = pl.BlockSpec((bn, k), lambda i, j: (off + j, 0))
    else:
        w_spec = pl.BlockSpec((k, bn), lambda i, j: (0, j))
    return pl.pallas_call(
        functools.partial(_mm_body, act=act, w_t=w_t),
        grid=(m // bm, n // bn),
        in_specs=[pl.BlockSpec((bm, k), lambda i, j: (i, 0)), w_spec],
        out_specs=pl.BlockSpec((bm, bn), lambda i, j: (i, j)),
        out_shape=jax.ShapeDtypeStruct((m, n), out_dtype),
        compiler_params=_cparams(("parallel", "arbitrary"), vmem),
        name=name,
    )(x, w)


def matmul_residual(x, w, res, *, bm, bn, bk, name):
    m, k = x.shape
    n = w.shape[1]
    bm, bn, bk = min(bm, m), min(bn, n), min(bk, k)
    assert m % bm == 0 and n % bn == 0 and k % bk == 0
    if bk == k:
        vmem = 2 * (bm * k * 2 + k * bn * 2 + 2 * bm * bn * 4) + bm * bn * 4 + (4 << 20)
        return pl.pallas_call(
            _mm_res_body,
            grid=(m // bm, n // bn),
            in_specs=[pl.BlockSpec((bm, k), lambda i, j: (i, 0)),
                      pl.BlockSpec((k, bn), lambda i, j: (0, j)),
                      pl.BlockSpec((bm, bn), lambda i, j: (i, j))],
            out_specs=pl.BlockSpec((bm, bn), lambda i, j: (i, j)),
            out_shape=jax.ShapeDtypeStruct((m, n), jnp.float32),
            compiler_params=_cparams(("parallel", "arbitrary"), vmem),
            name=name,
        )(x, w, res)
    vmem = 2 * (bm * bk * 2 + bk * bn * 2 + 2 * bm * bn * 4) + 2 * bm * bn * 4 + (4 << 20)
    return pl.pallas_call(
        _mm_res_k_body,
        grid=(m // bm, n // bn, k // bk),
        in_specs=[pl.BlockSpec((bm, bk), lambda i, j, kk: (i, kk)),
                  pl.BlockSpec((bk, bn), lambda i, j, kk: (kk, j)),
                  pl.BlockSpec((bm, bn), lambda i, j, kk: (i, j))],
        out_specs=pl.BlockSpec((bm, bn), lambda i, j, kk: (i, j)),
        out_shape=jax.ShapeDtypeStruct((m, n), jnp.float32),
        compiler_params=_cparams(("parallel", "arbitrary", "arbitrary"), vmem),
        name=name,
    )(x, w, res)


def _merge_body(x_ref, wg_ref, y_ref, wu_ref, o_ref, acc_ref):
    br = pl.program_id(2)
    last = pl.num_programs(2) - 1

    def contrib():
        g = lax.dot_general(x_ref[...], wg_ref[...], _NT, preferred_element_type=jnp.float32)
        u = jnp.dot(y_ref[0], wu_ref[0], preferred_element_type=jnp.float32)
        return u * (1.0 / (1.0 + jnp.exp(-g)))

    @pl.when(br == 0)
    def _():
        acc_ref[...] = contrib()

    @pl.when((br > 0) & (br < last))
    def _():
        acc_ref[...] += contrib()

    @pl.when(br == last)
    def _():
        o_ref[...] = (acc_ref[...] + contrib()).astype(o_ref.dtype)


def gated_merge(xb, w_all, y, wu, *, bm, bn):
    t, d = xb.shape
    nbr, _, kin = y.shape
    assert nbr >= 2
    bm, bn = min(bm, t), min(bn, d)
    assert WA_G % bn == 0 and d % bn == 0
    goff, gstride = WA_G // bn, d // bn
    vmem = 2 * (bm * d * 2 + d * bn * 2 + bm * kin * 2 + kin * bn * 2 + bm * bn * 2) \
        + 4 * bm * bn * 4 + (4 << 20)
    return pl.pallas_call(
        _merge_body,
        grid=(t // bm, d // bn, nbr),
        in_specs=[pl.BlockSpec((bm, d), lambda i, j, b: (i, 0)),
                  pl.BlockSpec((bn, d), lambda i, j, b: (goff + b * gstride + j, 0)),
                  pl.BlockSpec((1, bm, kin), lambda i, j, b: (b, i, 0)),
                  pl.BlockSpec((1, kin, bn), lambda i, j, b: (b, 0, j))],
        out_specs=pl.BlockSpec((bm, bn), lambda i, j, b: (i, j)),
        out_shape=jax.ShapeDtypeStruct((t, d), jnp.bfloat16),
        scratch_shapes=[pltpu.VMEM((bm, bn), jnp.float32)],
        compiler_params=_cparams(("parallel", "arbitrary", "arbitrary"), vmem),
        name="gated_merge",
    )(xb, w_all, y, wu)


def _ln_body(r_ref, g_ref, b_ref, of_ref, ob_ref):
    r = r_ref[...]
    mu = jnp.mean(r, axis=-1, keepdims=True)
    xc = r - mu
    var = jnp.mean(xc * xc, axis=-1, keepdims=True)
    y = xc * lax.rsqrt(var + LN_EPS) * g_ref[...] + b_ref[...]
    of_ref[...] = y
    ob_ref[...] = y.astype(jnp.bfloat16)


def layer_norm(r, g, b, *, bm=256):
    t, d = r.shape
    bm = min(bm, t)
    vmem = 2 * (bm * d * 4 * 2 + bm * d * 2) + 4 * bm * d * 4 + (4 << 20)
    return pl.pallas_call(
        _ln_body,
        grid=(t // bm,),
        in_specs=[pl.BlockSpec((bm, d), lambda i: (i, 0)),
                  pl.BlockSpec((1, d), lambda i: (0, 0)),
                  pl.BlockSpec((1, d), lambda i: (0, 0))],
        out_specs=[pl.BlockSpec((bm, d), lambda i: (i, 0)),
                   pl.BlockSpec((bm, d), lambda i: (i, 0))],
        out_shape=[jax.ShapeDtypeStruct((t, d), jnp.float32),
                   jax.ShapeDtypeStruct((t, d), jnp.bfloat16)],
        compiler_params=_cparams(("parallel",), vmem),
        name="layer_norm",
    )(r, g.reshape(1, d), b.reshape(1, d))


def _swa_body(sink_ref, bias_ref, q_ref, kp_ref, kc_ref, vp_ref, vc_ref, ybuf_ref, o_ref):
    del ybuf_ref
    blk = SWA_BLK
    c2 = SWA_HEAD_DIM ** -0.5 * LOG2E
    lo = lax.broadcasted_iota(jnp.int32, (2 * blk, LANES), 1) < SWA_HEAD_DIM
    group = SWA_HEADS // SWA_KV_HEADS
    pairs = SWA_HEADS // 2
    kz, vz = [], []
    for g in range(SWA_KV_HEADS):
        kd = jnp.concatenate([kp_ref[:, g * LANES:(g + 1) * LANES],
                              kc_ref[:, g * LANES:(g + 1) * LANES]], axis=0)
        vd = jnp.concatenate([vp_ref[:, g * LANES:(g + 1) * LANES],
                              vc_ref[:, g * LANES:(g + 1) * LANES]], axis=0)
        zero = jnp.zeros_like(kd)
        kz.append(jnp.concatenate([jnp.where(lo, kd, zero), jnp.where(lo, zero, kd)], axis=0))
        vz.append(jnp.concatenate([jnp.where(lo, vd, zero), jnp.where(lo, zero, vd)], axis=0))
    s2 = [lax.dot_general(q_ref[:, p * LANES:(p + 1) * LANES], kz[p // (group // 2)],
                          (((1,), (1,)), ((), ())), preferred_element_type=jnp.float32)
          for p in range(pairs)]
    for p in range(pairs):
        probs = []
        for e in range(2):
            h = 2 * p + e
            sink = sink_ref[h] * LOG2E
            s = s2[p][:, e * 2 * blk:(e + 1) * 2 * blk] * c2 + bias_ref[h]
            m = jnp.maximum(jnp.max(s, axis=-1, keepdims=True), sink)
            pe = jnp.exp2(s - m)
            den = jnp.sum(pe, axis=-1, keepdims=True) + jnp.exp2(sink - m)
            probs.append((pe * (1.0 / den)).astype(jnp.bfloat16))
        p2 = jnp.concatenate(probs, axis=1)
        o_ref[:, p * LANES:(p + 1) * LANES] = jnp.dot(
            p2, vz[p // (group // 2)], preferred_element_type=jnp.float32).astype(o_ref.dtype)


def _swa_bias_table():
    blk = SWA_BLK
    qi = lax.broadcasted_iota(jnp.int32, (blk, 2 * blk), 0)
    kj = lax.broadcasted_iota(jnp.int32, (blk, 2 * blk), 1)
    dist = qi + blk - kj
    window = (dist >= 0) & (dist < SWA_WINDOW)
    slopes = 2.0 ** (-8.0 * jnp.arange(1, SWA_HEADS + 1, dtype=jnp.float32) / SWA_HEADS)
    bias = -(LOG2E * slopes)[:, None, None] * dist.astype(jnp.float32)[None]
    later = jnp.where(window[None], bias, NEG_INF)
    first = jnp.where((window & (kj >= blk))[None], bias, NEG_INF)
    return jnp.stack([first, later])


def swa_attention(zb, sinks, ybuf, bsz, seq):
    blk = SWA_BLK
    nb = seq // blk
    kcol, vcol = ZB_AK // (2 * LANES), ZB_AV // (2 * LANES)

    def cur(b, n):
        return b * nb + n

    def prev(b, n):
        return b * nb + jnp.maximum(n - 1, 0)

    return pl.pallas_call(
        _swa_body,
        grid=(bsz, nb),
        in_specs=[pl.BlockSpec(memory_space=pltpu.SMEM),
                  pl.BlockSpec((None, SWA_HEADS, blk, 2 * blk),
                               lambda b, n: (jnp.minimum(n, 1), 0, 0, 0)),
                  pl.BlockSpec((blk, A_Q), lambda b, n: (cur(b, n), ZB_AQ // A_Q)),
                  pl.BlockSpec((blk, 2 * LANES), lambda b, n: (prev(b, n), kcol)),
                  pl.BlockSpec((blk, 2 * LANES), lambda b, n: (cur(b, n), kcol)),
                  pl.BlockSpec((blk, 2 * LANES), lambda b, n: (prev(b, n), vcol)),
                  pl.BlockSpec((blk, 2 * LANES), lambda b, n: (cur(b, n), vcol)),
                  pl.BlockSpec(memory_space=pl.ANY)],
        out_specs=pl.BlockSpec((None, blk, A_Q), lambda b, n: (0, cur(b, n), 0)),
        out_shape=jax.ShapeDtypeStruct(ybuf.shape, ybuf.dtype),
        input_output_aliases={7: 0},
        compiler_params=_cparams(("parallel", "arbitrary"), 32 << 20),
        name="swa_attention",
    )(sinks, _swa_bias_table(), zb, zb, zb, zb, zb, ybuf)


def _mlstm_body(bias_ref, q_ref, k_ref, v_ref, og_ref, g_ref, ybuf_ref, y_ref, cum_ref,
                c_scr, n_scr, m_scr, carry_scr):
    del ybuf_ref
    c = pl.program_id(1)
    L = MLSTM_L
    dk, dv = MLSTM_QK_DIM, MLSTM_V_DIM

    @pl.when(c == 0)
    def _():
        c_scr[...] = jnp.zeros_like(c_scr)
        n_scr[...] = jnp.zeros_like(n_scr)
        m_scr[...] = jnp.zeros_like(m_scr)
        carry_scr[...] = jnp.zeros_like(carry_scr)

    nh = MLSTM_HEADS
    a = g_ref[...] + bias_ref[...]
    lf = jnp.minimum(a, 0.0) - jnp.log1p(jnp.exp(-jnp.abs(a)))
    row = lax.broadcasted_iota(jnp.int32, (L, L), 0)
    col = lax.broadcasted_iota(jnp.int32, (L, L), 1)
    causal = row >= col
    tri = causal.astype(jnp.float32)
    b_all = jnp.dot(tri, lf, preferred_element_type=jnp.float32,
                    precision=lax.Precision.HIGHEST)
    cum = b_all + carry_scr[0:1, :]
    carry_scr[0:1, :] = cum[L - 1:L, :]
    cum_ref[0] = cum.T

    b2 = pltpu.roll(b_all, LANES - nh, axis=1) * LOG2E
    g2 = a * LOG2E - b2
    rowi = lax.broadcasted_iota(jnp.int32, (L, LANES), 0)
    cm2 = g2
    d = 1
    while d < L:
        cm2 = jnp.maximum(cm2, jnp.where(rowi >= d, pltpu.roll(cm2, d, axis=0), NEG_INF))
        d *= 2
    m2_prev = m_scr[0:1, :]
    u2 = jnp.maximum(m2_prev, cm2)
    u2_last = u2[L - 1:L, :]
    w_inter_all = jnp.exp2(m2_prev - u2)
    floor_all = jnp.exp2(-(b2 + u2))
    wk_all = jnp.exp2(g2 - u2_last)
    decay_row = jnp.exp2(m2_prev - u2_last)
    m_scr[0:1, :] = b2[L - 1:L, :] + u2_last
    g2_t = g2.T

    lane = lax.broadcasted_iota(jnp.int32, (L, LANES), 1)
    lo = lane < dk
    crow = lax.broadcasted_iota(jnp.int32, (2 * dk, 2 * dv), 0)
    ccol = lax.broadcasted_iota(jnp.int32, (2 * dk, 2 * dv), 1)
    crow_lo = crow < dk
    blockdiag = crow_lo == (ccol < dv)
    nlane_lo = lax.broadcasted_iota(jnp.int32, (1, LANES), 1) < dk

    qps, kps, scs, cps, qcs = [], [], [], [], []
    for p in range(MLSTM_HEADS // 2):
        qp = q_ref[:, p * LANES:(p + 1) * LANES]
        kp = k_ref[:, p * LANES:(p + 1) * LANES] * (dk ** -0.5)
        kpb = kp.astype(jnp.bfloat16)
        zero = jnp.zeros_like(kpb)
        kz = jnp.concatenate([jnp.where(lo, kpb, zero), jnp.where(lo, zero, kpb)], axis=0)
        scs.append(lax.dot_general(qp, kz, (((1,), (1,)), ((), ())),
                                   preferred_element_type=jnp.float32))
        cp = c_scr[p]
        qcs.append(jnp.dot(qp, cp.astype(jnp.bfloat16), preferred_element_type=jnp.float32))
        qps.append(qp); kps.append(kp); cps.append(cp)

    for p in range(MLSTM_HEADS // 2):
        qp, kp, sc, cp, qc = qps[p], kps[p], scs[p], cps[p], qcs[p]
        n_row = n_scr[p:p + 1, :]
        qn_prod = qp.astype(jnp.float32) * n_row
        for e in range(2):
            h = 2 * p + e
            decay_mat = jnp.exp2(jnp.where(causal, g2_t[h:h + 1, :] - u2[:, h:h + 1], NEG_INF))
            smat = sc[:, e * L:(e + 1) * L] * decay_mat
            w_inter = w_inter_all[:, h:h + 1]
            qn = jnp.sum(jnp.where(lo == (e == 0), qn_prod, 0.0), axis=-1, keepdims=True)
            num = w_inter * qc[:, e * dv:(e + 1) * dv] + jnp.dot(
                smat.astype(jnp.bfloat16), v_ref[:, h * dv:(h + 1) * dv],
                preferred_element_type=jnp.float32)
            den = w_inter * qn + jnp.sum(smat, axis=-1, keepdims=True)
            rden = 1.0 / jnp.maximum(jnp.abs(den), floor_all[:, h:h + 1])
            og = og_ref[:, h * dv:(h + 1) * dv]
            y_ref[:, h * dv:(h + 1) * dv] = (num * rden * (1.0 / (1.0 + jnp.exp(-og)))).astype(y_ref.dtype)
        decays = [decay_row[:, 2 * p + e:2 * p + e + 1] for e in range(2)]
        kw = kp * jnp.where(lo, wk_all[:, 2 * p:2 * p + 1], wk_all[:, 2 * p + 1:2 * p + 2])
        n_scr[p:p + 1, :] = (jnp.where(nlane_lo, decays[0], decays[1]) * n_row
                             + jnp.sum(kw, axis=0, keepdims=True))
        upd = jnp.dot(kw.T.astype(jnp.bfloat16), v_ref[:, 2 * p * dv:(2 * p + 2) * dv],
                      preferred_element_type=jnp.float32)
        c_scr[p] = jnp.where(crow_lo, decays[0], decays[1]) * cp + jnp.where(blockdiag, upd, 0.0)


def mlstm(zb, zf, bias_row, ybuf, bsz, seq):
    L = MLSTM_L
    nc = seq // L

    def rows(b, c):
        return b * nc + c

    return pl.pallas_call(
        _mlstm_body,
        grid=(bsz, nc),
        in_specs=[pl.BlockSpec((1, LANES), lambda b, c: (0, 0)),
                  pl.BlockSpec((L, B_QK), lambda b, c: (rows(b, c), ZB_MQ // B_QK)),
                  pl.BlockSpec((L, B_QK), lambda b, c: (rows(b, c), ZF_MK // B_QK)),
                  pl.BlockSpec((L, B_V), lambda b, c: (rows(b, c), ZB_MV // B_V)),
                  pl.BlockSpec((L, B_V), lambda b, c: (rows(b, c), ZF_MO // B_V)),
                  pl.BlockSpec((L, LANES), lambda b, c: (rows(b, c), ZF_G // LANES)),
                  pl.BlockSpec(memory_space=pl.ANY)],
        out_specs=[pl.BlockSpec((None, L, B_V), lambda b, c: (1, rows(b, c), 0)),
                   pl.BlockSpec((1, LANES, L), lambda b, c: (b, 0, c))],
        out_shape=[jax.ShapeDtypeStruct(ybuf.shape, ybuf.dtype),
                   jax.ShapeDtypeStruct((bsz, LANES, seq), jnp.float32)],
        input_output_aliases={6: 0},
        scratch_shapes=[pltpu.VMEM((MLSTM_HEADS // 2, 2 * MLSTM_QK_DIM, 2 * MLSTM_V_DIM), jnp.float32),
                        pltpu.VMEM((8, LANES), jnp.float32),
                        pltpu.VMEM((8, LANES), jnp.float32),
                        pltpu.VMEM((8, LANES), jnp.float32)],
        compiler_params=_cparams(("arbitrary", "arbitrary"), 32 << 20),
        name="mlstm",
    )(bias_row, zb, zf, zb, zf, zf, ybuf)


def _fox_body(q_ref, k_ref, v_ref, ck_ref, ybuf_ref, o_ref):
    del ybuf_ref
    hp = pl.program_id(1)
    qi = pl.program_id(2)
    blk = FOX_BLK
    dh = FOX_HEAD_DIM
    c2 = FOX_HEAD_DIM ** -0.5 * LOG2E

    def scores(j, e):
        start = pl.multiple_of(j * blk, blk)
        s = lax.dot_general(q_ref[:, e * dh:(e + 1) * dh], k_ref[pl.ds(start, blk), e * dh:(e + 1) * dh],
                            (((1,), (1,)), ((), ())), preferred_element_type=jnp.float32)
        return s * c2 - ck_ref[0, 2 * hp + e, pl.ds(j, 1), :] * LOG2E

    def update(s, j, e, m, l, acc):
        start = pl.multiple_of(j * blk, blk)
        m_new = jnp.maximum(m, jnp.max(s, axis=-1, keepdims=True))
        alpha = jnp.exp2(m - m_new)
        p = jnp.exp2(s - m_new)
        l = alpha * l + jnp.sum(p, axis=-1, keepdims=True)
        acc = alpha * acc + jnp.dot(p.astype(jnp.bfloat16),
                                    v_ref[pl.ds(start, blk), e * dh:(e + 1) * dh],
                                    preferred_element_type=jnp.float32)
        return m_new, l, acc

    def step(j, carry):
        s = [scores(j, e) for e in range(2)]
        return tuple(update(s[e], j, e, *carry[e]) for e in range(2))

    init1 = (jnp.full((blk, 1), NEG_INF, jnp.float32), jnp.zeros((blk, 1), jnp.float32),
             jnp.zeros((blk, dh), jnp.float32))
    carry = lax.fori_loop(0, qi, step, (init1, init1))
    row = lax.broadcasted_iota(jnp.int32, (blk, blk), 0)
    colm = lax.broadcasted_iota(jnp.int32, (blk, blk), 1)
    s = [jnp.where(colm <= row, scores(qi, e), NEG_INF) for e in range(2)]
    for e in range(2):
        _, l, acc = update(s[e], qi, e, *carry[e])
        o_ref[:, e * dh:(e + 1) * dh] = (acc / l).astype(o_ref.dtype)


def fox_attention(zb, cum_t, ybuf, bsz, seq):
    blk = FOX_BLK
    nq = seq // blk
    pw = 2 * FOX_HEAD_DIM
    ck = cum_t.reshape(bsz, LANES, nq, blk)
    return pl.pallas_call(
        _fox_body,
        grid=(bsz, FOX_HEADS // 2, nq),
        in_specs=[pl.BlockSpec((blk, pw), lambda b, h, i: (b * nq + i, ZB_CQ // pw + h)),
                  pl.BlockSpec((seq, pw), lambda b, h, i: (b, ZB_CK // pw + h)),
                  pl.BlockSpec((seq, pw), lambda b, h, i: (b, ZB_CV // pw + h)),
                  pl.BlockSpec((1, 8, nq, blk), lambda b, h, i: (b, 2, 0, 0)),
                  pl.BlockSpec(memory_space=pl.ANY)],
        out_specs=pl.BlockSpec((None, blk, pw), lambda b, h, i: (2, b * nq + i, h)),
        out_shape=jax.ShapeDtypeStruct(ybuf.shape, ybuf.dtype),
        input_output_aliases={4: 0},
        compiler_params=_cparams(("parallel", "arbitrary", "arbitrary"), 32 << 20),
        name="fox_attention",
    )(zb, zb, zb, ck, ybuf)


RG_COPY, RG_DUP, RG_GATES = 0, 1, 2
SUBLANES = 8
_SEG = dict(a_q=0, a_k=1, a_v=2, m_q=3, m_k=4, m_v=5, m_i=6, m_f=7, m_o=8,
            c_q=9, c_k=10, c_v=11, c_f=12, g_a=13, g_b=14, g_c=15)


def _regroup_table():
    tiles = []

    def copy(name):
        off, width = SEG_OFFS[_SEG[name]], SEG_WIDTHS[_SEG[name]]
        assert width % WT == 0 and off % SUBLANES == 0
        tiles.extend((off + k * WT, RG_COPY) for k in range(width // WT))

    copy("m_o"); copy("m_k")
    m_i, m_f, c_f = (SEG_OFFS[_SEG[n]] for n in ("m_i", "m_f", "c_f"))
    assert m_f == m_i + MLSTM_HEADS and m_i % SUBLANES == 0 and c_f % SUBLANES == 0
    tiles.append((m_i, RG_GATES))
    assert len(tiles) * WT == WA_ZB
    for name in ("a_q", "m_v", "c_q", "c_k", "c_v", "m_q"):
        copy(name)
    a_k, a_v = SEG_OFFS[_SEG["a_k"]], SEG_OFFS[_SEG["a_v"]]
    assert a_v == a_k + A_KV and a_k % SUBLANES == 0 and 4 * A_KV == WT
    tiles.append((a_k, RG_DUP))
    assert len(tiles) * WT == WA_G
    for name in ("g_a", "g_b", "g_c"):
        copy(name)
    assert len(tiles) * WT == WA_N
    return tiles, c_f


def _regroup_body(base_ref, mode_ref, w_ref, aux_ref, o_ref):
    del base_ref
    mode = mode_ref[pl.program_id(0)]
    bf = jnp.bfloat16
    hd = SWA_HEAD_DIM

    @pl.when(mode == RG_COPY)
    def _():
        o_ref[...] = w_ref[0].astype(bf)

    @pl.when(mode == RG_DUP)
    def _():
        for i in range(2 * SWA_KV_HEADS):
            head = w_ref[0, i * hd:(i + 1) * hd, :].astype(bf)
            o_ref[2 * i * hd:(2 * i + 1) * hd, :] = head
            o_ref[(2 * i + 1) * hd:(2 * i + 2) * hd, :] = head

    @pl.when(mode == RG_GATES)
    def _():
        nh = MLSTM_HEADS
        o_ref[0:2 * nh, :] = w_ref[0, 0:2 * nh, :].astype(bf)
        o_ref[2 * nh:3 * nh, :] = aux_ref[0].astype(bf)
        o_ref[3 * nh:WT, :] = jnp.zeros((WT - 3 * nh, o_ref.shape[1]), bf)


def regroup_w_in(w_in, l):
    tiles, c_f = _regroup_table()
    base = jnp.asarray([b for b, _ in tiles], jnp.int32)
    mode = jnp.asarray([m for _, m in tiles], jnp.int32)
    wt = jnp.swapaxes(w_in, 1, 2)
    k = wt.shape[2]
    el = pl.Element
    in_specs = [pl.BlockSpec((el(1), el(WT), el(k)),
                             lambda t, base, mode: (l, pl.multiple_of(base[t], SUBLANES), 0)),
                pl.BlockSpec((el(1), el(FOX_HEADS), el(k)), lambda t, base, mode: (l, c_f, 0))]
    vmem = 2 * (WT * k * 4 + WT * k * 2) + (12 << 20)
    return pl.pallas_call(
        _regroup_body,
        grid_spec=pltpu.PrefetchScalarGridSpec(
            num_scalar_prefetch=2, grid=(len(tiles),), in_specs=in_specs,
            out_specs=pl.BlockSpec((WT, k), lambda t, base, mode: (t, 0))),
        out_shape=jax.ShapeDtypeStruct((WA_N, k), jnp.bfloat16),
        compiler_params=_cparams(("arbitrary",), vmem),
        name="regroup_w_in",
    )(base, mode, wt, wt)


def _cast_body(w_ref, o_ref):
    o_ref[...] = w_ref[...].astype(o_ref.dtype)


def cast_bf16(w, l=None, *, br=1024, bc=2048):
    r, c = w.shape[-2:]
    br, bc = min(br, r), min(bc, c)
    assert r % br == 0 and c % bc == 0
    if l is None:
        in_spec = pl.BlockSpec((br, bc), lambda i, j: (i, j))
    else:
        in_spec = pl.BlockSpec((None, br, bc), lambda i, j: (l, i, j))
    return pl.pallas_call(
        _cast_body,
        grid=(r // br, c // bc),
        in_specs=[in_spec],
        out_specs=pl.BlockSpec((br, bc), lambda i, j: (i, j)),
        out_shape=jax.ShapeDtypeStruct((r, c), jnp.bfloat16),
        compiler_params=_cparams(("parallel", "parallel"), 2 * br * bc * 6 + (8 << 20)),
        name="cast_bf16",
    )(w)


def _cast3_body(a_ref, b_ref, c_ref, o_ref):
    br = pl.program_id(0)
    for idx, ref in enumerate((a_ref, b_ref, c_ref)):
        @pl.when(br == idx)
        def _(ref=ref):
            o_ref[...] = ref[...].astype(o_ref.dtype)


def cast_stack3(a, b, c, l, *, bc=1024):
    r, cols = a.shape[-2:]
    assert cols % bc == 0

    def spec(idx):
        return pl.BlockSpec((None, r, bc), lambda s, j: (l, 0, jnp.where(s == idx, j, 0)))

    return pl.pallas_call(
        _cast3_body,
        grid=(3, cols // bc),
        in_specs=[spec(0), spec(1), spec(2)],
        out_specs=pl.BlockSpec((None, r, bc), lambda s, j: (s, 0, j)),
        out_shape=jax.ShapeDtypeStruct((3, r, cols), jnp.bfloat16),
        compiler_params=_cparams(("arbitrary", "arbitrary"), 2 * r * bc * 14 + (8 << 20)),
        name="cast_stack3",
    )(a, b, c)


def kernel(x, w_in, b_mlstm_i, b_mlstm_f, b_fox_f, attn_sinks, w_up_swa, w_up_mlstm, w_up_fox,
           w_o, ln1_g, ln1_b, w_ff1, w_ff2, ln2_g, ln2_b):
    bsz, seq, d = x.shape
    t = bsz * seq
    assert d == D_MODEL and seq % FOX_BLK == 0 and seq % MLSTM_L == 0
    xf = x.reshape(t, d).astype(jnp.float32)
    xb = cast_bf16(xf)
    gate_pad = jnp.zeros((LANES - 3 * MLSTM_HEADS,), jnp.float32)
    for l in range(DEPTH):
        w_all = regroup_w_in(w_in, l)
        w_up = cast_stack3(w_up_swa, w_up_mlstm, w_up_fox, l)
        w_ob, w_1b, w_2b = cast_bf16(w_o, l), cast_bf16(w_ff1, l), cast_bf16(w_ff2, l)
        bias_row = jnp.concatenate([b_mlstm_i[l], b_mlstm_f[l], b_fox_f[l], gate_pad]).reshape(1, LANES)
        zf = matmul(xb, w_all, n=ZF_N, w_row=WA_ZF, bm=512, bn=ZF_N, out_dtype=jnp.float32,
                    name="in_proj_f32")
        zb = matmul(xb, w_all, n=ZB_N, w_row=WA_ZB, bm=1024, bn=1024, out_dtype=jnp.bfloat16,
                    name="in_proj_bf16")
        ybuf = jnp.zeros((3, t, A_Q), jnp.bfloat16)
        ybuf = swa_attention(zb, attn_sinks[l].astype(jnp.float32), ybuf, bsz, seq)
        ybuf, cum_t = mlstm(zb, zf, bias_row.astype(jnp.float32), ybuf, bsz, seq)
        ybuf = fox_attention(zb, cum_t, ybuf, bsz, seq)
        mix = gated_merge(xb, w_all, ybuf, w_up, bm=1024, bn=512)
        r1 = matmul_residual(mix, w_ob, xf, bm=1024, bn=512, bk=D_MODEL, name="out_proj")
        xf, xb = layer_norm(r1, ln1_g[l], ln1_b[l])
        hid = matmul(xb, w_1b, bm=1024, bn=1024, out_dtype=jnp.bfloat16, act="relu2", name="ff1")
        r2 = matmul_residual(hid, w_2b, xf, bm=1024, bn=1024, bk=2048, name="ff2")
        xf, xb = layer_norm(r2, ln2_g[l], ln2_b[l])
    return xf.reshape(bsz, seq, d).astype(x.dtype)
```

```python
import functools
from typing import NamedTuple

import jax
import jax.numpy as jnp
from jax import lax
from jax.experimental import pallas as pl
from jax.experimental.pallas import tpu as pltpu

D_MODEL = 4096
SWA_HEADS, SWA_KV_HEADS, SWA_HEAD_DIM, SWA_WINDOW = 16, 2, 64, 128
MLSTM_HEADS, MLSTM_QK_DIM, MLSTM_V_DIM = 8, 64, 128
FOX_HEADS, FOX_HEAD_DIM = 8, 128
D_FF = 4 * D_MODEL
LN_EPS = 1e-5
DEPTH = 2
DN_ALPHA = (2 * DEPTH) ** 0.25

A_Q = SWA_HEADS * SWA_HEAD_DIM
A_KV = SWA_KV_HEADS * SWA_HEAD_DIM
B_QK = MLSTM_HEADS * MLSTM_QK_DIM
B_V = MLSTM_HEADS * MLSTM_V_DIM
C_W = FOX_HEADS * FOX_HEAD_DIM
SEG_WIDTHS = (A_Q, A_KV, A_KV, B_QK, B_QK, B_V, MLSTM_HEADS, MLSTM_HEADS, B_V,
              C_W, C_W, C_W, FOX_HEADS, D_MODEL, D_MODEL, D_MODEL)
SEG_OFFS = tuple(sum(SEG_WIDTHS[:i]) for i in range(len(SEG_WIDTHS)))

LANES = 128
VMEM_LIMIT_CAP = 56 * 1024 * 1024

ZB_AQ = 0
ZB_MV = 1024
ZB_CQ = 2048
ZB_CK = 3072
ZB_CV = 4096
ZB_MQ = 5120
ZB_AK = 5632
ZB_AV = 5888
ZB_N = 6144
ZF_MO = 0
ZF_MK = 1024
ZF_G = 1536
ZF_N = 1664
WT = 512
WA_ZF = 0
WA_ZB = 2048
WA_G = WA_ZB + ZB_N
WA_N = WA_G + 3 * D_MODEL

MLSTM_L = 256
FOX_BLK = 512
SWA_BLK = 128
NEG_INF = float("-inf")
LOG2E = 1.4426950408889634


def _cparams(sem, vmem_bytes):
    return pltpu.CompilerParams(dimension_semantics=sem,
                                vmem_limit_bytes=int(min(vmem_bytes, VMEM_LIMIT_CAP)))


_NT = (((1,), (1,)), ((), ()))


class Side(NamedTuple):
    src: jax.Array
    layer: int
    rows: int


def _side_plumbing(sides, grid):
    steps = 1
    for n in grid:
        steps *= n

    def lin(*g):
        s = g[0]
        for a, n in zip(g[1:], grid[1:]):
            s = s * n + a
        return s

    in_specs, out_specs, out_shapes, vmem = [], [], [], 0
    for sd in sides:
        r, c = sd.src.shape[-2:]
        rows = sd.rows
        while r // rows > steps:
            rows *= 2
        assert r % rows == 0
        last = r // rows - 1

        def blk(*g, last=last):
            return jnp.minimum(lin(*g), last)

        in_specs.append(pl.BlockSpec((None, rows, c), lambda *g, blk=blk, l=sd.layer: (l, blk(*g), 0)))
        out_specs.append(pl.BlockSpec((rows, c), lambda *g, blk=blk: (blk(*g), 0)))
        out_shapes.append(jax.ShapeDtypeStruct((r, c), jnp.bfloat16))
        vmem += 2 * rows * c * (4 + 2)
    return in_specs, out_specs, out_shapes, vmem


def _cast_sides(side_in, side_out):
    for si, so in zip(side_in, side_out):
        so[...] = si[...].astype(so.dtype)


def _mm_body(x_ref, w_ref, *rest, act, w_t, n_side):
    side_in, o_ref, side_out = rest[:n_side], rest[n_side], rest[n_side + 1:]
    _cast_sides(side_in, side_out)
    if w_t:
        acc = lax.dot_general(x_ref[...], w_ref[...], _NT, preferred_element_type=jnp.float32)
    else:
        acc = jnp.dot(x_ref[...], w_ref[...], preferred_element_type=jnp.float32)
    if act == "relu2":
        acc = jnp.square(jnp.maximum(acc, 0.0))
    o_ref[...] = acc.astype(o_ref.dtype)


def _mm_res_body(x_ref, w_ref, r_ref, o_ref):
    acc = jnp.dot(x_ref[...], w_ref[...], preferred_element_type=jnp.float32)
    o_ref[...] = DN_ALPHA * r_ref[...] + acc


def _mm_res_k_body(x_ref, w_ref, r_ref, *rest, n_side):
    side_in, o_ref, side_out = rest[:n_side], rest[n_side], rest[n_side + 1:]
    k = pl.program_id(2)

    @pl.when(k == 0)
    def _():
        _cast_sides(side_in, side_out)
        o_ref[...] = DN_ALPHA * r_ref[...] + jnp.dot(x_ref[...], w_ref[...],
                                                     preferred_element_type=jnp.float32)

    @pl.when(k > 0)
    def _():
        _cast_sides(side_in, side_out)
        o_ref[...] += jnp.dot(x_ref[...], w_ref[...], preferred_element_type=jnp.float32)


def matmul(x, w, *, bm, bn, out_dtype, act=None, name, n=None, w_row=None, sides=()):
    m, k = x.shape
    w_t = w_row is not None
    n = w.shape[1] if n is None else n
    bm, bn = min(bm, m), min(bn, n)
    assert m % bm == 0 and n % bn == 0
    osz = jnp.dtype(out_dtype).itemsize
    grid = (m // bm, n // bn)
    s_in, s_out, s_shapes, s_vmem = _side_plumbing(sides, grid)
    vmem = 2 * (bm * k * 2 + k * bn * 2 + bm * bn * osz) + bm * bn * 4 + (4 << 20) + s_vmem
    if w_t:
        assert w_row % bn == 0
        off = w_row // bn
        w_spec = pl.BlockSpec((bn, k), lambda i, j: (off + j, 0))
    else:
        w_spec = pl.BlockSpec((k, bn), lambda i, j: (0, j))
    return pl.pallas_call(
        functools.partial(_mm_body, act=act, w_t=w_t, n_side=len(sides)),
        grid=grid,
        in_specs=[pl.BlockSpec((bm, k), lambda i, j: (i, 0)), w_spec] + s_in,
        out_specs=[pl.BlockSpec((bm, bn), lambda i, j: (i, j))] + s_out,
        out_shape=[jax.ShapeDtypeStruct((m, n), out_dtype)] + s_shapes,
        compiler_params=_cparams(("arbitrary", "arbitrary"), vmem),
        name=name,
    )(x, w, *[sd.src for sd in sides])


def matmul_residual(x, w, res, *, bm, bn, bk, name, sides=()):
    m, k = x.shape
    n = w.shape[1]
    bm, bn, bk = min(bm, m), min(bn, n), min(bk, k)
    assert m % bm == 0 and n % bn == 0 and k % bk == 0
    if bk == k:
        assert not sides
        vmem = 2 * (bm * k * 2 + k * bn * 2 + 2 * bm * bn * 4) + bm * bn * 4 + (4 << 20)
        return [pl.pallas_call(
            _mm_res_body,
            grid=(m // bm, n // bn),
            in_specs=[pl.BlockSpec((bm, k), lambda i, j: (i, 0)),
                      pl.BlockSpec((k, bn), lambda i, j: (0, j)),
                      pl.BlockSpec((bm, bn), lambda i, j: (i, j))],
            out_specs=pl.BlockSpec((bm, bn), lambda i, j: (i, j)),
            out_shape=jax.ShapeDtypeStruct((m, n), jnp.float32),
            compiler_params=_cparams(("parallel", "arbitrary"), vmem),
            name=name,
        )(x, w, res)]
    grid = (m // bm, n // bn, k // bk)
    s_in, s_out, s_shapes, s_vmem = _side_plumbing(sides, grid)
    vmem = 2 * (bm * bk * 2 + bk * bn * 2 + 2 * bm * bn * 4) + 2 * bm * bn * 4 + (4 << 20) + s_vmem
    return pl.pallas_call(
        functools.partial(_mm_res_k_body, n_side=len(sides)),
        grid=grid,
        in_specs=[pl.BlockSpec((bm, bk), lambda i, j, kk: (i, kk)),
                  pl.BlockSpec((bk, bn), lambda i, j, kk: (kk, j)),
                  pl.BlockSpec((bm, bn), lambda i, j, kk: (i, j))] + s_in,
        out_specs=[pl.BlockSpec((bm, bn), lambda i, j, kk: (i, j))] + s_out,
        out_shape=[jax.ShapeDtypeStruct((m, n), jnp.float32)] + s_shapes,
        compiler_params=_cparams(("arbitrary", "arbitrary", "arbitrary"), vmem),
        name=name,
    )(x, w, res, *[sd.src for sd in sides])


N_BRANCH = 3


def _merge_body(x_ref, wg_ref, y_ref, wu0_ref, wu1_ref, wu2_ref, *rest, n_side):
    side_in, o_ref, side_out, acc_ref = rest[:n_side], rest[n_side], rest[n_side + 1:-1], rest[-1]
    br = pl.program_id(2)

    def contrib(wu_ref):
        _cast_sides(side_in, side_out)
        g = lax.dot_general(x_ref[...], wg_ref[...], _NT, preferred_element_type=jnp.float32)
        u = jnp.dot(y_ref[0], wu_ref[...], preferred_element_type=jnp.float32)
        return u * (1.0 / (1.0 + jnp.exp(-g)))

    @pl.when(br == 0)
    def _():
        acc_ref[...] = contrib(wu0_ref)

    @pl.when(br == 1)
    def _():
        acc_ref[...] += contrib(wu1_ref)

    @pl.when(br == 2)
    def _():
        o_ref[...] = (acc_ref[...] + contrib(wu2_ref)).astype(o_ref.dtype)


def gated_merge(xb, w_all, y, wus, *, bm, bn, sides=()):
    t, d = xb.shape
    nbr, _, kin = y.shape
    assert nbr == N_BRANCH == len(wus)
    bm, bn = min(bm, t), min(bn, d)
    assert WA_G % bn == 0 and d % bn == 0
    goff, gstride = WA_G // bn, d // bn
    grid = (t // bm, d // bn, nbr)
    s_in, s_out, s_shapes, s_vmem = _side_plumbing(sides, grid)
    vmem = 2 * (bm * d * 2 + d * bn * 2 + bm * kin * 2 + nbr * kin * bn * 2 + bm * bn * 2) \
        + 4 * bm * bn * 4 + (4 << 20) + s_vmem
    wu_spec = pl.BlockSpec((kin, bn), lambda i, j, b: (0, j))
    return pl.pallas_call(
        functools.partial(_merge_body, n_side=len(sides)),
        grid=grid,
        in_specs=[pl.BlockSpec((bm, d), lambda i, j, b: (i, 0)),
                  pl.BlockSpec((bn, d), lambda i, j, b: (goff + b * gstride + j, 0)),
                  pl.BlockSpec((1, bm, kin), lambda i, j, b: (b, i, 0)),
                  wu_spec, wu_spec, wu_spec] + s_in,
        out_specs=[pl.BlockSpec((bm, bn), lambda i, j, b: (i, j))] + s_out,
        out_shape=[jax.ShapeDtypeStruct((t, d), jnp.bfloat16)] + s_shapes,
        scratch_shapes=[pltpu.VMEM((bm, bn), jnp.float32)],
        compiler_params=_cparams(("arbitrary", "arbitrary", "arbitrary"), vmem),
        name="gated_merge",
    )(xb, w_all, y, *wus, *[sd.src for sd in sides])


def _ln_body(r_ref, g_ref, b_ref, of_ref, ob_ref):
    r = r_ref[...]
    mu = jnp.mean(r, axis=-1, keepdims=True)
    xc = r - mu
    var = jnp.mean(xc * xc, axis=-1, keepdims=True)
    y = xc * lax.rsqrt(var + LN_EPS) * g_ref[...] + b_ref[...]
    of_ref[...] = y
    ob_ref[...] = y.astype(jnp.bfloat16)


def layer_norm(r, g, b, *, bm=256):
    t, d = r.shape
    bm = min(bm, t)
    vmem = 2 * (bm * d * 4 * 2 + bm * d * 2) + 4 * bm * d * 4 + (4 << 20)
    return pl.pallas_call(
        _ln_body,
        grid=(t // bm,),
        in_specs=[pl.BlockSpec((bm, d), lambda i: (i, 0)),
                  pl.BlockSpec((1, d), lambda i: (0, 0)),
                  pl.BlockSpec((1, d), lambda i: (0, 0))],
        out_specs=[pl.BlockSpec((bm, d), lambda i: (i, 0)),
                   pl.BlockSpec((bm, d), lambda i: (i, 0))],
        out_shape=[jax.ShapeDtypeStruct((t, d), jnp.float32),
                   jax.ShapeDtypeStruct((t, d), jnp.bfloat16)],
        compiler_params=_cparams(("parallel",), vmem),
        name="layer_norm",
    )(r, g.reshape(1, d), b.reshape(1, d))


def _swa_body(sink_ref, bias_ref, q_ref, kp_ref, kc_ref, vp_ref, vc_ref, ybuf_ref, o_ref):
    del ybuf_ref
    blk = SWA_BLK
    c2 = SWA_HEAD_DIM ** -0.5 * LOG2E
    lo = lax.broadcasted_iota(jnp.int32, (2 * blk, LANES), 1) < SWA_HEAD_DIM
    group = SWA_HEADS // SWA_KV_HEADS
    pairs = SWA_HEADS // 2
    kz, vz = [], []
    for g in range(SWA_KV_HEADS):
        kd = jnp.concatenate([kp_ref[:, g * LANES:(g + 1) * LANES],
                              kc_ref[:, g * LANES:(g + 1) * LANES]], axis=0)
        vd = jnp.concatenate([vp_ref[:, g * LANES:(g + 1) * LANES],
                              vc_ref[:, g * LANES:(g + 1) * LANES]], axis=0)
        zero = jnp.zeros_like(kd)
        kz.append(jnp.concatenate([jnp.where(lo, kd, zero), jnp.where(lo, zero, kd)], axis=0))
        vz.append(jnp.concatenate([jnp.where(lo, vd, zero), jnp.where(lo, zero, vd)], axis=0))
    s2 = [lax.dot_general(q_ref[:, p * LANES:(p + 1) * LANES], kz[p // (group // 2)],
                          (((1,), (1,)), ((), ())), preferred_element_type=jnp.float32)
          for p in range(pairs)]
    for p in range(pairs):
        probs = []
        for e in range(2):
            h = 2 * p + e
            sink = sink_ref[h] * LOG2E
            s = s2[p][:, e * 2 * blk:(e + 1) * 2 * blk] * c2 + bias_ref[h]
            m = jnp.maximum(jnp.max(s, axis=-1, keepdims=True), sink)
            pe = jnp.exp2(s - m)
            den = jnp.sum(pe, axis=-1, keepdims=True) + jnp.exp2(sink - m)
            probs.append((pe * (1.0 / den)).astype(jnp.bfloat16))
        p2 = jnp.concatenate(probs, axis=1)
        o_ref[:, p * LANES:(p + 1) * LANES] = jnp.dot(
            p2, vz[p // (group // 2)], preferred_element_type=jnp.float32).astype(o_ref.dtype)


def _swa_bias_table():
    blk = SWA_BLK
    qi = lax.broadcasted_iota(jnp.int32, (blk, 2 * blk), 0)
    kj = lax.broadcasted_iota(jnp.int32, (blk, 2 * blk), 1)
    dist = qi + blk - kj
    window = (dist >= 0) & (dist < SWA_WINDOW)
    slopes = 2.0 ** (-8.0 * jnp.arange(1, SWA_HEADS + 1, dtype=jnp.float32) / SWA_HEADS)
    bias = -(LOG2E * slopes)[:, None, None] * dist.astype(jnp.float32)[None]
    later = jnp.where(window[None], bias, NEG_INF)
    first = jnp.where((window & (kj >= blk))[None], bias, NEG_INF)
    return jnp.stack([first, later])


def swa_attention(zb, sinks, ybuf, bsz, seq):
    blk = SWA_BLK
    nb = seq // blk
    kcol, vcol = ZB_AK // (2 * LANES), ZB_AV // (2 * LANES)

    def cur(b, n):
        return b * nb + n

    def prev(b, n):
        return b * nb + jnp.maximum(n - 1, 0)

    return pl.pallas_call(
        _swa_body,
        grid=(bsz, nb),
        in_specs=[pl.BlockSpec(memory_space=pltpu.SMEM),
                  pl.BlockSpec((None, SWA_HEADS, blk, 2 * blk),
                               lambda b, n: (jnp.minimum(n, 1), 0, 0, 0)),
                  pl.BlockSpec((blk, A_Q), lambda b, n: (cur(b, n), ZB_AQ // A_Q)),
                  pl.BlockSpec((blk, 2 * LANES), lambda b, n: (prev(b, n), kcol)),
                  pl.BlockSpec((blk, 2 * LANES), lambda b, n: (cur(b, n), kcol)),
                  pl.BlockSpec((blk, 2 * LANES), lambda b, n: (prev(b, n), vcol)),
                  pl.BlockSpec((blk, 2 * LANES), lambda b, n: (cur(b, n), vcol)),
                  pl.BlockSpec(memory_space=pl.ANY)],
        out_specs=pl.BlockSpec((None, blk, A_Q), lambda b, n: (0, cur(b, n), 0)),
        out_shape=jax.ShapeDtypeStruct(ybuf.shape, ybuf.dtype),
        input_output_aliases={7: 0},
        compiler_params=_cparams(("parallel", "arbitrary"), 32 << 20),
        name="swa_attention",
    )(sinks, _swa_bias_table(), zb, zb, zb, zb, zb, ybuf)


def _mlstm_body(bias_ref, q_ref, k_ref, v_ref, og_ref, g_ref, ybuf_ref, y_ref, cum_ref,
                c_scr, n_scr, m_scr, carry_scr):
    del ybuf_ref
    c = pl.program_id(1)
    L = MLSTM_L
    dk, dv = MLSTM_QK_DIM, MLSTM_V_DIM

    @pl.when(c == 0)
    def _():
        c_scr[...] = jnp.zeros_like(c_scr)
        n_scr[...] = jnp.zeros_like(n_scr)
        m_scr[...] = jnp.zeros_like(m_scr)
        carry_scr[...] = jnp.zeros_like(carry_scr)

    nh = MLSTM_HEADS
    a = g_ref[...] + bias_ref[...]
    lf = jnp.minimum(a, 0.0) - jnp.log1p(jnp.exp(-jnp.abs(a)))
    row = lax.broadcasted_iota(jnp.int32, (L, L), 0)
    col = lax.broadcasted_iota(jnp.int32, (L, L), 1)
    causal = row >= col
    tri = causal.astype(jnp.float32)
    b_all = jnp.dot(tri, lf, preferred_element_type=jnp.float32,
                    precision=lax.Precision.HIGHEST)
    cum = b_all + carry_scr[0:1, :]
    carry_scr[0:1, :] = cum[L - 1:L, :]
    cum_ref[0] = cum.T

    b2 = pltpu.roll(b_all, LANES - nh, axis=1) * LOG2E
    g2 = a * LOG2E - b2
    rowi = lax.broadcasted_iota(jnp.int32, (L, LANES), 0)
    cm2 = g2
    d = 1
    while d < L:
        cm2 = jnp.maximum(cm2, jnp.where(rowi >= d, pltpu.roll(cm2, d, axis=0), NEG_INF))
        d *= 2
    m2_prev = m_scr[0:1, :]
    u2 = jnp.maximum(m2_prev, cm2)
    u2_last = u2[L - 1:L, :]
    w_inter_all = jnp.exp2(m2_prev - u2)
    floor_all = jnp.exp2(-(b2 + u2))
    wk_all = jnp.exp2(g2 - u2_last)
    decay_row = jnp.exp2(m2_prev - u2_last)
    m_scr[0:1, :] = b2[L - 1:L, :] + u2_last
    g2_t = g2.T

    lane = lax.broadcasted_iota(jnp.int32, (L, LANES), 1)
    lo = lane < dk
    crow = lax.broadcasted_iota(jnp.int32, (2 * dk, 2 * dv), 0)
    ccol = lax.broadcasted_iota(jnp.int32, (2 * dk, 2 * dv), 1)
    crow_lo = crow < dk
    blockdiag = crow_lo == (ccol < dv)
    nlane_lo = lax.broadcasted_iota(jnp.int32, (1, LANES), 1) < dk

    qps, kps, scs, cps, qcs = [], [], [], [], []
    for p in range(MLSTM_HEADS // 2):
        qp = q_ref[:, p * LANES:(p + 1) * LANES]
        kp = k_ref[:, p * LANES:(p + 1) * LANES] * (dk ** -0.5)
        kpb = kp.astype(jnp.bfloat16)
        zero = jnp.zeros_like(kpb)
        kz = jnp.concatenate([jnp.where(lo, kpb, zero), jnp.where(lo, zero, kpb)], axis=0)
        scs.append(lax.dot_general(qp, kz, (((1,), (1,)), ((), ())),
                                   preferred_element_type=jnp.float32))
        cp = c_scr[p]
        qcs.append(jnp.dot(qp, cp.astype(jnp.bfloat16), preferred_element_type=jnp.float32))
        qps.append(qp); kps.append(kp); cps.append(cp)

    for p in range(MLSTM_HEADS // 2):
        qp, kp, sc, cp, qc = qps[p], kps[p], scs[p], cps[p], qcs[p]
        n_row = n_scr[p:p + 1, :]
        qn_prod = qp.astype(jnp.float32) * n_row
        for e in range(2):
            h = 2 * p + e
            decay_mat = jnp.exp2(jnp.where(causal, g2_t[h:h + 1, :] - u2[:, h:h + 1], NEG_INF))
            smat = sc[:, e * L:(e + 1) * L] * decay_mat
            w_inter = w_inter_all[:, h:h + 1]
            qn = jnp.sum(jnp.where(lo == (e == 0), qn_prod, 0.0), axis=-1, keepdims=True)
            num = w_inter * qc[:, e * dv:(e + 1) * dv] + jnp.dot(
                smat.astype(jnp.bfloat16), v_ref[:, h * dv:(h + 1) * dv],
                preferred_element_type=jnp.float32)
            den = w_inter * qn + jnp.sum(smat, axis=-1, keepdims=True)
            rden = 1.0 / jnp.maximum(jnp.abs(den), floor_all[:, h:h + 1])
            og = og_ref[:, h * dv:(h + 1) * dv]
            y_ref[:, h * dv:(h + 1) * dv] = (num * rden * (1.0 / (1.0 + jnp.exp(-og)))).astype(y_ref.dtype)
        decays = [decay_row[:, 2 * p + e:2 * p + e + 1] for e in range(2)]
        kw = kp * jnp.where(lo, wk_all[:, 2 * p:2 * p + 1], wk_all[:, 2 * p + 1:2 * p + 2])
        n_scr[p:p + 1, :] = (jnp.where(nlane_lo, decays[0], decays[1]) * n_row
                             + jnp.sum(kw, axis=0, keepdims=True))
        upd = jnp.dot(kw.T.astype(jnp.bfloat16), v_ref[:, 2 * p * dv:(2 * p + 2) * dv],
                      preferred_element_type=jnp.float32)
        c_scr[p] = jnp.where(crow_lo, decays[0], decays[1]) * cp + jnp.where(blockdiag, upd, 0.0)


def mlstm(zb, zf, bias_row, ybuf, bsz, seq):
    L = MLSTM_L
    nc = seq // L

    def rows(b, c):
        return b * nc + c

    return pl.pallas_call(
        _mlstm_body,
        grid=(bsz, nc),
        in_specs=[pl.BlockSpec((1, LANES), lambda b, c: (0, 0)),
                  pl.BlockSpec((L, B_QK), lambda b, c: (rows(b, c), ZB_MQ // B_QK)),
                  pl.BlockSpec((L, B_QK), lambda b, c: (rows(b, c), ZF_MK // B_QK)),
                  pl.BlockSpec((L, B_V), lambda b, c: (rows(b, c), ZB_MV // B_V)),
                  pl.BlockSpec((L, B_V), lambda b, c: (rows(b, c), ZF_MO // B_V)),
                  pl.BlockSpec((L, LANES), lambda b, c: (rows(b, c), ZF_G // LANES)),
                  pl.BlockSpec(memory_space=pl.ANY)],
        out_specs=[pl.BlockSpec((None, L, B_V), lambda b, c: (1, rows(b, c), 0)),
                   pl.BlockSpec((1, LANES, L), lambda b, c: (b, 0, c))],
        out_shape=[jax.ShapeDtypeStruct(ybuf.shape, ybuf.dtype),
                   jax.ShapeDtypeStruct((bsz, LANES, seq), jnp.float32)],
        input_output_aliases={6: 0},
        scratch_shapes=[pltpu.VMEM((MLSTM_HEADS // 2, 2 * MLSTM_QK_DIM, 2 * MLSTM_V_DIM), jnp.float32),
                        pltpu.VMEM((8, LANES), jnp.float32),
                        pltpu.VMEM((8, LANES), jnp.float32),
                        pltpu.VMEM((8, LANES), jnp.float32)],
        compiler_params=_cparams(("arbitrary", "arbitrary"), 32 << 20),
        name="mlstm",
    )(bias_row, zb, zf, zb, zf, zf, ybuf)


def _fox_body(q_ref, k_ref, v_ref, ck_ref, ybuf_ref, o_ref):
    del ybuf_ref
    hp = pl.program_id(1)
    qi = pl.program_id(2)
    blk = FOX_BLK
    dh = FOX_HEAD_DIM
    c2 = FOX_HEAD_DIM ** -0.5 * LOG2E

    def scores(j, e):
        start = pl.multiple_of(j * blk, blk)
        s = lax.dot_general(q_ref[:, e * dh:(e + 1) * dh], k_ref[pl.ds(start, blk), e * dh:(e + 1) * dh],
                            (((1,), (1,)), ((), ())), preferred_element_type=jnp.float32)
        return s * c2 - ck_ref[0, 2 * hp + e, pl.ds(j, 1), :] * LOG2E

    def update(s, j, e, m, l, acc):
        start = pl.multiple_of(j * blk, blk)
        m_new = jnp.maximum(m, jnp.max(s, axis=-1, keepdims=True))
        alpha = jnp.exp2(m - m_new)
        p = jnp.exp2(s - m_new)
        l = alpha * l + jnp.sum(p, axis=-1, keepdims=True)
        acc = alpha * acc + jnp.dot(p.astype(jnp.bfloat16),
                                    v_ref[pl.ds(start, blk), e * dh:(e + 1) * dh],
                                    preferred_element_type=jnp.float32)
        return m_new, l, acc

    def step(j, carry):
        s = [scores(j, e) for e in range(2)]
        return tuple(update(s[e], j, e, *carry[e]) for e in range(2))

    init1 = (jnp.full((blk, 1), NEG_INF, jnp.float32), jnp.zeros((blk, 1), jnp.float32),
             jnp.zeros((blk, dh), jnp.float32))
    carry = lax.fori_loop(0, qi, step, (init1, init1))
    row = lax.broadcasted_iota(jnp.int32, (blk, blk), 0)
    colm = lax.broadcasted_iota(jnp.int32, (blk, blk), 1)
    s = [jnp.where(colm <= row, scores(qi, e), NEG_INF) for e in range(2)]
    for e in range(2):
        _, l, acc = update(s[e], qi, e, *carry[e])
        o_ref[:, e * dh:(e + 1) * dh] = (acc / l).astype(o_ref.dtype)


def fox_attention(zb, cum_t, ybuf, bsz, seq):
    blk = FOX_BLK
    nq = seq // blk
    pw = 2 * FOX_HEAD_DIM
    ck = cum_t.reshape(bsz, LANES, nq, blk)
    return pl.pallas_call(
        _fox_body,
        grid=(bsz, FOX_HEADS // 2, nq),
        in_specs=[pl.BlockSpec((blk, pw), lambda b, h, i: (b * nq + i, ZB_CQ // pw + h)),
                  pl.BlockSpec((seq, pw), lambda b, h, i: (b, ZB_CK // pw + h)),
                  pl.BlockSpec((seq, pw), lambda b, h, i: (b, ZB_CV // pw + h)),
                  pl.BlockSpec((1, 8, nq, blk), lambda b, h, i: (b, 2, 0, 0)),
                  pl.BlockSpec(memory_space=pl.ANY)],
        out_specs=pl.BlockSpec((None, blk, pw), lambda b, h, i: (2, b * nq + i, h)),
        out_shape=jax.ShapeDtypeStruct(ybuf.shape, ybuf.dtype),
        input_output_aliases={4: 0},
        compiler_params=_cparams(("parallel", "arbitrary", "arbitrary"), 32 << 20),
        name="fox_attention",
    )(zb, zb, zb, ck, ybuf)


RG_COPY, RG_DUP, RG_GATES = 0, 1, 2
SUBLANES = 8
_SEG = dict(a_q=0, a_k=1, a_v=2, m_q=3, m_k=4, m_v=5, m_i=6, m_f=7, m_o=8,
            c_q=9, c_k=10, c_v=11, c_f=12, g_a=13, g_b=14, g_c=15)


def _regroup_table():
    tiles = []

    def copy(name):
        off, width = SEG_OFFS[_SEG[name]], SEG_WIDTHS[_SEG[name]]
        assert width % WT == 0 and off % SUBLANES == 0
        tiles.extend((off + k * WT, RG_COPY) for k in range(width // WT))

    copy("m_o"); copy("m_k")
    m_i, m_f, c_f = (SEG_OFFS[_SEG[n]] for n in ("m_i", "m_f", "c_f"))
    assert m_f == m_i + MLSTM_HEADS and m_i % SUBLANES == 0 and c_f % SUBLANES == 0
    tiles.append((m_i, RG_GATES))
    assert len(tiles) * WT == WA_ZB
    for name in ("a_q", "m_v", "c_q", "c_k", "c_v", "m_q"):
        copy(name)
    a_k, a_v = SEG_OFFS[_SEG["a_k"]], SEG_OFFS[_SEG["a_v"]]
    assert a_v == a_k + A_KV and a_k % SUBLANES == 0 and 4 * A_KV == WT
    tiles.append((a_k, RG_DUP))
    assert len(tiles) * WT == WA_G
    for name in ("g_a", "g_b", "g_c"):
        copy(name)
    assert len(tiles) * WT == WA_N
    return tiles, c_f


def _regroup_body(base_ref, mode_ref, w_ref, aux_ref, o_ref):
    del base_ref
    mode = mode_ref[pl.program_id(0)]
    bf = jnp.bfloat16
    hd = SWA_HEAD_DIM

    @pl.when(mode == RG_COPY)
    def _():
        o_ref[...] = w_ref[0].astype(bf)

    @pl.when(mode == RG_DUP)
    def _():
        for i in range(2 * SWA_KV_HEADS):
            head = w_ref[0, i * hd:(i + 1) * hd, :].astype(bf)
            o_ref[2 * i * hd:(2 * i + 1) * hd, :] = head
            o_ref[(2 * i + 1) * hd:(2 * i + 2) * hd, :] = head

    @pl.when(mode == RG_GATES)
    def _():
        nh = MLSTM_HEADS
        o_ref[0:2 * nh, :] = w_ref[0, 0:2 * nh, :].astype(bf)
        o_ref[2 * nh:3 * nh, :] = aux_ref[0].astype(bf)
        o_ref[3 * nh:WT, :] = jnp.zeros((WT - 3 * nh, o_ref.shape[1]), bf)


def regroup_w_in(w_in, l):
    tiles, c_f = _regroup_table()
    base = jnp.asarray([b for b, _ in tiles], jnp.int32)
    mode = jnp.asarray([m for _, m in tiles], jnp.int32)
    wt = jnp.swapaxes(w_in, 1, 2)
    k = wt.shape[2]
    el = pl.Element
    in_specs = [pl.BlockSpec((el(1), el(WT), el(k)),
                             lambda t, base, mode: (l, pl.multiple_of(base[t], SUBLANES), 0)),
                pl.BlockSpec((el(1), el(FOX_HEADS), el(k)), lambda t, base, mode: (l, c_f, 0))]
    vmem = 2 * (WT * k * 4 + WT * k * 2) + (12 << 20)
    return pl.pallas_call(
        _regroup_body,
        grid_spec=pltpu.PrefetchScalarGridSpec(
            num_scalar_prefetch=2, grid=(len(tiles),), in_specs=in_specs,
            out_specs=pl.BlockSpec((WT, k), lambda t, base, mode: (t, 0))),
        out_shape=jax.ShapeDtypeStruct((WA_N, k), jnp.bfloat16),
        compiler_params=_cparams(("arbitrary",), vmem),
        name="regroup_w_in",
    )(base, mode, wt, wt)


def _cast_body(w_ref, o_ref):
    o_ref[...] = w_ref[...].astype(o_ref.dtype)


def cast_bf16(w, l=None, *, br=1024, bc=2048):
    r, c = w.shape[-2:]
    br, bc = min(br, r), min(bc, c)
    assert r % br == 0 and c % bc == 0
    if l is None:
        in_spec = pl.BlockSpec((br, bc), lambda i, j: (i, j))
    else:
        in_spec = pl.BlockSpec((None, br, bc), lambda i, j: (l, i, j))
    return pl.pallas_call(
        _cast_body,
        grid=(r // br, c // bc),
        in_specs=[in_spec],
        out_specs=pl.BlockSpec((br, bc), lambda i, j: (i, j)),
        out_shape=jax.ShapeDtypeStruct((r, c), jnp.bfloat16),
        compiler_params=_cparams(("parallel", "parallel"), 2 * br * bc * 6 + (8 << 20)),
        name="cast_bf16",
    )(w)


def kernel(x, w_in, b_mlstm_i, b_mlstm_f, b_fox_f, attn_sinks, w_up_swa, w_up_mlstm, w_up_fox,
           w_o, ln1_g, ln1_b, w_ff1, w_ff2, ln2_g, ln2_b):
    bsz, seq, d = x.shape
    t = bsz * seq
    assert d == D_MODEL and seq % FOX_BLK == 0 and seq % MLSTM_L == 0
    xf = x.reshape(t, d).astype(jnp.float32)
    xb = cast_bf16(xf)
    gate_pad = jnp.zeros((LANES - 3 * MLSTM_HEADS,), jnp.float32)
    ups = (w_up_swa, w_up_mlstm, w_up_fox)
    w_ups = [cast_bf16(w, 0) for w in ups]
    w_ob = cast_bf16(w_o, 0)
    for l in range(DEPTH):
        w_all = regroup_w_in(w_in, l)
        bias_row = jnp.concatenate([b_mlstm_i[l], b_mlstm_f[l], b_fox_f[l], gate_pad]).reshape(1, LANES)
        (zf,) = matmul(xb, w_all, n=ZF_N, w_row=WA_ZF, bm=512, bn=ZF_N, out_dtype=jnp.float32,
                       name="in_proj_f32")
        (zb,) = matmul(xb, w_all, n=ZB_N, w_row=WA_ZB, bm=1024, bn=1024, out_dtype=jnp.bfloat16,
                       name="in_proj_bf16")
        ybuf = jnp.zeros((3, t, A_Q), jnp.bfloat16)
        ybuf = swa_attention(zb, attn_sinks[l].astype(jnp.float32), ybuf, bsz, seq)
        ybuf, cum_t = mlstm(zb, zf, bias_row.astype(jnp.float32), ybuf, bsz, seq)
        ybuf = fox_attention(zb, cum_t, ybuf, bsz, seq)
        mix, w_1b = gated_merge(xb, w_all, ybuf, w_ups, bm=1024, bn=512, sides=[Side(w_ff1, l, 32)])
        (r1,) = matmul_residual(mix, w_ob, xf, bm=1024, bn=512, bk=D_MODEL, name="out_proj")
        xf, xb = layer_norm(r1, ln1_g[l], ln1_b[l])
        hid, w_2b = matmul(xb, w_1b, bm=1024, bn=1024, out_dtype=jnp.bfloat16, act="relu2",
                           name="ff1", sides=[Side(w_ff2, l, 128)])
        nxt = [Side(w, l + 1, 16) for w in (w_o,) + ups] if l + 1 < DEPTH else []
        r2, *cast_next = matmul_residual(hid, w_2b, xf, bm=1024, bn=1024, bk=2048, name="ff2", sides=nxt)
        if cast_next:
            w_ob, *w_ups = cast_next
        xf, xb = layer_norm(r2, ln2_g[l], ln2_b[l])
    return xf.reshape(bsz, seq, d).astype(x.dtype)
```

```python
import functools
from typing import NamedTuple

import jax
import jax.numpy as jnp
from jax import lax
from jax.experimental import pallas as pl
from jax.experimental.pallas import tpu as pltpu

D_MODEL = 4096
SWA_HEADS, SWA_KV_HEADS, SWA_HEAD_DIM, SWA_WINDOW = 16, 2, 64, 128
MLSTM_HEADS, MLSTM_QK_DIM, MLSTM_V_DIM = 8, 64, 128
FOX_HEADS, FOX_HEAD_DIM = 8, 128
D_FF = 4 * D_MODEL
LN_EPS = 1e-5
DEPTH = 2
DN_ALPHA = (2 * DEPTH) ** 0.25

A_Q = SWA_HEADS * SWA_HEAD_DIM
A_KV = SWA_KV_HEADS * SWA_HEAD_DIM
B_QK = MLSTM_HEADS * MLSTM_QK_DIM
B_V = MLSTM_HEADS * MLSTM_V_DIM
C_W = FOX_HEADS * FOX_HEAD_DIM
SEG_WIDTHS = (A_Q, A_KV, A_KV, B_QK, B_QK, B_V, MLSTM_HEADS, MLSTM_HEADS, B_V,
              C_W, C_W, C_W, FOX_HEADS, D_MODEL, D_MODEL, D_MODEL)
SEG_OFFS = tuple(sum(SEG_WIDTHS[:i]) for i in range(len(SEG_WIDTHS)))

LANES = 128
SUBLANES = 8
VMEM_LIMIT_CAP = 56 * 1024 * 1024

ZB_AQ = 0
ZB_MV = 1024
ZB_CQ = 2048
ZB_CK = 3072
ZB_CV = 4096
ZB_MQ = 5120
ZB_AK = 5632
ZB_AV = 5888
ZB_N = 6144
ZF_MO = 0
ZF_MK = 1024
ZF_G = 1536
ZF_N = 1664
WT = 512
WA_ZF = 0
WA_ZB = 2048
WA_N = WA_ZB + ZB_N

MLSTM_L = 256
FOX_BLK = 512
SWA_BLK = 128
NEG_INF = float("-inf")
LOG2E = 1.4426950408889634


def _cparams(sem, vmem_bytes):
    return pltpu.CompilerParams(dimension_semantics=sem,
                                vmem_limit_bytes=int(min(vmem_bytes, VMEM_LIMIT_CAP)))


_NT = (((1,), (1,)), ((), ()))


class Side(NamedTuple):
    src: jax.Array
    layer: int
    rows: int
    row_off: int = 0
    n_rows: int = 0


def _side_plumbing(sides, grid):
    steps = 1
    for n in grid:
        steps *= n

    def lin(*g):
        s = g[0]
        for a, n in zip(g[1:], grid[1:]):
            s = s * n + a
        return s

    in_specs, out_specs, out_shapes, vmem = [], [], [], 0
    for sd in sides:
        c = sd.src.shape[-1]
        r = sd.n_rows or sd.src.shape[-2]
        rows = sd.rows
        while r // rows > steps:
            rows *= 2
        assert r % rows == 0 and sd.row_off % SUBLANES == 0
        last = r // rows - 1

        def blk(*g, last=last):
            return jnp.minimum(lin(*g), last)

        if sd.row_off or sd.n_rows:
            el = pl.Element
            in_specs.append(pl.BlockSpec(
                (el(1), el(rows), el(c)),
                lambda *g, blk=blk, sd=sd, rows=rows: (
                    sd.layer, pl.multiple_of(sd.row_off + blk(*g) * rows, SUBLANES), 0)))
        else:
            in_specs.append(pl.BlockSpec((None, rows, c),
                                         lambda *g, blk=blk, l=sd.layer: (l, blk(*g), 0)))
        out_specs.append(pl.BlockSpec((rows, c), lambda *g, blk=blk: (blk(*g), 0)))
        out_shapes.append(jax.ShapeDtypeStruct((r, c), jnp.bfloat16))
        vmem += 2 * rows * c * (4 + 2)
    return in_specs, out_specs, out_shapes, vmem


def _cast_sides(side_in, side_out):
    for si, so in zip(side_in, side_out):
        so[...] = si[...].reshape(so.shape).astype(so.dtype)


def _mm_body(x_ref, w_ref, *rest, act, w_t, n_side):
    side_in, o_ref, side_out = rest[:n_side], rest[n_side], rest[n_side + 1:]
    _cast_sides(side_in, side_out)
    if w_t:
        acc = lax.dot_general(x_ref[...], w_ref[...], _NT, preferred_element_type=jnp.float32)
    else:
        acc = jnp.dot(x_ref[...], w_ref[...], preferred_element_type=jnp.float32)
    if act == "relu2":
        acc = jnp.square(jnp.maximum(acc, 0.0))
    o_ref[...] = acc.astype(o_ref.dtype)


def _mm_res_body(x_ref, w_ref, r_ref, o_ref):
    acc = jnp.dot(x_ref[...], w_ref[...], preferred_element_type=jnp.float32)
    o_ref[...] = DN_ALPHA * r_ref[...] + acc


def _mm_res_k_body(x_ref, w_ref, r_ref, *rest, n_side):
    side_in, o_ref, side_out = rest[:n_side], rest[n_side], rest[n_side + 1:]
    k = pl.program_id(2)

    @pl.when(k == 0)
    def _():
        _cast_sides(side_in, side_out)
        o_ref[...] = DN_ALPHA * r_ref[...] + jnp.dot(x_ref[...], w_ref[...],
                                                     preferred_element_type=jnp.float32)

    @pl.when(k > 0)
    def _():
        _cast_sides(side_in, side_out)
        o_ref[...] += jnp.dot(x_ref[...], w_ref[...], preferred_element_type=jnp.float32)


def matmul(x, w, *, bm, bn, out_dtype, act=None, name, n=None, w_row=None, sides=()):
    m, k = x.shape
    w_t = w_row is not None
    n = w.shape[1] if n is None else n
    bm, bn = min(bm, m), min(bn, n)
    assert m % bm == 0 and n % bn == 0
    osz = jnp.dtype(out_dtype).itemsize
    grid = (m // bm, n // bn)
    s_in, s_out, s_shapes, s_vmem = _side_plumbing(sides, grid)
    vmem = 2 * (bm * k * 2 + k * bn * 2 + bm * bn * osz) + bm * bn * 4 + (4 << 20) + s_vmem
    if w_t:
        assert w_row % bn == 0
        off = w_row // bn
        w_spec = pl.BlockSpec((bn, k), lambda i, j: (off + j, 0))
    else:
        w_spec = pl.BlockSpec((k, bn), lambda i, j: (0, j))
    return pl.pallas_call(
        functools.partial(_mm_body, act=act, w_t=w_t, n_side=len(sides)),
        grid=grid,
        in_specs=[pl.BlockSpec((bm, k), lambda i, j: (i, 0)), w_spec] + s_in,
        out_specs=[pl.BlockSpec((bm, bn), lambda i, j: (i, j))] + s_out,
        out_shape=[jax.ShapeDtypeStruct((m, n), out_dtype)] + s_shapes,
        compiler_params=_cparams(("arbitrary", "arbitrary"), vmem),
        name=name,
    )(x, w, *[sd.src for sd in sides])


def matmul_residual(x, w, res, *, bm, bn, bk, name, sides=()):
    m, k = x.shape
    n = w.shape[1]
    bm, bn, bk = min(bm, m), min(bn, n), min(bk, k)
    assert m % bm == 0 and n % bn == 0 and k % bk == 0
    if bk == k:
        assert not sides
        vmem = 2 * (bm * k * 2 + k * bn * 2 + 2 * bm * bn * 4) + bm * bn * 4 + (4 << 20)
        return [pl.pallas_call(
            _mm_res_body,
            grid=(m // bm, n // bn),
            in_specs=[pl.BlockSpec((bm, k), lambda i, j: (i, 0)),
                      pl.BlockSpec((k, bn), lambda i, j: (0, j)),
                      pl.BlockSpec((bm, bn), lambda i, j: (i, j))],
            out_specs=pl.BlockSpec((bm, bn), lambda i, j: (i, j)),
            out_shape=jax.ShapeDtypeStruct((m, n), jnp.float32),
            compiler_params=_cparams(("parallel", "arbitrary"), vmem),
            name=name,
        )(x, w, res)]
    grid = (m // bm, n // bn, k // bk)
    s_in, s_out, s_shapes, s_vmem = _side_plumbing(sides, grid)
    vmem = 2 * (bm * bk * 2 + bk * bn * 2 + 2 * bm * bn * 4) + 2 * bm * bn * 4 + (4 << 20) + s_vmem
    return pl.pallas_call(
        functools.partial(_mm_res_k_body, n_side=len(sides)),
        grid=grid,
        in_specs=[pl.BlockSpec((bm, bk), lambda i, j, kk: (i, kk)),
                  pl.BlockSpec((bk, bn), lambda i, j, kk: (kk, j)),
                  pl.BlockSpec((bm, bn), lambda i, j, kk: (i, j))] + s_in,
        out_specs=[pl.BlockSpec((bm, bn), lambda i, j, kk: (i, j))] + s_out,
        out_shape=[jax.ShapeDtypeStruct((m, n), jnp.float32)] + s_shapes,
        compiler_params=_cparams(("arbitrary", "arbitrary", "arbitrary"), vmem),
        name=name,
    )(x, w, res, *[sd.src for sd in sides])


N_BRANCH = 3


def _merge_body(x_ref, wg_ref, y_ref, wu0_ref, wu1_ref, wu2_ref, *rest, n_side):
    side_in, o_ref, side_out, acc_ref = rest[:n_side], rest[n_side], rest[n_side + 1:-1], rest[-1]
    br = pl.program_id(2)

    def contrib(wu_ref):
        _cast_sides(side_in, side_out)
        g = lax.dot_general(x_ref[...], wg_ref[...], _NT, preferred_element_type=jnp.float32)
        u = jnp.dot(y_ref[0], wu_ref[...], preferred_element_type=jnp.float32)
        return u * (1.0 / (1.0 + jnp.exp(-g)))

    @pl.when(br == 0)
    def _():
        acc_ref[...] = contrib(wu0_ref)

    @pl.when(br == 1)
    def _():
        acc_ref[...] += contrib(wu1_ref)

    @pl.when(br == 2)
    def _():
        o_ref[...] = (acc_ref[...] + contrib(wu2_ref)).astype(o_ref.dtype)


def gated_merge(xb, wg, y, wus, *, bm, bn, sides=()):
    t, d = xb.shape
    nbr, _, kin = y.shape
    assert nbr == N_BRANCH == len(wus) and wg.shape == (nbr * d, d)
    bm, bn = min(bm, t), min(bn, d)
    assert d % bn == 0
    gstride = d // bn
    grid = (t // bm, d // bn, nbr)
    s_in, s_out, s_shapes, s_vmem = _side_plumbing(sides, grid)
    vmem = 2 * (bm * d * 2 + d * bn * 2 + bm * kin * 2 + nbr * kin * bn * 2 + bm * bn * 2) \
        + 4 * bm * bn * 4 + (4 << 20) + s_vmem
    wu_spec = pl.BlockSpec((kin, bn), lambda i, j, b: (0, j))
    return pl.pallas_call(
        functools.partial(_merge_body, n_side=len(sides)),
        grid=grid,
        in_specs=[pl.BlockSpec((bm, d), lambda i, j, b: (i, 0)),
                  pl.BlockSpec((bn, d), lambda i, j, b: (b * gstride + j, 0)),
                  pl.BlockSpec((1, bm, kin), lambda i, j, b: (b, i, 0)),
                  wu_spec, wu_spec, wu_spec] + s_in,
        out_specs=[pl.BlockSpec((bm, bn), lambda i, j, b: (i, j))] + s_out,
        out_shape=[jax.ShapeDtypeStruct((t, d), jnp.bfloat16)] + s_shapes,
        scratch_shapes=[pltpu.VMEM((bm, bn), jnp.float32)],
        compiler_params=_cparams(("arbitrary", "arbitrary", "arbitrary"), vmem),
        name="gated_merge",
    )(xb, wg, y, *wus, *[sd.src for sd in sides])


def _ln_body(r_ref, g_ref, b_ref, of_ref, ob_ref):
    r = r_ref[...]
    mu = jnp.mean(r, axis=-1, keepdims=True)
    xc = r - mu
    var = jnp.mean(xc * xc, axis=-1, keepdims=True)
    y = xc * lax.rsqrt(var + LN_EPS) * g_ref[...] + b_ref[...]
    of_ref[...] = y
    ob_ref[...] = y.astype(jnp.bfloat16)


def layer_norm(r, g, b, *, bm=256):
    t, d = r.shape
    bm = min(bm, t)
    vmem = 2 * (bm * d * 4 * 2 + bm * d * 2) + 4 * bm * d * 4 + (4 << 20)
    return pl.pallas_call(
        _ln_body,
        grid=(t // bm,),
        in_specs=[pl.BlockSpec((bm, d), lambda i: (i, 0)),
                  pl.BlockSpec((1, d), lambda i: (0, 0)),
                  pl.BlockSpec((1, d), lambda i: (0, 0))],
        out_specs=[pl.BlockSpec((bm, d), lambda i: (i, 0)),
                   pl.BlockSpec((bm, d), lambda i: (i, 0))],
        out_shape=[jax.ShapeDtypeStruct((t, d), jnp.float32),
                   jax.ShapeDtypeStruct((t, d), jnp.bfloat16)],
        compiler_params=_cparams(("parallel",), vmem),
        name="layer_norm",
    )(r, g.reshape(1, d), b.reshape(1, d))


def _swa_body(sink_ref, bias_ref, q_ref, kp_ref, kc_ref, vp_ref, vc_ref, ybuf_ref, o_ref):
    del ybuf_ref
    blk = SWA_BLK
    c2 = SWA_HEAD_DIM ** -0.5 * LOG2E
    lo = lax.broadcasted_iota(jnp.int32, (2 * blk, LANES), 1) < SWA_HEAD_DIM
    group = SWA_HEADS // SWA_KV_HEADS
    pairs = SWA_HEADS // 2
    kz, vz = [], []
    for g in range(SWA_KV_HEADS):
        kd = jnp.concatenate([kp_ref[:, g * LANES:(g + 1) * LANES],
                              kc_ref[:, g * LANES:(g + 1) * LANES]], axis=0)
        vd = jnp.concatenate([vp_ref[:, g * LANES:(g + 1) * LANES],
                              vc_ref[:, g * LANES:(g + 1) * LANES]], axis=0)
        zero = jnp.zeros_like(kd)
        kz.append(jnp.concatenate([jnp.where(lo, kd, zero), jnp.where(lo, zero, kd)], axis=0))
        vz.append(jnp.concatenate([jnp.where(lo, vd, zero), jnp.where(lo, zero, vd)], axis=0))
    s2 = [lax.dot_general(q_ref[:, p * LANES:(p + 1) * LANES], kz[p // (group // 2)],
                          (((1,), (1,)), ((), ())), preferred_element_type=jnp.float32)
          for p in range(pairs)]
    for p in range(pairs):
        probs = []
        for e in range(2):
            h = 2 * p + e
            sink = sink_ref[h] * LOG2E
            s = s2[p][:, e * 2 * blk:(e + 1) * 2 * blk] * c2 + bias_ref[h]
            m = jnp.maximum(jnp.max(s, axis=-1, keepdims=True), sink)
            pe = jnp.exp2(s - m)
            den = jnp.sum(pe, axis=-1, keepdims=True) + jnp.exp2(sink - m)
            probs.append((pe * (1.0 / den)).astype(jnp.bfloat16))
        p2 = jnp.concatenate(probs, axis=1)
        o_ref[:, p * LANES:(p + 1) * LANES] = jnp.dot(
            p2, vz[p // (group // 2)], preferred_element_type=jnp.float32).astype(o_ref.dtype)


def _swa_bias_table():
    blk = SWA_BLK
    qi = lax.broadcasted_iota(jnp.int32, (blk, 2 * blk), 0)
    kj = lax.broadcasted_iota(jnp.int32, (blk, 2 * blk), 1)
    dist = qi + blk - kj
    window = (dist >= 0) & (dist < SWA_WINDOW)
    slopes = 2.0 ** (-8.0 * jnp.arange(1, SWA_HEADS + 1, dtype=jnp.float32) / SWA_HEADS)
    bias = -(LOG2E * slopes)[:, None, None] * dist.astype(jnp.float32)[None]
    later = jnp.where(window[None], bias, NEG_INF)
    first = jnp.where((window & (kj >= blk))[None], bias, NEG_INF)
    return jnp.stack([first, later])


def swa_attention(zb, sinks, ybuf, bsz, seq):
    blk = SWA_BLK
    nb = seq // blk
    kcol, vcol = ZB_AK // (2 * LANES), ZB_AV // (2 * LANES)

    def cur(b, n):
        return b * nb + n

    def prev(b, n):
        return b * nb + jnp.maximum(n - 1, 0)

    return pl.pallas_call(
        _swa_body,
        grid=(bsz, nb),
        in_specs=[pl.BlockSpec(memory_space=pltpu.SMEM),
                  pl.BlockSpec((None, SWA_HEADS, blk, 2 * blk),
                               lambda b, n: (jnp.minimum(n, 1), 0, 0, 0)),
                  pl.BlockSpec((blk, A_Q), lambda b, n: (cur(b, n), ZB_AQ // A_Q)),
                  pl.BlockSpec((blk, 2 * LANES), lambda b, n: (prev(b, n), kcol)),
                  pl.BlockSpec((blk, 2 * LANES), lambda b, n: (cur(b, n), kcol)),
                  pl.BlockSpec((blk, 2 * LANES), lambda b, n: (prev(b, n), vcol)),
                  pl.BlockSpec((blk, 2 * LANES), lambda b, n: (cur(b, n), vcol)),
                  pl.BlockSpec(memory_space=pl.ANY)],
        out_specs=pl.BlockSpec((None, blk, A_Q), lambda b, n: (0, cur(b, n), 0)),
        out_shape=jax.ShapeDtypeStruct(ybuf.shape, ybuf.dtype),
        input_output_aliases={7: 0},
        compiler_params=_cparams(("parallel", "arbitrary"), 32 << 20),
        name="swa_attention",
    )(sinks, _swa_bias_table(), zb, zb, zb, zb, zb, ybuf)


def _mlstm_body(bias_ref, q_ref, k_ref, v_ref, og_ref, g_ref, ybuf_ref, y_ref, cum_ref,
                c_scr, n_scr, m_scr, carry_scr):
    del ybuf_ref
    c = pl.program_id(1)
    L = MLSTM_L
    dk, dv = MLSTM_QK_DIM, MLSTM_V_DIM

    @pl.when(c == 0)
    def _():
        c_scr[...] = jnp.zeros_like(c_scr)
        n_scr[...] = jnp.zeros_like(n_scr)
        m_scr[...] = jnp.zeros_like(m_scr)
        carry_scr[...] = jnp.zeros_like(carry_scr)

    nh = MLSTM_HEADS
    a = g_ref[...] + bias_ref[...]
    lf = jnp.minimum(a, 0.0) - jnp.log1p(jnp.exp(-jnp.abs(a)))
    row = lax.broadcasted_iota(jnp.int32, (L, L), 0)
    col = lax.broadcasted_iota(jnp.int32, (L, L), 1)
    causal = row >= col
    tri = causal.astype(jnp.float32)
    b_all = jnp.dot(tri, lf, preferred_element_type=jnp.float32,
                    precision=lax.Precision.HIGHEST)
    cum = b_all + carry_scr[0:1, :]
    carry_scr[0:1, :] = cum[L - 1:L, :]
    cum_ref[0] = cum.T

    b2 = pltpu.roll(b_all, LANES - nh, axis=1) * LOG2E
    g2 = a * LOG2E - b2
    rowi = lax.broadcasted_iota(jnp.int32, (L, LANES), 0)
    cm2 = g2
    d = 1
    while d < L:
        cm2 = jnp.maximum(cm2, jnp.where(rowi >= d, pltpu.roll(cm2, d, axis=0), NEG_INF))
        d *= 2
    m2_prev = m_scr[0:1, :]
    u2 = jnp.maximum(m2_prev, cm2)
    u2_last = u2[L - 1:L, :]
    w_inter_all = jnp.exp2(m2_prev - u2)
    floor_all = jnp.exp2(-(b2 + u2))
    wk_all = jnp.exp2(g2 - u2_last)
    decay_row = jnp.exp2(m2_prev - u2_last)
    m_scr[0:1, :] = b2[L - 1:L, :] + u2_last
    g2_t = g2.T

    lane = lax.broadcasted_iota(jnp.int32, (L, LANES), 1)
    lo = lane < dk
    crow = lax.broadcasted_iota(jnp.int32, (2 * dk, 2 * dv), 0)
    ccol = lax.broadcasted_iota(jnp.int32, (2 * dk, 2 * dv), 1)
    crow_lo = crow < dk
    blockdiag = crow_lo == (ccol < dv)
    nlane_lo = lax.broadcasted_iota(jnp.int32, (1, LANES), 1) < dk

    qps, kps, scs, cps, qcs = [], [], [], [], []
    for p in range(MLSTM_HEADS // 2):
        qp = q_ref[:, p * LANES:(p + 1) * LANES]
        kp = k_ref[:, p * LANES:(p + 1) * LANES] * (dk ** -0.5)
        kpb = kp.astype(jnp.bfloat16)
        zero = jnp.zeros_like(kpb)
        kz = jnp.concatenate([jnp.where(lo, kpb, zero), jnp.where(lo, zero, kpb)], axis=0)
        scs.append(lax.dot_general(qp, kz, (((1,), (1,)), ((), ())),
                                   preferred_element_type=jnp.float32))
        cp = c_scr[p]
        qcs.append(jnp.dot(qp, cp.astype(jnp.bfloat16), preferred_element_type=jnp.float32))
        qps.append(qp); kps.append(kp); cps.append(cp)

    for p in range(MLSTM_HEADS // 2):
        qp, kp, sc, cp, qc = qps[p], kps[p], scs[p], cps[p], qcs[p]
        n_row = n_scr[p:p + 1, :]
        qn_prod = qp.astype(jnp.float32) * n_row
        for e in range(2):
            h = 2 * p + e
            decay_mat = jnp.exp2(jnp.where(causal, g2_t[h:h + 1, :] - u2[:, h:h + 1], NEG_INF))
            smat = sc[:, e * L:(e + 1) * L] * decay_mat
            w_inter = w_inter_all[:, h:h + 1]
            qn = jnp.sum(jnp.where(lo == (e == 0), qn_prod, 0.0), axis=-1, keepdims=True)
            num = w_inter * qc[:, e * dv:(e + 1) * dv] + jnp.dot(
                smat.astype(jnp.bfloat16), v_ref[:, h * dv:(h + 1) * dv],
                preferred_element_type=jnp.float32)
            den = w_inter * qn + jnp.sum(smat, axis=-1, keepdims=True)
            rden = 1.0 / jnp.maximum(jnp.abs(den), floor_all[:, h:h + 1])
            og = og_ref[:, h * dv:(h + 1) * dv]
            y_ref[:, h * dv:(h + 1) * dv] = (num * rden * (1.0 / (1.0 + jnp.exp(-og)))).astype(y_ref.dtype)
        decays = [decay_row[:, 2 * p + e:2 * p + e + 1] for e in range(2)]
        kw = kp * jnp.where(lo, wk_all[:, 2 * p:2 * p + 1], wk_all[:, 2 * p + 1:2 * p + 2])
        n_scr[p:p + 1, :] = (jnp.where(nlane_lo, decays[0], decays[1]) * n_row
                             + jnp.sum(kw, axis=0, keepdims=True))
        upd = jnp.dot(kw.T.astype(jnp.bfloat16), v_ref[:, 2 * p * dv:(2 * p + 2) * dv],
                      preferred_element_type=jnp.float32)
        c_scr[p] = jnp.where(crow_lo, decays[0], decays[1]) * cp + jnp.where(blockdiag, upd, 0.0)


def mlstm(zb, zf, bias_row, ybuf, bsz, seq):
    L = MLSTM_L
    nc = seq // L

    def rows(b, c):
        return b * nc + c

    return pl.pallas_call(
        _mlstm_body,
        grid=(bsz, nc),
        in_specs=[pl.BlockSpec((1, LANES), lambda b, c: (0, 0)),
                  pl.BlockSpec((L, B_QK), lambda b, c: (rows(b, c), ZB_MQ // B_QK)),
                  pl.BlockSpec((L, B_QK), lambda b, c: (rows(b, c), ZF_MK // B_QK)),
                  pl.BlockSpec((L, B_V), lambda b, c: (rows(b, c), ZB_MV // B_V)),
                  pl.BlockSpec((L, B_V), lambda b, c: (rows(b, c), ZF_MO // B_V)),
                  pl.BlockSpec((L, LANES), lambda b, c: (rows(b, c), ZF_G // LANES)),
                  pl.BlockSpec(memory_space=pl.ANY)],
        out_specs=[pl.BlockSpec((None, L, B_V), lambda b, c: (1, rows(b, c), 0)),
                   pl.BlockSpec((1, LANES, L), lambda b, c: (b, 0, c))],
        out_shape=[jax.ShapeDtypeStruct(ybuf.shape, ybuf.dtype),
                   jax.ShapeDtypeStruct((bsz, LANES, seq), jnp.float32)],
        input_output_aliases={6: 0},
        scratch_shapes=[pltpu.VMEM((MLSTM_HEADS // 2, 2 * MLSTM_QK_DIM, 2 * MLSTM_V_DIM), jnp.float32),
                        pltpu.VMEM((8, LANES), jnp.float32),
                        pltpu.VMEM((8, LANES), jnp.float32),
                        pltpu.VMEM((8, LANES), jnp.float32)],
        compiler_params=_cparams(("arbitrary", "arbitrary"), 32 << 20),
        name="mlstm",
    )(bias_row, zb, zf, zb, zf, zf, ybuf)


def _fox_body(q_ref, k_ref, v_ref, ck_ref, ybuf_ref, o_ref):
    del ybuf_ref
    hp = pl.program_id(1)
    qi = pl.program_id(2)
    blk = FOX_BLK
    dh = FOX_HEAD_DIM
    c2 = FOX_HEAD_DIM ** -0.5 * LOG2E

    def scores(j, e):
        start = pl.multiple_of(j * blk, blk)
        s = lax.dot_general(q_ref[:, e * dh:(e + 1) * dh], k_ref[pl.ds(start, blk), e * dh:(e + 1) * dh],
                            (((1,), (1,)), ((), ())), preferred_element_type=jnp.float32)
        return s * c2 - ck_ref[0, 2 * hp + e, pl.ds(j, 1), :] * LOG2E

    def update(s, j, e, m, l, acc):
        start = pl.multiple_of(j * blk, blk)
        m_new = jnp.maximum(m, jnp.max(s, axis=-1, keepdims=True))
        alpha = jnp.exp2(m - m_new)
        p = jnp.exp2(s - m_new)
        l = alpha * l + jnp.sum(p, axis=-1, keepdims=True)
        acc = alpha * acc + jnp.dot(p.astype(jnp.bfloat16),
                                    v_ref[pl.ds(start, blk), e * dh:(e + 1) * dh],
                                    preferred_element_type=jnp.float32)
        return m_new, l, acc

    def step(j, carry):
        s = [scores(j, e) for e in range(2)]
        return tuple(update(s[e], j, e, *carry[e]) for e in range(2))

    init1 = (jnp.full((blk, 1), NEG_INF, jnp.float32), jnp.zeros((blk, 1), jnp.float32),
             jnp.zeros((blk, dh), jnp.float32))
    carry = lax.fori_loop(0, qi, step, (init1, init1))
    row = lax.broadcasted_iota(jnp.int32, (blk, blk), 0)
    colm = lax.broadcasted_iota(jnp.int32, (blk, blk), 1)
    s = [jnp.where(colm <= row, scores(qi, e), NEG_INF) for e in range(2)]
    for e in range(2):
        _, l, acc = update(s[e], qi, e, *carry[e])
        o_ref[:, e * dh:(e + 1) * dh] = (acc / l).astype(o_ref.dtype)


def fox_attention(zb, cum_t, ybuf, bsz, seq):
    blk = FOX_BLK
    nq = seq // blk
    pw = 2 * FOX_HEAD_DIM
    ck = cum_t.reshape(bsz, LANES, nq, blk)
    return pl.pallas_call(
        _fox_body,
        grid=(bsz, FOX_HEADS // 2, nq),
        in_specs=[pl.BlockSpec((blk, pw), lambda b, h, i: (b * nq + i, ZB_CQ // pw + h)),
                  pl.BlockSpec((seq, pw), lambda b, h, i: (b, ZB_CK // pw + h)),
                  pl.BlockSpec((seq, pw), lambda b, h, i: (b, ZB_CV // pw + h)),
                  pl.BlockSpec((1, 8, nq, blk), lambda b, h, i: (b, 2, 0, 0)),
                  pl.BlockSpec(memory_space=pl.ANY)],
        out_specs=pl.BlockSpec((None, blk, pw), lambda b, h, i: (2, b * nq + i, h)),
        out_shape=jax.ShapeDtypeStruct(ybuf.shape, ybuf.dtype),
        input_output_aliases={4: 0},
        compiler_params=_cparams(("parallel", "arbitrary", "arbitrary"), 32 << 20),
        name="fox_attention",
    )(zb, zb, zb, ck, ybuf)


RG_COPY, RG_DUP, RG_GATES = 0, 1, 2
_SEG = dict(a_q=0, a_k=1, a_v=2, m_q=3, m_k=4, m_v=5, m_i=6, m_f=7, m_o=8,
            c_q=9, c_k=10, c_v=11, c_f=12, g_a=13, g_b=14, g_c=15)


def _regroup_table():
    tiles = []

    def copy(name):
        off, width = SEG_OFFS[_SEG[name]], SEG_WIDTHS[_SEG[name]]
        assert width % WT == 0 and off % SUBLANES == 0
        tiles.extend((off + k * WT, RG_COPY) for k in range(width // WT))

    copy("m_o"); copy("m_k")
    m_i, m_f, c_f = (SEG_OFFS[_SEG[n]] for n in ("m_i", "m_f", "c_f"))
    assert m_f == m_i + MLSTM_HEADS and m_i % SUBLANES == 0 and c_f % SUBLANES == 0
    tiles.append((m_i, RG_GATES))
    assert len(tiles) * WT == WA_ZB
    for name in ("a_q", "m_v", "c_q", "c_k", "c_v", "m_q"):
        copy(name)
    a_k, a_v = SEG_OFFS[_SEG["a_k"]], SEG_OFFS[_SEG["a_v"]]
    assert a_v == a_k + A_KV and a_k % SUBLANES == 0 and 4 * A_KV == WT
    tiles.append((a_k, RG_DUP))
    assert len(tiles) * WT == WA_N
    return tiles, c_f


def _regroup_body(base_ref, mode_ref, w_ref, aux_ref, o_ref):
    del base_ref
    mode = mode_ref[pl.program_id(0)]
    bf = jnp.bfloat16
    hd = SWA_HEAD_DIM

    @pl.when(mode == RG_COPY)
    def _():
        o_ref[...] = w_ref[0].astype(bf)

    @pl.when(mode == RG_DUP)
    def _():
        for i in range(2 * SWA_KV_HEADS):
            head = w_ref[0, i * hd:(i + 1) * hd, :].astype(bf)
            o_ref[2 * i * hd:(2 * i + 1) * hd, :] = head
            o_ref[(2 * i + 1) * hd:(2 * i + 2) * hd, :] = head

    @pl.when(mode == RG_GATES)
    def _():
        nh = MLSTM_HEADS
        o_ref[0:2 * nh, :] = w_ref[0, 0:2 * nh, :].astype(bf)
        o_ref[2 * nh:3 * nh, :] = aux_ref[0].astype(bf)
        o_ref[3 * nh:WT, :] = jnp.zeros((WT - 3 * nh, o_ref.shape[1]), bf)


def regroup_w_in(wt, l):
    tiles, c_f = _regroup_table()
    base = jnp.asarray([b for b, _ in tiles], jnp.int32)
    mode = jnp.asarray([m for _, m in tiles], jnp.int32)
    k = wt.shape[2]
    el = pl.Element
    in_specs = [pl.BlockSpec((el(1), el(WT), el(k)),
                             lambda t, base, mode: (l, pl.multiple_of(base[t], SUBLANES), 0)),
                pl.BlockSpec((el(1), el(FOX_HEADS), el(k)), lambda t, base, mode: (l, c_f, 0))]
    vmem = 2 * (WT * k * 4 + WT * k * 2) + (12 << 20)
    return pl.pallas_call(
        _regroup_body,
        grid_spec=pltpu.PrefetchScalarGridSpec(
            num_scalar_prefetch=2, grid=(len(tiles),), in_specs=in_specs,
            out_specs=pl.BlockSpec((WT, k), lambda t, base, mode: (t, 0))),
        out_shape=jax.ShapeDtypeStruct((WA_N, k), jnp.bfloat16),
        compiler_params=_cparams(("arbitrary",), vmem),
        name="regroup_w_in",
    )(base, mode, wt, wt)


def _cast_body(w_ref, o_ref):
    o_ref[...] = w_ref[...].astype(o_ref.dtype)


def cast_bf16(w, l=None, *, br=1024, bc=2048):
    r, c = w.shape[-2:]
    br, bc = min(br, r), min(bc, c)
    assert r % br == 0 and c % bc == 0
    if l is None:
        in_spec = pl.BlockSpec((br, bc), lambda i, j: (i, j))
    else:
        in_spec = pl.BlockSpec((None, br, bc), lambda i, j: (l, i, j))
    return pl.pallas_call(
        _cast_body,
        grid=(r // br, c // bc),
        in_specs=[in_spec],
        out_specs=pl.BlockSpec((br, bc), lambda i, j: (i, j)),
        out_shape=jax.ShapeDtypeStruct((r, c), jnp.bfloat16),
        compiler_params=_cparams(("parallel", "parallel"), 2 * br * bc * 6 + (8 << 20)),
        name="cast_bf16",
    )(w)


def kernel(x, w_in, b_mlstm_i, b_mlstm_f, b_fox_f, attn_sinks, w_up_swa, w_up_mlstm, w_up_fox,
           w_o, ln1_g, ln1_b, w_ff1, w_ff2, ln2_g, ln2_b):
    bsz, seq, d = x.shape
    t = bsz * seq
    assert d == D_MODEL and seq % FOX_BLK == 0 and seq % MLSTM_L == 0
    xf = x.reshape(t, d).astype(jnp.float32)
    xb = cast_bf16(xf)
    gate_pad = jnp.zeros((LANES - 3 * MLSTM_HEADS,), jnp.float32)
    ups = (w_up_swa, w_up_mlstm, w_up_fox)
    w_ups = [cast_bf16(w, 0) for w in ups]
    w_ob = cast_bf16(w_o, 0)
    w_int = jnp.swapaxes(w_in, 1, 2)
    for l in range(DEPTH):
        w_all = regroup_w_in(w_int, l)
        bias_row = jnp.concatenate([b_mlstm_i[l], b_mlstm_f[l], b_fox_f[l], gate_pad]).reshape(1, LANES)
        (zf,) = matmul(xb, w_all, n=ZF_N, w_row=WA_ZF, bm=512, bn=ZF_N, out_dtype=jnp.float32,
                       name="in_proj_f32")
        gates = Side(w_int, l, 128, row_off=SEG_OFFS[_SEG["g_a"]], n_rows=N_BRANCH * D_MODEL)
        zb, w_g = matmul(xb, w_all, n=ZB_N, w_row=WA_ZB, bm=1024, bn=512, out_dtype=jnp.bfloat16,
                         name="in_proj_bf16", sides=[gates])
        ybuf = jnp.zeros((3, t, A_Q), jnp.bfloat16)
        ybuf = swa_attention(zb, attn_sinks[l].astype(jnp.float32), ybuf, bsz, seq)
        ybuf, cum_t = mlstm(zb, zf, bias_row.astype(jnp.float32), ybuf, bsz, seq)
        ybuf = fox_attention(zb, cum_t, ybuf, bsz, seq)
        mix, w_1b = gated_merge(xb, w_g, ybuf, w_ups, bm=1024, bn=512, sides=[Side(w_ff1, l, 32)])
        (r1,) = matmul_residual(mix, w_ob, xf, bm=1024, bn=512, bk=D_MODEL, name="out_proj")
        xf, xb = layer_norm(r1, ln1_g[l], ln1_b[l])
        hid, w_2b = matmul(xb, w_1b, bm=1024, bn=1024, out_dtype=jnp.bfloat16, act="relu2",
                           name="ff1", sides=[Side(w_ff2, l, 128)])
        nxt = [Side(w, l + 1, 16) for w in (w_o,) + ups] if l + 1 < DEPTH else []
        r2, *cast_next = matmul_residual(hid, w_2b, xf, bm=1024, bn=1024, bk=2048, name="ff2", sides=nxt)
        if cast_next:
            w_ob, *w_ups = cast_next
        xf, xb = layer_norm(r2, ln2_g[l], ln2_b[l])
    return xf.reshape(bsz, seq, d).astype(x.dtype)
```

```python
import functools
from typing import NamedTuple

import jax
import jax.numpy as jnp
from jax import lax
from jax.experimental import pallas as pl
from jax.experimental.pallas import tpu as pltpu

D_MODEL = 4096
SWA_HEADS, SWA_KV_HEADS, SWA_HEAD_DIM, SWA_WINDOW = 16, 2, 64, 128
MLSTM_HEADS, MLSTM_QK_DIM, MLSTM_V_DIM = 8, 64, 128
FOX_HEADS, FOX_HEAD_DIM = 8, 128
D_FF = 4 * D_MODEL
LN_EPS = 1e-5
DEPTH = 2
DN_ALPHA = (2 * DEPTH) ** 0.25

A_Q = SWA_HEADS * SWA_HEAD_DIM
A_KV = SWA_KV_HEADS * SWA_HEAD_DIM
B_QK = MLSTM_HEADS * MLSTM_QK_DIM
B_V = MLSTM_HEADS * MLSTM_V_DIM
C_W = FOX_HEADS * FOX_HEAD_DIM
SEG_WIDTHS = (A_Q, A_KV, A_KV, B_QK, B_QK, B_V, MLSTM_HEADS, MLSTM_HEADS, B_V,
              C_W, C_W, C_W, FOX_HEADS, D_MODEL, D_MODEL, D_MODEL)
SEG_OFFS = tuple(sum(SEG_WIDTHS[:i]) for i in range(len(SEG_WIDTHS)))

LANES = 128
SUBLANES = 8
VMEM_LIMIT_CAP = 56 * 1024 * 1024

ZB_AQ = 0
ZB_MV = 1024
ZB_CQ = 2048
ZB_CK = 3072
ZB_CV = 4096
ZB_MQ = 5120
ZB_AK = 5632
ZB_AV = 5888
ZB_N = 6144
ZF_MO = 0
ZF_MK = 1024
ZF_G = 1536
ZF_N = 1664
WT = 512
WA_ZF = 0
WA_ZB = 2048
WA_N = WA_ZB + ZB_N

MLSTM_L = 256
FOX_BLK = 512
SWA_BLK = 128
NEG_INF = float("-inf")
LOG2E = 1.4426950408889634


def _cparams(sem, vmem_bytes):
    return pltpu.CompilerParams(dimension_semantics=sem,
                                vmem_limit_bytes=int(min(vmem_bytes, VMEM_LIMIT_CAP)))


_NT = (((1,), (1,)), ((), ()))


class Side(NamedTuple):
    src: jax.Array
    layer: int
    rows: int
    row_off: int = 0
    n_rows: int = 0


def _side_plumbing(sides, grid):
    steps = 1
    for n in grid:
        steps *= n

    def lin(*g):
        s = g[0]
        for a, n in zip(g[1:], grid[1:]):
            s = s * n + a
        return s

    in_specs, out_specs, out_shapes, vmem = [], [], [], 0
    for sd in sides:
        c = sd.src.shape[-1]
        r = sd.n_rows or sd.src.shape[-2]
        rows = sd.rows
        while r // rows > steps:
            rows *= 2
        assert r % rows == 0 and sd.row_off % SUBLANES == 0
        last = r // rows - 1

        def blk(*g, last=last):
            return jnp.minimum(lin(*g), last)

        if sd.row_off or sd.n_rows:
            el = pl.Element
            in_specs.append(pl.BlockSpec(
                (el(1), el(rows), el(c)),
                lambda *g, blk=blk, sd=sd, rows=rows: (
                    sd.layer, pl.multiple_of(sd.row_off + blk(*g) * rows, SUBLANES), 0)))
        else:
            in_specs.append(pl.BlockSpec((None, rows, c),
                                         lambda *g, blk=blk, l=sd.layer: (l, blk(*g), 0)))
        out_specs.append(pl.BlockSpec((rows, c), lambda *g, blk=blk: (blk(*g), 0)))
        out_shapes.append(jax.ShapeDtypeStruct((r, c), jnp.bfloat16))
        vmem += 2 * rows * c * (4 + 2)
    return in_specs, out_specs, out_shapes, vmem


def _cast_sides(side_in, side_out):
    for si, so in zip(side_in, side_out):
        so[...] = si[...].reshape(so.shape).astype(so.dtype)


def _mm_body(x_ref, w_ref, *rest, act, w_t, n_side):
    side_in, o_ref, side_out = rest[:n_side], rest[n_side], rest[n_side + 1:]
    _cast_sides(side_in, side_out)
    if w_t:
        acc = lax.dot_general(x_ref[...], w_ref[...], _NT, preferred_element_type=jnp.float32)
    else:
        acc = jnp.dot(x_ref[...], w_ref[...], preferred_element_type=jnp.float32)
    if act == "relu2":
        acc = jnp.square(jnp.maximum(acc, 0.0))
    o_ref[...] = acc.astype(o_ref.dtype)


def _residual(res_refs):
    if len(res_refs) == 1:
        return res_refs[0][...]
    return _ln_apply(*(ref[...] for ref in res_refs))


def _mm_res_body(x_ref, w_ref, *rest):
    res_refs, o_ref = rest[:-1], rest[-1]
    acc = jnp.dot(x_ref[...], w_ref[...], preferred_element_type=jnp.float32)
    o_ref[...] = DN_ALPHA * _residual(res_refs) + acc


def _mm_res_k_body(x_ref, w_ref, *rest, n_res, n_side):
    res_refs, rest = rest[:n_res], rest[n_res:]
    side_in, o_ref, side_out = rest[:n_side], rest[n_side], rest[n_side + 1:]
    k = pl.program_id(2)

    @pl.when(k == 0)
    def _():
        _cast_sides(side_in, side_out)
        o_ref[...] = DN_ALPHA * _residual(res_refs) + jnp.dot(x_ref[...], w_ref[...],
                                                              preferred_element_type=jnp.float32)

    @pl.when(k > 0)
    def _():
        _cast_sides(side_in, side_out)
        o_ref[...] += jnp.dot(x_ref[...], w_ref[...], preferred_element_type=jnp.float32)


def matmul(x, w, *, bm, bn, out_dtype, act=None, name, n=None, w_row=None, sides=()):
    m, k = x.shape
    w_t = w_row is not None
    n = w.shape[1] if n is None else n
    bm, bn = min(bm, m), min(bn, n)
    assert m % bm == 0 and n % bn == 0
    osz = jnp.dtype(out_dtype).itemsize
    grid = (m // bm, n // bn)
    s_in, s_out, s_shapes, s_vmem = _side_plumbing(sides, grid)
    vmem = 2 * (bm * k * 2 + k * bn * 2 + bm * bn * osz) + bm * bn * 4 + (4 << 20) + s_vmem
    if w_t:
        assert w_row % bn == 0
        off = w_row // bn
        w_spec = pl.BlockSpec((bn, k), lambda i, j: (off + j, 0))
    else:
        w_spec = pl.BlockSpec((k, bn), lambda i, j: (0, j))
    return pl.pallas_call(
        functools.partial(_mm_body, act=act, w_t=w_t, n_side=len(sides)),
        grid=grid,
        in_specs=[pl.BlockSpec((bm, k), lambda i, j: (i, 0)), w_spec] + s_in,
        out_specs=[pl.BlockSpec((bm, bn), lambda i, j: (i, j))] + s_out,
        out_shape=[jax.ShapeDtypeStruct((m, n), out_dtype)] + s_shapes,
        compiler_params=_cparams(("arbitrary", "arbitrary"), vmem),
        name=name,
    )(x, w, *[sd.src for sd in sides])


def matmul_residual(x, w, res, *, bm, bn, bk, name, sides=()):
    m, k = x.shape
    n = w.shape[1]
    bm, bn, bk = min(bm, m), min(bn, n), min(bk, k)
    assert m % bm == 0 and n % bn == 0 and k % bk == 0
    res_args = tuple(res) if isinstance(res, LnRes) else (res,)
    res_dims = [(bm, bn, True, True), (bm, 1, True, False), (bm, 1, True, False),
                (1, bn, False, True), (1, bn, False, True)][:len(res_args)]

    def res_specs(ij):
        return [pl.BlockSpec((r, c), lambda *g, ri=ri, ci=ci: (ij(*g)[0] if ri else 0, ij(*g)[1] if ci else 0))
                for r, c, ri, ci in res_dims]

    if bk == k:
        assert not sides
        vmem = 2 * (bm * k * 2 + k * bn * 2 + 2 * bm * bn * 4) + bm * bn * 4 + (4 << 20)
        return [pl.pallas_call(
            _mm_res_body,
            grid=(m // bm, n // bn),
            in_specs=[pl.BlockSpec((bm, k), lambda i, j: (i, 0)),
                      pl.BlockSpec((k, bn), lambda i, j: (0, j))] + res_specs(lambda i, j: (i, j)),
            out_specs=pl.BlockSpec((bm, bn), lambda i, j: (i, j)),
            out_shape=jax.ShapeDtypeStruct((m, n), jnp.float32),
            compiler_params=_cparams(("parallel", "arbitrary"), vmem),
            name=name,
        )(x, w, *res_args)]
    grid = (m // bm, n // bn, k // bk)
    s_in, s_out, s_shapes, s_vmem = _side_plumbing(sides, grid)
    vmem = 2 * (bm * bk * 2 + bk * bn * 2 + 2 * bm * bn * 4) + 2 * bm * bn * 4 + (4 << 20) + s_vmem
    return pl.pallas_call(
        functools.partial(_mm_res_k_body, n_res=len(res_args), n_side=len(sides)),
        grid=grid,
        in_specs=[pl.BlockSpec((bm, bk), lambda i, j, kk: (i, kk)),
                  pl.BlockSpec((bk, bn), lambda i, j, kk: (kk, j))]
        + res_specs(lambda i, j, kk: (i, j)) + s_in,
        out_specs=[pl.BlockSpec((bm, bn), lambda i, j, kk: (i, j))] + s_out,
        out_shape=[jax.ShapeDtypeStruct((m, n), jnp.float32)] + s_shapes,
        compiler_params=_cparams(("arbitrary", "arbitrary", "arbitrary"), vmem),
        name=name,
    )(x, w, *res_args, *[sd.src for sd in sides])


N_BRANCH = 3


def _merge_body(x_ref, wg_ref, y_ref, wu0_ref, wu1_ref, wu2_ref, *rest, n_side):
    side_in, o_ref, side_out, acc_ref = rest[:n_side], rest[n_side], rest[n_side + 1:-1], rest[-1]
    br = pl.program_id(2)

    def contrib(wu_ref):
        _cast_sides(side_in, side_out)
        g = lax.dot_general(x_ref[...], wg_ref[...], _NT, preferred_element_type=jnp.float32)
        u = jnp.dot(y_ref[0], wu_ref[...], preferred_element_type=jnp.float32)
        return u * (1.0 / (1.0 + jnp.exp(-g)))

    @pl.when(br == 0)
    def _():
        acc_ref[...] = contrib(wu0_ref)

    @pl.when(br == 1)
    def _():
        acc_ref[...] += contrib(wu1_ref)

    @pl.when(br == 2)
    def _():
        o_ref[...] = (acc_ref[...] + contrib(wu2_ref)).astype(o_ref.dtype)


def gated_merge(xb, wg, y, wus, *, bm, bn, sides=()):
    t, d = xb.shape
    nbr, _, kin = y.shape
    assert nbr == N_BRANCH == len(wus) and wg.shape == (nbr * d, d)
    bm, bn = min(bm, t), min(bn, d)
    assert d % bn == 0
    gstride = d // bn
    grid = (t // bm, d // bn, nbr)
    s_in, s_out, s_shapes, s_vmem = _side_plumbing(sides, grid)
    vmem = 2 * (bm * d * 2 + d * bn * 2 + bm * kin * 2 + nbr * kin * bn * 2 + bm * bn * 2) \
        + 4 * bm * bn * 4 + (4 << 20) + s_vmem
    wu_spec = pl.BlockSpec((kin, bn), lambda i, j, b: (0, j))
    return pl.pallas_call(
        functools.partial(_merge_body, n_side=len(sides)),
        grid=grid,
        in_specs=[pl.BlockSpec((bm, d), lambda i, j, b: (i, 0)),
                  pl.BlockSpec((bn, d), lambda i, j, b: (b * gstride + j, 0)),
                  pl.BlockSpec((1, bm, kin), lambda i, j, b: (b, i, 0)),
                  wu_spec, wu_spec, wu_spec] + s_in,
        out_specs=[pl.BlockSpec((bm, bn), lambda i, j, b: (i, j))] + s_out,
        out_shape=[jax.ShapeDtypeStruct((t, d), jnp.bfloat16)] + s_shapes,
        scratch_shapes=[pltpu.VMEM((bm, bn), jnp.float32)],
        compiler_params=_cparams(("arbitrary", "arbitrary", "arbitrary"), vmem),
        name="gated_merge",
    )(xb, wg, y, *wus, *[sd.src for sd in sides])


class LnRes(NamedTuple):
    r: jax.Array
    mu: jax.Array
    rstd: jax.Array
    g: jax.Array
    b: jax.Array


def _ln_apply(r, mu, rstd, g, b):
    return (r - mu) * rstd * g + b


def _ln_body(r_ref, g_ref, b_ref, *outs, emit_f32):
    r = r_ref[...]
    mu = jnp.mean(r, axis=-1, keepdims=True)
    xc = r - mu
    rstd = lax.rsqrt(jnp.mean(xc * xc, axis=-1, keepdims=True) + LN_EPS)
    y = _ln_apply(r, mu, rstd, g_ref[...], b_ref[...])
    if emit_f32:
        of_ref, ob_ref = outs
        of_ref[...] = y
    else:
        ob_ref, mu_ref, rstd_ref = outs
        mu_ref[...] = mu
        rstd_ref[...] = rstd
    ob_ref[...] = y.astype(jnp.bfloat16)


def layer_norm(r, g, b, *, emit_f32, bm=256):
    t, d = r.shape
    bm = min(bm, t)
    g2, b2 = g.reshape(1, d), b.reshape(1, d)
    row = pl.BlockSpec((bm, d), lambda i: (i, 0))
    stat = pl.BlockSpec((bm, 1), lambda i: (i, 0))
    vmem = 2 * (bm * d * 4 * 2 + bm * d * 2) + 4 * bm * d * 4 + (4 << 20)
    if emit_f32:
        out_specs = [row, row]
        out_shape = [jax.ShapeDtypeStruct((t, d), jnp.float32), jax.ShapeDtypeStruct((t, d), jnp.bfloat16)]
    else:
        out_specs = [row, stat, stat]
        out_shape = [jax.ShapeDtypeStruct((t, d), jnp.bfloat16),
                     jax.ShapeDtypeStruct((t, 1), jnp.float32), jax.ShapeDtypeStruct((t, 1), jnp.float32)]
    outs = pl.pallas_call(
        functools.partial(_ln_body, emit_f32=emit_f32),
        grid=(t // bm,),
        in_specs=[row, pl.BlockSpec((1, d), lambda i: (0, 0)), pl.BlockSpec((1, d), lambda i: (0, 0))],
        out_specs=out_specs,
        out_shape=out_shape,
        compiler_params=_cparams(("parallel",), vmem),
        name="layer_norm",
    )(r, g2, b2)
    if emit_f32:
        return outs[0], outs[1]
    xb, mu, rstd = outs
    return LnRes(r, mu, rstd, g2, b2), xb


def _swa_body(sink_ref, bias_ref, q_ref, kp_ref, kc_ref, vp_ref, vc_ref, ybuf_ref, o_ref):
    del ybuf_ref
    blk = SWA_BLK
    c2 = SWA_HEAD_DIM ** -0.5 * LOG2E
    lo = lax.broadcasted_iota(jnp.int32, (2 * blk, LANES), 1) < SWA_HEAD_DIM
    group = SWA_HEADS // SWA_KV_HEADS
    pairs = SWA_HEADS // 2
    kz, vz = [], []
    for g in range(SWA_KV_HEADS):
        kd = jnp.concatenate([kp_ref[:, g * LANES:(g + 1) * LANES],
                              kc_ref[:, g * LANES:(g + 1) * LANES]], axis=0)
        vd = jnp.concatenate([vp_ref[:, g * LANES:(g + 1) * LANES],
                              vc_ref[:, g * LANES:(g + 1) * LANES]], axis=0)
        zero = jnp.zeros_like(kd)
        kz.append(jnp.concatenate([jnp.where(lo, kd, zero), jnp.where(lo, zero, kd)], axis=0))
        vz.append(jnp.concatenate([jnp.where(lo, vd, zero), jnp.where(lo, zero, vd)], axis=0))
    s2 = [lax.dot_general(q_ref[:, p * LANES:(p + 1) * LANES], kz[p // (group // 2)],
                          (((1,), (1,)), ((), ())), preferred_element_type=jnp.float32)
          for p in range(pairs)]
    for p in range(pairs):
        probs = []
        for e in range(2):
            h = 2 * p + e
            sink = sink_ref[h] * LOG2E
            s = s2[p][:, e * 2 * blk:(e + 1) * 2 * blk] * c2 + bias_ref[h]
            m = jnp.maximum(jnp.max(s, axis=-1, keepdims=True), sink)
            pe = jnp.exp2(s - m)
            den = jnp.sum(pe, axis=-1, keepdims=True) + jnp.exp2(sink - m)
            probs.append((pe * (1.0 / den)).astype(jnp.bfloat16))
        p2 = jnp.concatenate(probs, axis=1)
        o_ref[:, p * LANES:(p + 1) * LANES] = jnp.dot(
            p2, vz[p // (group // 2)], preferred_element_type=jnp.float32).astype(o_ref.dtype)


def _swa_bias_table():
    blk = SWA_BLK
    qi = lax.broadcasted_iota(jnp.int32, (blk, 2 * blk), 0)
    kj = lax.broadcasted_iota(jnp.int32, (blk, 2 * blk), 1)
    dist = qi + blk - kj
    window = (dist >= 0) & (dist < SWA_WINDOW)
    slopes = 2.0 ** (-8.0 * jnp.arange(1, SWA_HEADS + 1, dtype=jnp.float32) / SWA_HEADS)
    bias = -(LOG2E * slopes)[:, None, None] * dist.astype(jnp.float32)[None]
    later = jnp.where(window[None], bias, NEG_INF)
    first = jnp.where((window & (kj >= blk))[None], bias, NEG_INF)
    return jnp.stack([first, later])


def swa_attention(zb, sinks, ybuf, bsz, seq):
    blk = SWA_BLK
    nb = seq // blk
    kcol, vcol = ZB_AK // (2 * LANES), ZB_AV // (2 * LANES)

    def cur(b, n):
        return b * nb + n

    def prev(b, n):
        return b * nb + jnp.maximum(n - 1, 0)

    return pl.pallas_call(
        _swa_body,
        grid=(bsz, nb),
        in_specs=[pl.BlockSpec(memory_space=pltpu.SMEM),
                  pl.BlockSpec((None, SWA_HEADS, blk, 2 * blk),
                               lambda b, n: (jnp.minimum(n, 1), 0, 0, 0)),
                  pl.BlockSpec((blk, A_Q), lambda b, n: (cur(b, n), ZB_AQ // A_Q)),
                  pl.BlockSpec((blk, 2 * LANES), lambda b, n: (prev(b, n), kcol)),
                  pl.BlockSpec((blk, 2 * LANES), lambda b, n: (cur(b, n), kcol)),
                  pl.BlockSpec((blk, 2 * LANES), lambda b, n: (prev(b, n), vcol)),
                  pl.BlockSpec((blk, 2 * LANES), lambda b, n: (cur(b, n), vcol)),
                  pl.BlockSpec(memory_space=pl.ANY)],
        out_specs=pl.BlockSpec((None, blk, A_Q), lambda b, n: (0, cur(b, n), 0)),
        out_shape=jax.ShapeDtypeStruct(ybuf.shape, ybuf.dtype),
        input_output_aliases={7: 0},
        compiler_params=_cparams(("parallel", "arbitrary"), 32 << 20),
        name="swa_attention",
    )(sinks, _swa_bias_table(), zb, zb, zb, zb, zb, ybuf)


def _mlstm_body(bias_ref, q_ref, k_ref, v_ref, og_ref, g_ref, ybuf_ref, y_ref, cum_ref,
                c_scr, n_scr, m_scr, carry_scr):
    del ybuf_ref
    c = pl.program_id(1)
    L = MLSTM_L
    dk, dv = MLSTM_QK_DIM, MLSTM_V_DIM

    @pl.when(c == 0)
    def _():
        c_scr[...] = jnp.zeros_like(c_scr)
        n_scr[...] = jnp.zeros_like(n_scr)
        m_scr[...] = jnp.zeros_like(m_scr)
        carry_scr[...] = jnp.zeros_like(carry_scr)

    nh = MLSTM_HEADS
    a = g_ref[...] + bias_ref[...]
    lf = jnp.minimum(a, 0.0) - jnp.log1p(jnp.exp(-jnp.abs(a)))
    row = lax.broadcasted_iota(jnp.int32, (L, L), 0)
    col = lax.broadcasted_iota(jnp.int32, (L, L), 1)
    causal = row >= col
    tri = causal.astype(jnp.float32)
    b_all = jnp.dot(tri, lf, preferred_element_type=jnp.float32,
                    precision=lax.Precision.HIGHEST)
    cum = b_all + carry_scr[0:1, :]
    carry_scr[0:1, :] = cum[L - 1:L, :]
    cum_ref[0] = cum.T

    b2 = pltpu.roll(b_all, LANES - nh, axis=1) * LOG2E
    g2 = a * LOG2E - b2
    rowi = lax.broadcasted_iota(jnp.int32, (L, LANES), 0)
    cm2 = g2
    d = 1
    while d < L:
        cm2 = jnp.maximum(cm2, jnp.where(rowi >= d, pltpu.roll(cm2, d, axis=0), NEG_INF))
        d *= 2
    m2_prev = m_scr[0:1, :]
    u2 = jnp.maximum(m2_prev, cm2)
    u2_last = u2[L - 1:L, :]
    w_inter_all = jnp.exp2(m2_prev - u2)
    floor_all = jnp.exp2(-(b2 + u2))
    wk_all = jnp.exp2(g2 - u2_last)
    decay_row = jnp.exp2(m2_prev - u2_last)
    m_scr[0:1, :] = b2[L - 1:L, :] + u2_last
    g2_t = g2.T

    lane = lax.broadcasted_iota(jnp.int32, (L, LANES), 1)
    lo = lane < dk
    crow = lax.broadcasted_iota(jnp.int32, (2 * dk, 2 * dv), 0)
    ccol = lax.broadcasted_iota(jnp.int32, (2 * dk, 2 * dv), 1)
    crow_lo = crow < dk
    blockdiag = crow_lo == (ccol < dv)
    nlane_lo = lax.broadcasted_iota(jnp.int32, (1, LANES), 1) < dk

    qps, kps, scs, cps, qcs = [], [], [], [], []
    for p in range(MLSTM_HEADS // 2):
        qp = q_ref[:, p * LANES:(p + 1) * LANES]
        kp = k_ref[:, p * LANES:(p + 1) * LANES] * (dk ** -0.5)
        kpb = kp.astype(jnp.bfloat16)
        zero = jnp.zeros_like(kpb)
        kz = jnp.concatenate([jnp.where(lo, kpb, zero), jnp.where(lo, zero, kpb)], axis=0)
        scs.append(lax.dot_general(qp, kz, (((1,), (1,)), ((), ())),
                                   preferred_element_type=jnp.float32))
        cp = c_scr[p]
        qcs.append(jnp.dot(qp, cp.astype(jnp.bfloat16), preferred_element_type=jnp.float32))
        qps.append(qp); kps.append(kp); cps.append(cp)

    for p in range(MLSTM_HEADS // 2):
        qp, kp, sc, cp, qc = qps[p], kps[p], scs[p], cps[p], qcs[p]
        n_row = n_scr[p:p + 1, :]
        qn_prod = qp.astype(jnp.float32) * n_row
        for e in range(2):
            h = 2 * p + e
            decay_mat = jnp.exp2(jnp.where(causal, g2_t[h:h + 1, :] - u2[:, h:h + 1], NEG_INF))
            smat = sc[:, e * L:(e + 1) * L] * decay_mat
            w_inter = w_inter_all[:, h:h + 1]
            qn = jnp.sum(jnp.where(lo == (e == 0), qn_prod, 0.0), axis=-1, keepdims=True)
            num = w_inter * qc[:, e * dv:(e + 1) * dv] + jnp.dot(
                smat.astype(jnp.bfloat16), v_ref[:, h * dv:(h + 1) * dv],
                preferred_element_type=jnp.float32)
            den = w_inter * qn + jnp.sum(smat, axis=-1, keepdims=True)
            rden = 1.0 / jnp.maximum(jnp.abs(den), floor_all[:, h:h + 1])
            og = og_ref[:, h * dv:(h + 1) * dv]
            y_ref[:, h * dv:(h + 1) * dv] = (num * rden * (1.0 / (1.0 + jnp.exp(-og)))).astype(y_ref.dtype)
        decays = [decay_row[:, 2 * p + e:2 * p + e + 1] for e in range(2)]
        kw = kp * jnp.where(lo, wk_all[:, 2 * p:2 * p + 1], wk_all[:, 2 * p + 1:2 * p + 2])
        n_scr[p:p + 1, :] = (jnp.where(nlane_lo, decays[0], decays[1]) * n_row
                             + jnp.sum(kw, axis=0, keepdims=True))
        upd = jnp.dot(kw.T.astype(jnp.bfloat16), v_ref[:, 2 * p * dv:(2 * p + 2) * dv],
                      preferred_element_type=jnp.float32)
        c_scr[p] = jnp.where(crow_lo, decays[0], decays[1]) * cp + jnp.where(blockdiag, upd, 0.0)


def mlstm(zb, zf, bias_row, ybuf, bsz, seq):
    L = MLSTM_L
    nc = seq // L

    def rows(b, c):
        return b * nc + c

    return pl.pallas_call(
        _mlstm_body,
        grid=(bsz, nc),
        in_specs=[pl.BlockSpec((1, LANES), lambda b, c: (0, 0)),
                  pl.BlockSpec((L, B_QK), lambda b, c: (rows(b, c), ZB_MQ // B_QK)),
                  pl.BlockSpec((L, B_QK), lambda b, c: (rows(b, c), ZF_MK // B_QK)),
                  pl.BlockSpec((L, B_V), lambda b, c: (rows(b, c), ZB_MV // B_V)),
                  pl.BlockSpec((L, B_V), lambda b, c: (rows(b, c), ZF_MO // B_V)),
                  pl.BlockSpec((L, LANES), lambda b, c: (rows(b, c), ZF_G // LANES)),
                  pl.BlockSpec(memory_space=pl.ANY)],
        out_specs=[pl.BlockSpec((None, L, B_V), lambda b, c: (1, rows(b, c), 0)),
                   pl.BlockSpec((1, LANES, L), lambda b, c: (b, 0, c))],
        out_shape=[jax.ShapeDtypeStruct(ybuf.shape, ybuf.dtype),
                   jax.ShapeDtypeStruct((bsz, LANES, seq), jnp.float32)],
        input_output_aliases={6: 0},
        scratch_shapes=[pltpu.VMEM((MLSTM_HEADS // 2, 2 * MLSTM_QK_DIM, 2 * MLSTM_V_DIM), jnp.float32),
                        pltpu.VMEM((8, LANES), jnp.float32),
                        pltpu.VMEM((8, LANES), jnp.float32),
                        pltpu.VMEM((8, LANES), jnp.float32)],
        compiler_params=_cparams(("arbitrary", "arbitrary"), 32 << 20),
        name="mlstm",
    )(bias_row, zb, zf, zb, zf, zf, ybuf)


def _fox_body(q_ref, k_ref, v_ref, ck_ref, ybuf_ref, o_ref):
    del ybuf_ref
    hp = pl.program_id(1)
    qi = pl.program_id(2)
    blk = FOX_BLK
    dh = FOX_HEAD_DIM
    c2 = FOX_HEAD_DIM ** -0.5 * LOG2E

    def scores(j, e):
        start = pl.multiple_of(j * blk, blk)
        s = lax.dot_general(q_ref[:, e * dh:(e + 1) * dh], k_ref[pl.ds(start, blk), e * dh:(e + 1) * dh],
                            (((1,), (1,)), ((), ())), preferred_element_type=jnp.float32)
        return s * c2 - ck_ref[0, 2 * hp + e, pl.ds(j, 1), :] * LOG2E

    def update(s, j, e, m, l, acc):
        start = pl.multiple_of(j * blk, blk)
        m_new = jnp.maximum(m, jnp.max(s, axis=-1, keepdims=True))
        alpha = jnp.exp2(m - m_new)
        p = jnp.exp2(s - m_new)
        l = alpha * l + jnp.sum(p, axis=-1, keepdims=True)
        acc = alpha * acc + jnp.dot(p.astype(jnp.bfloat16),
                                    v_ref[pl.ds(start, blk), e * dh:(e + 1) * dh],
                                    preferred_element_type=jnp.float32)
        return m_new, l, acc

    def step(j, carry):
        s = [scores(j, e) for e in range(2)]
        return tuple(update(s[e], j, e, *carry[e]) for e in range(2))

    init1 = (jnp.full((blk, 1), NEG_INF, jnp.float32), jnp.zeros((blk, 1), jnp.float32),
             jnp.zeros((blk, dh), jnp.float32))
    carry = lax.fori_loop(0, qi, step, (init1, init1))
    row = lax.broadcasted_iota(jnp.int32, (blk, blk), 0)
    colm = lax.broadcasted_iota(jnp.int32, (blk, blk), 1)
    s = [jnp.where(colm <= row, scores(qi, e), NEG_INF) for e in range(2)]
    for e in range(2):
        _, l, acc = update(s[e], qi, e, *carry[e])
        o_ref[:, e * dh:(e + 1) * dh] = (acc / l).astype(o_ref.dtype)


def fox_attention(zb, cum_t, ybuf, bsz, seq):
    blk = FOX_BLK
    nq = seq // blk
    pw = 2 * FOX_HEAD_DIM
    ck = cum_t.reshape(bsz, LANES, nq, blk)
    return pl.pallas_call(
        _fox_body,
        grid=(bsz, FOX_HEADS // 2, nq),
        in_specs=[pl.BlockSpec((blk, pw), lambda b, h, i: (b * nq + i, ZB_CQ // pw + h)),
                  pl.BlockSpec((seq, pw), lambda b, h, i: (b, ZB_CK // pw + h)),
                  pl.BlockSpec((seq, pw), lambda b, h, i: (b, ZB_CV // pw + h)),
                  pl.BlockSpec((1, 8, nq, blk), lambda b, h, i: (b, 2, 0, 0)),
                  pl.BlockSpec(memory_space=pl.ANY)],
        out_specs=pl.BlockSpec((None, blk, pw), lambda b, h, i: (2, b * nq + i, h)),
        out_shape=jax.ShapeDtypeStruct(ybuf.shape, ybuf.dtype),
        input_output_aliases={4: 0},
        compiler_params=_cparams(("parallel", "arbitrary", "arbitrary"), 32 << 20),
        name="fox_attention",
    )(zb, zb, zb, ck, ybuf)


RG_COPY, RG_DUP, RG_GATES = 0, 1, 2
_SEG = dict(a_q=0, a_k=1, a_v=2, m_q=3, m_k=4, m_v=5, m_i=6, m_f=7, m_o=8,
            c_q=9, c_k=10, c_v=11, c_f=12, g_a=13, g_b=14, g_c=15)


def _regroup_table():
    tiles = []

    def copy(name):
        off, width = SEG_OFFS[_SEG[name]], SEG_WIDTHS[_SEG[name]]
        assert width % WT == 0 and off % SUBLANES == 0
        tiles.extend((off + k * WT, RG_COPY) for k in range(width // WT))

    copy("m_o"); copy("m_k")
    m_i, m_f, c_f = (SEG_OFFS[_SEG[n]] for n in ("m_i", "m_f", "c_f"))
    assert m_f == m_i + MLSTM_HEADS and m_i % SUBLANES == 0 and c_f % SUBLANES == 0
    tiles.append((m_i, RG_GATES))
    assert len(tiles) * WT == WA_ZB
    for name in ("a_q", "m_v", "c_q", "c_k", "c_v", "m_q"):
        copy(name)
    a_k, a_v = SEG_OFFS[_SEG["a_k"]], SEG_OFFS[_SEG["a_v"]]
    assert a_v == a_k + A_KV and a_k % SUBLANES == 0 and 4 * A_KV == WT
    tiles.append((a_k, RG_DUP))
    assert len(tiles) * WT == WA_N
    return tiles, c_f


def _regroup_body(base_ref, mode_ref, w_ref, aux_ref, o_ref):
    del base_ref
    mode = mode_ref[pl.program_id(0)]
    bf = jnp.bfloat16
    hd = SWA_HEAD_DIM

    @pl.when(mode == RG_COPY)
    def _():
        o_ref[...] = w_ref[0].astype(bf)

    @pl.when(mode == RG_DUP)
    def _():
        for i in range(2 * SWA_KV_HEADS):
            head = w_ref[0, i * hd:(i + 1) * hd, :].astype(bf)
            o_ref[2 * i * hd:(2 * i + 1) * hd, :] = head
            o_ref[(2 * i + 1) * hd:(2 * i + 2) * hd, :] = head

    @pl.when(mode == RG_GATES)
    def _():
        nh = MLSTM_HEADS
        o_ref[0:2 * nh, :] = w_ref[0, 0:2 * nh, :].astype(bf)
        o_ref[2 * nh:3 * nh, :] = aux_ref[0].astype(bf)
        o_ref[3 * nh:WT, :] = jnp.zeros((WT - 3 * nh, o_ref.shape[1]), bf)


def regroup_w_in(wt, l):
    tiles, c_f = _regroup_table()
    base = jnp.asarray([b for b, _ in tiles], jnp.int32)
    mode = jnp.asarray([m for _, m in tiles], jnp.int32)
    k = wt.shape[2]
    el = pl.Element
    in_specs = [pl.BlockSpec((el(1), el(WT), el(k)),
                             lambda t, base, mode: (l, pl.multiple_of(base[t], SUBLANES), 0)),
                pl.BlockSpec((el(1), el(FOX_HEADS), el(k)), lambda t, base, mode: (l, c_f, 0))]
    vmem = 2 * (WT * k * 4 + WT * k * 2) + (12 << 20)
    return pl.pallas_call(
        _regroup_body,
        grid_spec=pltpu.PrefetchScalarGridSpec(
            num_scalar_prefetch=2, grid=(len(tiles),), in_specs=in_specs,
            out_specs=pl.BlockSpec((WT, k), lambda t, base, mode: (t, 0))),
        out_shape=jax.ShapeDtypeStruct((WA_N, k), jnp.bfloat16),
        compiler_params=_cparams(("arbitrary",), vmem),
        name="regroup_w_in",
    )(base, mode, wt, wt)


def _cast_body(w_ref, o_ref):
    o_ref[...] = w_ref[...].astype(o_ref.dtype)


def cast_bf16(w, l=None, *, br=1024, bc=2048):
    r, c = w.shape[-2:]
    br, bc = min(br, r), min(bc, c)
    assert r % br == 0 and c % bc == 0
    if l is None:
        in_spec = pl.BlockSpec((br, bc), lambda i, j: (i, j))
    else:
        in_spec = pl.BlockSpec((None, br, bc), lambda i, j: (l, i, j))
    return pl.pallas_call(
        _cast_body,
        grid=(r // br, c // bc),
        in_specs=[in_spec],
        out_specs=pl.BlockSpec((br, bc), lambda i, j: (i, j)),
        out_shape=jax.ShapeDtypeStruct((r, c), jnp.bfloat16),
        compiler_params=_cparams(("parallel", "parallel"), 2 * br * bc * 6 + (8 << 20)),
        name="cast_bf16",
    )(w)


def kernel(x, w_in, b_mlstm_i, b_mlstm_f, b_fox_f, attn_sinks, w_up_swa, w_up_mlstm, w_up_fox,
           w_o, ln1_g, ln1_b, w_ff1, w_ff2, ln2_g, ln2_b):
    bsz, seq, d = x.shape
    t = bsz * seq
    assert d == D_MODEL and seq % FOX_BLK == 0 and seq % MLSTM_L == 0
    xf = x.reshape(t, d).astype(jnp.float32)
    xb = cast_bf16(xf)
    gate_pad = jnp.zeros((LANES - 3 * MLSTM_HEADS,), jnp.float32)
    ups = (w_up_swa, w_up_mlstm, w_up_fox)
    w_ups = [cast_bf16(w, 0) for w in ups]
    w_ob = cast_bf16(w_o, 0)
    w_int = jnp.swapaxes(w_in, 1, 2)
    for l in range(DEPTH):
        w_all = regroup_w_in(w_int, l)
        bias_row = jnp.concatenate([b_mlstm_i[l], b_mlstm_f[l], b_fox_f[l], gate_pad]).reshape(1, LANES)
        (zf,) = matmul(xb, w_all, n=ZF_N, w_row=WA_ZF, bm=512, bn=ZF_N, out_dtype=jnp.float32,
                       name="in_proj_f32")
        gates = Side(w_int, l, 128, row_off=SEG_OFFS[_SEG["g_a"]], n_rows=N_BRANCH * D_MODEL)
        zb, w_g = matmul(xb, w_all, n=ZB_N, w_row=WA_ZB, bm=1024, bn=512, out_dtype=jnp.bfloat16,
                         name="in_proj_bf16", sides=[gates])
        ybuf = jnp.zeros((3, t, A_Q), jnp.bfloat16)
        ybuf = swa_attention(zb, attn_sinks[l].astype(jnp.float32), ybuf, bsz, seq)
        ybuf, cum_t = mlstm(zb, zf, bias_row.astype(jnp.float32), ybuf, bsz, seq)
        ybuf = fox_attention(zb, cum_t, ybuf, bsz, seq)
        mix, w_1b = gated_merge(xb, w_g, ybuf, w_ups, bm=1024, bn=512, sides=[Side(w_ff1, l, 32)])
        (r1,) = matmul_residual(mix, w_ob, xf, bm=1024, bn=512, bk=D_MODEL, name="out_proj")
        xf, xb = layer_norm(r1, ln1_g[l], ln1_b[l], emit_f32=False)
        hid, w_2b = matmul(xb, w_1b, bm=1024, bn=1024, out_dtype=jnp.bfloat16, act="relu2",
                           name="ff1", sides=[Side(w_ff2, l, 128)])
        nxt = [Side(w, l + 1, 16) for w in (w_o,) + ups] if l + 1 < DEPTH else []
        r2, *cast_next = matmul_residual(hid, w_2b, xf, bm=1024, bn=1024, bk=2048, name="ff2", sides=nxt)
        if cast_next:
            w_ob, *w_ups = cast_next
        xf, xb = layer_norm(r2, ln2_g[l], ln2_b[l], emit_f32=(l + 1 == DEPTH))
    return xf.reshape(bsz, seq, d).astype(x.dtype)
```

```python
import functools
from typing import NamedTuple

import jax
import jax.numpy as jnp
from jax import lax
from jax.experimental import pallas as pl
from jax.experimental.pallas import tpu as pltpu

D_MODEL = 4096
SWA_HEADS, SWA_KV_HEADS, SWA_HEAD_DIM, SWA_WINDOW = 16, 2, 64, 128
MLSTM_HEADS, MLSTM_QK_DIM, MLSTM_V_DIM = 8, 64, 128
FOX_HEADS, FOX_HEAD_DIM = 8, 128
D_FF = 4 * D_MODEL
LN_EPS = 1e-5
DEPTH = 2
DN_ALPHA = (2 * DEPTH) ** 0.25

A_Q = SWA_HEADS * SWA_HEAD_DIM
A_KV = SWA_KV_HEADS * SWA_HEAD_DIM
B_QK = MLSTM_HEADS * MLSTM_QK_DIM
B_V = MLSTM_HEADS * MLSTM_V_DIM
C_W = FOX_HEADS * FOX_HEAD_DIM
SEG_WIDTHS = (A_Q, A_KV, A_KV, B_QK, B_QK, B_V, MLSTM_HEADS, MLSTM_HEADS, B_V,
              C_W, C_W, C_W, FOX_HEADS, D_MODEL, D_MODEL, D_MODEL)
SEG_OFFS = tuple(sum(SEG_WIDTHS[:i]) for i in range(len(SEG_WIDTHS)))

LANES = 128
SUBLANES = 8
VMEM_LIMIT_CAP = 56 * 1024 * 1024

ZB_AQ = 0
ZB_MV = 1024
ZB_CQ = 2048
ZB_CK = 3072
ZB_CV = 4096
ZB_MQ = 5120
ZB_AK = 5632
ZB_AV = 5888
ZB_N = 6144
ZF_MO = 0
ZF_MK = 1024
ZF_G = 1536
ZF_N = 1664
WT = 512
WA_ZF = 0
WA_ZB = 2048
WA_N = WA_ZB + ZB_N

MLSTM_L = 256
FOX_BLK = 512
SWA_BLK = 128
NEG_INF = float("-inf")
LOG2E = 1.4426950408889634


def _cparams(sem, vmem_bytes):
    return pltpu.CompilerParams(dimension_semantics=sem,
                                vmem_limit_bytes=int(min(vmem_bytes, VMEM_LIMIT_CAP)))


_NT = (((1,), (1,)), ((), ()))


class Side(NamedTuple):
    src: jax.Array
    layer: int
    rows: int
    row_off: int = 0
    n_rows: int = 0


def _side_plumbing(sides, grid):
    steps = 1
    for n in grid:
        steps *= n

    def lin(*g):
        s = g[0]
        for a, n in zip(g[1:], grid[1:]):
            s = s * n + a
        return s

    in_specs, out_specs, out_shapes, vmem = [], [], [], 0
    for sd in sides:
        c = sd.src.shape[-1]
        r = sd.n_rows or sd.src.shape[-2]
        rows = sd.rows
        while r // rows > steps:
            rows *= 2
        assert r % rows == 0 and sd.row_off % SUBLANES == 0
        last = r // rows - 1

        def blk(*g, last=last):
            return jnp.minimum(lin(*g), last)

        if sd.row_off or sd.n_rows:
            el = pl.Element
            in_specs.append(pl.BlockSpec(
                (el(1), el(rows), el(c)),
                lambda *g, blk=blk, sd=sd, rows=rows: (
                    sd.layer, pl.multiple_of(sd.row_off + blk(*g) * rows, SUBLANES), 0)))
        else:
            in_specs.append(pl.BlockSpec((None, rows, c),
                                         lambda *g, blk=blk, l=sd.layer: (l, blk(*g), 0)))
        out_specs.append(pl.BlockSpec((rows, c), lambda *g, blk=blk: (blk(*g), 0)))
        out_shapes.append(jax.ShapeDtypeStruct((r, c), jnp.bfloat16))
        vmem += 2 * rows * c * (4 + 2)
    return in_specs, out_specs, out_shapes, vmem


def _cast_sides(side_in, side_out):
    for si, so in zip(side_in, side_out):
        so[...] = si[...].reshape(so.shape).astype(so.dtype)


def _mm_body(x_ref, w_ref, *rest, act, w_t, n_side):
    side_in, o_ref, side_out = rest[:n_side], rest[n_side], rest[n_side + 1:]
    _cast_sides(side_in, side_out)
    if w_t:
        acc = lax.dot_general(x_ref[...], w_ref[...], _NT, preferred_element_type=jnp.float32)
    else:
        acc = jnp.dot(x_ref[...], w_ref[...], preferred_element_type=jnp.float32)
    if act == "relu2":
        acc = jnp.square(jnp.maximum(acc, 0.0))
    o_ref[...] = acc.astype(o_ref.dtype)


def _residual(res_refs):
    if len(res_refs) == 1:
        return res_refs[0][...]
    return _ln_apply(*(ref[...] for ref in res_refs))


def _mm_res_body(x_ref, w_ref, *rest):
    res_refs, o_ref = rest[:-1], rest[-1]
    acc = jnp.dot(x_ref[...], w_ref[...], preferred_element_type=jnp.float32)
    o_ref[...] = DN_ALPHA * _residual(res_refs) + acc


def _mm_res_k_body(x_ref, w_ref, *rest, n_res, n_side):
    res_refs, rest = rest[:n_res], rest[n_res:]
    side_in, o_ref, side_out = rest[:n_side], rest[n_side], rest[n_side + 1:]
    k = pl.program_id(2)

    @pl.when(k == 0)
    def _():
        _cast_sides(side_in, side_out)
        o_ref[...] = DN_ALPHA * _residual(res_refs) + jnp.dot(x_ref[...], w_ref[...],
                                                              preferred_element_type=jnp.float32)

    @pl.when(k > 0)
    def _():
        _cast_sides(side_in, side_out)
        o_ref[...] += jnp.dot(x_ref[...], w_ref[...], preferred_element_type=jnp.float32)


def matmul(x, w, *, bm, bn, out_dtype, act=None, name, n=None, w_row=None, sides=()):
    m, k = x.shape
    w_t = w_row is not None
    n = w.shape[1] if n is None else n
    bm, bn = min(bm, m), min(bn, n)
    assert m % bm == 0 and n % bn == 0
    osz = jnp.dtype(out_dtype).itemsize
    grid = (m // bm, n // bn)
    s_in, s_out, s_shapes, s_vmem = _side_plumbing(sides, grid)
    vmem = 2 * (bm * k * 2 + k * bn * 2 + bm * bn * osz) + bm * bn * 4 + (4 << 20) + s_vmem
    if w_t:
        assert w_row % bn == 0
        off = w_row // bn
        w_spec = pl.BlockSpec((bn, k), lambda i, j: (off + j, 0))
    else:
        w_spec = pl.BlockSpec((k, bn), lambda i, j: (0, j))
    return pl.pallas_call(
        functools.partial(_mm_body, act=act, w_t=w_t, n_side=len(sides)),
        grid=grid,
        in_specs=[pl.BlockSpec((bm, k), lambda i, j: (i, 0)), w_spec] + s_in,
        out_specs=[pl.BlockSpec((bm, bn), lambda i, j: (i, j))] + s_out,
        out_shape=[jax.ShapeDtypeStruct((m, n), out_dtype)] + s_shapes,
        compiler_params=_cparams(("arbitrary", "arbitrary"), vmem),
        name=name,
    )(x, w, *[sd.src for sd in sides])


def matmul_residual(x, w, res, *, bm, bn, bk, name, sides=()):
    m, k = x.shape
    n = w.shape[1]
    bm, bn, bk = min(bm, m), min(bn, n), min(bk, k)
    assert m % bm == 0 and n % bn == 0 and k % bk == 0
    res_args = tuple(res) if isinstance(res, LnRes) else (res,)
    res_dims = [(bm, bn, True, True), (bm, 1, True, False), (bm, 1, True, False),
                (1, bn, False, True), (1, bn, False, True)][:len(res_args)]

    def res_specs(ij):
        return [pl.BlockSpec((r, c), lambda *g, ri=ri, ci=ci: (ij(*g)[0] if ri else 0, ij(*g)[1] if ci else 0))
                for r, c, ri, ci in res_dims]

    if bk == k:
        assert not sides
        vmem = 2 * (bm * k * 2 + k * bn * 2 + 2 * bm * bn * 4) + bm * bn * 4 + (4 << 20)
        return [pl.pallas_call(
            _mm_res_body,
            grid=(m // bm, n // bn),
            in_specs=[pl.BlockSpec((bm, k), lambda i, j: (i, 0)),
                      pl.BlockSpec((k, bn), lambda i, j: (0, j))] + res_specs(lambda i, j: (i, j)),
            out_specs=pl.BlockSpec((bm, bn), lambda i, j: (i, j)),
            out_shape=jax.ShapeDtypeStruct((m, n), jnp.float32),
            compiler_params=_cparams(("parallel", "arbitrary"), vmem),
            name=name,
        )(x, w, *res_args)]
    grid = (m // bm, n // bn, k // bk)
    s_in, s_out, s_shapes, s_vmem = _side_plumbing(sides, grid)
    vmem = 2 * (bm * bk * 2 + bk * bn * 2 + 2 * bm * bn * 4) + 2 * bm * bn * 4 + (4 << 20) + s_vmem
    return pl.pallas_call(
        functools.partial(_mm_res_k_body, n_res=len(res_args), n_side=len(sides)),
        grid=grid,
        in_specs=[pl.BlockSpec((bm, bk), lambda i, j, kk: (i, kk)),
                  pl.BlockSpec((bk, bn), lambda i, j, kk: (kk, j))]
        + res_specs(lambda i, j, kk: (i, j)) + s_in,
        out_specs=[pl.BlockSpec((bm, bn), lambda i, j, kk: (i, j))] + s_out,
        out_shape=[jax.ShapeDtypeStruct((m, n), jnp.float32)] + s_shapes,
        compiler_params=_cparams(("arbitrary", "arbitrary", "arbitrary"), vmem),
        name=name,
    )(x, w, *res_args, *[sd.src for sd in sides])


N_BRANCH = 3


def _merge_body(x_ref, wg0_ref, wg1_ref, wg2_ref, y_ref, wu0_ref, wu1_ref, wu2_ref, *rest, n_side):
    side_in, o_ref, side_out = rest[:n_side], rest[n_side], rest[n_side + 1:]
    _cast_sides(side_in, side_out)
    mix = None
    for br, (wg_ref, wu_ref) in enumerate(((wg0_ref, wu0_ref), (wg1_ref, wu1_ref), (wg2_ref, wu2_ref))):
        g = lax.dot_general(x_ref[...], wg_ref[...], _NT, preferred_element_type=jnp.float32)
        u = jnp.dot(y_ref[br], wu_ref[...], preferred_element_type=jnp.float32)
        c = u * (1.0 / (1.0 + jnp.exp(-g)))
        mix = c if mix is None else mix + c
    o_ref[...] = mix.astype(o_ref.dtype)


def gated_merge(xb, wg, y, wus, *, bm, bn, sides=()):
    t, d = xb.shape
    nbr, _, kin = y.shape
    assert nbr == N_BRANCH == len(wus) and wg.shape == (nbr * d, d)
    bm, bn = min(bm, t), min(bn, d)
    assert d % bn == 0
    gstride = d // bn
    grid = (t // bm, d // bn)
    s_in, s_out, s_shapes, s_vmem = _side_plumbing(sides, grid)
    vmem = bm * d * 2 + 2 * (nbr * (d * bn * 2 + bm * kin * 2 + kin * bn * 2) + bm * bn * 2) \
        + 2 * nbr * bm * bn * 4 + (8 << 20) + s_vmem
    wg_specs = [pl.BlockSpec((bn, d), lambda i, j, b=b: (b * gstride + j, 0)) for b in range(nbr)]
    wu_spec = pl.BlockSpec((kin, bn), lambda i, j: (0, j))
    return pl.pallas_call(
        functools.partial(_merge_body, n_side=len(sides)),
        grid=grid,
        in_specs=[pl.BlockSpec((bm, d), lambda i, j: (i, 0), pipeline_mode=pl.Buffered(1))] + wg_specs
        + [pl.BlockSpec((nbr, bm, kin), lambda i, j: (0, i, 0)), wu_spec, wu_spec, wu_spec] + s_in,
        out_specs=[pl.BlockSpec((bm, bn), lambda i, j: (i, j))] + s_out,
        out_shape=[jax.ShapeDtypeStruct((t, d), jnp.bfloat16)] + s_shapes,
        compiler_params=_cparams(("arbitrary", "arbitrary"), vmem),
        name="gated_merge",
    )(xb, wg, wg, wg, y, *wus, *[sd.src for sd in sides])


class LnRes(NamedTuple):
    r: jax.Array
    mu: jax.Array
    rstd: jax.Array
    g: jax.Array
    b: jax.Array


def _ln_apply(r, mu, rstd, g, b):
    return (r - mu) * rstd * g + b


def _ln_body(r_ref, g_ref, b_ref, *outs, emit_f32):
    r = r_ref[...]
    mu = jnp.mean(r, axis=-1, keepdims=True)
    xc = r - mu
    rstd = lax.rsqrt(jnp.mean(xc * xc, axis=-1, keepdims=True) + LN_EPS)
    y = _ln_apply(r, mu, rstd, g_ref[...], b_ref[...])
    if emit_f32:
        of_ref, ob_ref = outs
        of_ref[...] = y
    else:
        ob_ref, mu_ref, rstd_ref = outs
        mu_ref[...] = mu
        rstd_ref[...] = rstd
    ob_ref[...] = y.astype(jnp.bfloat16)


def layer_norm(r, g, b, *, emit_f32, bm=256):
    t, d = r.shape
    bm = min(bm, t)
    g2, b2 = g.reshape(1, d), b.reshape(1, d)
    row = pl.BlockSpec((bm, d), lambda i: (i, 0))
    stat = pl.BlockSpec((bm, 1), lambda i: (i, 0))
    vmem = 2 * (bm * d * 4 * 2 + bm * d * 2) + 4 * bm * d * 4 + (4 << 20)
    if emit_f32:
        out_specs = [row, row]
        out_shape = [jax.ShapeDtypeStruct((t, d), jnp.float32), jax.ShapeDtypeStruct((t, d), jnp.bfloat16)]
    else:
        out_specs = [row, stat, stat]
        out_shape = [jax.ShapeDtypeStruct((t, d), jnp.bfloat16),
                     jax.ShapeDtypeStruct((t, 1), jnp.float32), jax.ShapeDtypeStruct((t, 1), jnp.float32)]
    outs = pl.pallas_call(
        functools.partial(_ln_body, emit_f32=emit_f32),
        grid=(t // bm,),
        in_specs=[row, pl.BlockSpec((1, d), lambda i: (0, 0)), pl.BlockSpec((1, d), lambda i: (0, 0))],
        out_specs=out_specs,
        out_shape=out_shape,
        compiler_params=_cparams(("parallel",), vmem),
        name="layer_norm",
    )(r, g2, b2)
    if emit_f32:
        return outs[0], outs[1]
    xb, mu, rstd = outs
    return LnRes(r, mu, rstd, g2, b2), xb


def _swa_body(sink_ref, bias_ref, q_ref, kp_ref, kc_ref, vp_ref, vc_ref, ybuf_ref, o_ref):
    del ybuf_ref
    blk = SWA_BLK
    c2 = SWA_HEAD_DIM ** -0.5 * LOG2E
    lo = lax.broadcasted_iota(jnp.int32, (2 * blk, LANES), 1) < SWA_HEAD_DIM
    group = SWA_HEADS // SWA_KV_HEADS
    pairs = SWA_HEADS // 2
    kz, vz = [], []
    for g in range(SWA_KV_HEADS):
        kd = jnp.concatenate([kp_ref[:, g * LANES:(g + 1) * LANES],
                              kc_ref[:, g * LANES:(g + 1) * LANES]], axis=0)
        vd = jnp.concatenate([vp_ref[:, g * LANES:(g + 1) * LANES],
                              vc_ref[:, g * LANES:(g + 1) * LANES]], axis=0)
        zero = jnp.zeros_like(kd)
        kz.append(jnp.concatenate([jnp.where(lo, kd, zero), jnp.where(lo, zero, kd)], axis=0))
        vz.append(jnp.concatenate([jnp.where(lo, vd, zero), jnp.where(lo, zero, vd)], axis=0))
    s2 = [lax.dot_general(q_ref[:, p * LANES:(p + 1) * LANES], kz[p // (group // 2)],
                          (((1,), (1,)), ((), ())), preferred_element_type=jnp.float32)
          for p in range(pairs)]
    for p in range(pairs):
        probs = []
        for e in range(2):
            h = 2 * p + e
            sink = sink_ref[h] * LOG2E
            s = s2[p][:, e * 2 * blk:(e + 1) * 2 * blk] * c2 + bias_ref[h]
            m = jnp.maximum(jnp.max(s, axis=-1, keepdims=True), sink)
            pe = jnp.exp2(s - m)
            den = jnp.sum(pe, axis=-1, keepdims=True) + jnp.exp2(sink - m)
            probs.append((pe * (1.0 / den)).astype(jnp.bfloat16))
        p2 = jnp.concatenate(probs, axis=1)
        o_ref[:, p * LANES:(p + 1) * LANES] = jnp.dot(
            p2, vz[p // (group // 2)], preferred_element_type=jnp.float32).astype(o_ref.dtype)


def _swa_bias_table():
    blk = SWA_BLK
    qi = lax.broadcasted_iota(jnp.int32, (blk, 2 * blk), 0)
    kj = lax.broadcasted_iota(jnp.int32, (blk, 2 * blk), 1)
    dist = qi + blk - kj
    window = (dist >= 0) & (dist < SWA_WINDOW)
    slopes = 2.0 ** (-8.0 * jnp.arange(1, SWA_HEADS + 1, dtype=jnp.float32) / SWA_HEADS)
    bias = -(LOG2E * slopes)[:, None, None] * dist.astype(jnp.float32)[None]
    later = jnp.where(window[None], bias, NEG_INF)
    first = jnp.where((window & (kj >= blk))[None], bias, NEG_INF)
    return jnp.stack([first, later])


def swa_attention(zb, sinks, ybuf, bsz, seq):
    blk = SWA_BLK
    nb = seq // blk
    kcol, vcol = ZB_AK // (2 * LANES), ZB_AV // (2 * LANES)

    def cur(b, n):
        return b * nb + n

    def prev(b, n):
        return b * nb + jnp.maximum(n - 1, 0)

    return pl.pallas_call(
        _swa_body,
        grid=(bsz, nb),
        in_specs=[pl.BlockSpec(memory_space=pltpu.SMEM),
                  pl.BlockSpec((None, SWA_HEADS, blk, 2 * blk),
                               lambda b, n: (jnp.minimum(n, 1), 0, 0, 0)),
                  pl.BlockSpec((blk, A_Q), lambda b, n: (cur(b, n), ZB_AQ // A_Q)),
                  pl.BlockSpec((blk, 2 * LANES), lambda b, n: (prev(b, n), kcol)),
                  pl.BlockSpec((blk, 2 * LANES), lambda b, n: (cur(b, n), kcol)),
                  pl.BlockSpec((blk, 2 * LANES), lambda b, n: (prev(b, n), vcol)),
                  pl.BlockSpec((blk, 2 * LANES), lambda b, n: (cur(b, n), vcol)),
                  pl.BlockSpec(memory_space=pl.ANY)],
        out_specs=pl.BlockSpec((None, blk, A_Q), lambda b, n: (0, cur(b, n), 0)),
        out_shape=jax.ShapeDtypeStruct(ybuf.shape, ybuf.dtype),
        input_output_aliases={7: 0},
        compiler_params=_cparams(("parallel", "arbitrary"), 32 << 20),
        name="swa_attention",
    )(sinks, _swa_bias_table(), zb, zb, zb, zb, zb, ybuf)


def _mlstm_body(bias_ref, q_ref, k_ref, v_ref, og_ref, g_ref, ybuf_ref, y_ref, cum_ref,
                c_scr, n_scr, m_scr, carry_scr):
    del ybuf_ref
    c = pl.program_id(1)
    L = MLSTM_L
    dk, dv = MLSTM_QK_DIM, MLSTM_V_DIM

    @pl.when(c == 0)
    def _():
        c_scr[...] = jnp.zeros_like(c_scr)
        n_scr[...] = jnp.zeros_like(n_scr)
        m_scr[...] = jnp.zeros_like(m_scr)
        carry_scr[...] = jnp.zeros_like(carry_scr)

    nh = MLSTM_HEADS
    a = g_ref[...] + bias_ref[...]
    lf = jnp.minimum(a, 0.0) - jnp.log1p(jnp.exp(-jnp.abs(a)))
    row = lax.broadcasted_iota(jnp.int32, (L, L), 0)
    col = lax.broadcasted_iota(jnp.int32, (L, L), 1)
    causal = row >= col
    tri = causal.astype(jnp.float32)
    b_all = jnp.dot(tri, lf, preferred_element_type=jnp.float32,
                    precision=lax.Precision.HIGHEST)
    cum = b_all + carry_scr[0:1, :]
    carry_scr[0:1, :] = cum[L - 1:L, :]
    cum_ref[0] = cum.T

    b2 = pltpu.roll(b_all, LANES - nh, axis=1) * LOG2E
    g2 = a * LOG2E - b2
    rowi = lax.broadcasted_iota(jnp.int32, (L, LANES), 0)
    cm2 = g2
    d = 1
    while d < L:
        cm2 = jnp.maximum(cm2, jnp.where(rowi >= d, pltpu.roll(cm2, d, axis=0), NEG_INF))
        d *= 2
    m2_prev = m_scr[0:1, :]
    u2 = jnp.maximum(m2_prev, cm2)
    u2_last = u2[L - 1:L, :]
    w_inter_all = jnp.exp2(m2_prev - u2)
    floor_all = jnp.exp2(-(b2 + u2))
    wk_all = jnp.exp2(g2 - u2_last)
    decay_row = jnp.exp2(m2_prev - u2_last)
    m_scr[0:1, :] = b2[L - 1:L, :] + u2_last
    g2_t = g2.T

    lane = lax.broadcasted_iota(jnp.int32, (L, LANES), 1)
    lo = lane < dk
    crow = lax.broadcasted_iota(jnp.int32, (2 * dk, 2 * dv), 0)
    ccol = lax.broadcasted_iota(jnp.int32, (2 * dk, 2 * dv), 1)
    crow_lo = crow < dk
    blockdiag = crow_lo == (ccol < dv)
    nlane_lo = lax.broadcasted_iota(jnp.int32, (1, LANES), 1) < dk

    qps, kps, scs, cps, qcs = [], [], [], [], []
    for p in range(MLSTM_HEADS // 2):
        qp = q_ref[:, p * LANES:(p + 1) * LANES]
        kp = k_ref[:, p * LANES:(p + 1) * LANES] * (dk ** -0.5)
        kpb = kp.astype(jnp.bfloat16)
        zero = jnp.zeros_like(kpb)
        kz = jnp.concatenate([jnp.where(lo, kpb, zero), jnp.where(lo, zero, kpb)], axis=0)
        scs.append(lax.dot_general(qp, kz, (((1,), (1,)), ((), ())),
                                   preferred_element_type=jnp.float32))
        cp = c_scr[p]
        qcs.append(jnp.dot(qp, cp.astype(jnp.bfloat16), preferred_element_type=jnp.float32))
        qps.append(qp); kps.append(kp); cps.append(cp)

    for p in range(MLSTM_HEADS // 2):
        qp, kp, sc, cp, qc = qps[p], kps[p], scs[p], cps[p], qcs[p]
        n_row = n_scr[p:p + 1, :]
        qn_prod = qp.astype(jnp.float32) * n_row
        for e in range(2):
            h = 2 * p + e
            decay_mat = jnp.exp2(jnp.where(causal, g2_t[h:h + 1, :] - u2[:, h:h + 1], NEG_INF))
            smat = sc[:, e * L:(e + 1) * L] * decay_mat
            w_inter = w_inter_all[:, h:h + 1]
            qn = jnp.sum(jnp.where(lo == (e == 0), qn_prod, 0.0), axis=-1, keepdims=True)
            num = w_inter * qc[:, e * dv:(e + 1) * dv] + jnp.dot(
                smat.astype(jnp.bfloat16), v_ref[:, h * dv:(h + 1) * dv],
                preferred_element_type=jnp.float32)
            den = w_inter * qn + jnp.sum(smat, axis=-1, keepdims=True)
            rden = 1.0 / jnp.maximum(jnp.abs(den), floor_all[:, h:h + 1])
            og = og_ref[:, h * dv:(h + 1) * dv]
            y_ref[:, h * dv:(h + 1) * dv] = (num * rden * (1.0 / (1.0 + jnp.exp(-og)))).astype(y_ref.dtype)
        decays = [decay_row[:, 2 * p + e:2 * p + e + 1] for e in range(2)]
        kw = kp * jnp.where(lo, wk_all[:, 2 * p:2 * p + 1], wk_all[:, 2 * p + 1:2 * p + 2])
        n_scr[p:p + 1, :] = (jnp.where(nlane_lo, decays[0], decays[1]) * n_row
                             + jnp.sum(kw, axis=0, keepdims=True))
        upd = jnp.dot(kw.T.astype(jnp.bfloat16), v_ref[:, 2 * p * dv:(2 * p + 2) * dv],
                      preferred_element_type=jnp.float32)
        c_scr[p] = jnp.where(crow_lo, decays[0], decays[1]) * cp + jnp.where(blockdiag, upd, 0.0)


def mlstm(zb, zf, bias_row, ybuf, bsz, seq):
    L = MLSTM_L
    nc = seq // L

    def rows(b, c):
        return b * nc + c

    return pl.pallas_call(
        _mlstm_body,
        grid=(bsz, nc),
        in_specs=[pl.BlockSpec((1, LANES), lambda b, c: (0, 0)),
                  pl.BlockSpec((L, B_QK), lambda b, c: (rows(b, c), ZB_MQ // B_QK)),
                  pl.BlockSpec((L, B_QK), lambda b, c: (rows(b, c), ZF_MK // B_QK)),
                  pl.BlockSpec((L, B_V), lambda b, c: (rows(b, c), ZB_MV // B_V)),
                  pl.BlockSpec((L, B_V), lambda b, c: (rows(b, c), ZF_MO // B_V)),
                  pl.BlockSpec((L, LANES), lambda b, c: (rows(b, c), ZF_G // LANES)),
                  pl.BlockSpec(memory_space=pl.ANY)],
        out_specs=[pl.BlockSpec((None, L, B_V), lambda b, c: (1, rows(b, c), 0)),
                   pl.BlockSpec((1, LANES, L), lambda b, c: (b, 0, c))],
        out_shape=[jax.ShapeDtypeStruct(ybuf.shape, ybuf.dtype),
                   jax.ShapeDtypeStruct((bsz, LANES, seq), jnp.float32)],
        input_output_aliases={6: 0},
        scratch_shapes=[pltpu.VMEM((MLSTM_HEADS // 2, 2 * MLSTM_QK_DIM, 2 * MLSTM_V_DIM), jnp.float32),
                        pltpu.VMEM((8, LANES), jnp.float32),
                        pltpu.VMEM((8, LANES), jnp.float32),
                        pltpu.VMEM((8, LANES), jnp.float32)],
        compiler_params=_cparams(("arbitrary", "arbitrary"), 32 << 20),
        name="mlstm",
    )(bias_row, zb, zf, zb, zf, zf, ybuf)


def _fox_body(q_ref, k_ref, v_ref, ck_ref, ybuf_ref, o_ref):
    del ybuf_ref
    hp = pl.program_id(1)
    qi = pl.program_id(2)
    blk = FOX_BLK
    dh = FOX_HEAD_DIM
    c2 = FOX_HEAD_DIM ** -0.5 * LOG2E

    def scores(j, e):
        start = pl.multiple_of(j * blk, blk)
        s = lax.dot_general(q_ref[:, e * dh:(e + 1) * dh], k_ref[pl.ds(start, blk), e * dh:(e + 1) * dh],
                            (((1,), (1,)), ((), ())), preferred_element_type=jnp.float32)
        return s * c2 - ck_ref[0, 2 * hp + e, pl.ds(j, 1), :] * LOG2E

    def update(s, j, e, m, l, acc):
        start = pl.multiple_of(j * blk, blk)
        m_new = jnp.maximum(m, jnp.max(s, axis=-1, keepdims=True))
        alpha = jnp.exp2(m - m_new)
        p = jnp.exp2(s - m_new)
        l = alpha * l + jnp.sum(p, axis=-1, keepdims=True)
        acc = alpha * acc + jnp.dot(p.astype(jnp.bfloat16),
                                    v_ref[pl.ds(start, blk), e * dh:(e + 1) * dh],
                                    preferred_element_type=jnp.float32)
        return m_new, l, acc

    def step(j, carry):
        s = [scores(j, e) for e in range(2)]
        return tuple(update(s[e], j, e, *carry[e]) for e in range(2))

    init1 = (jnp.full((blk, 1), NEG_INF, jnp.float32), jnp.zeros((blk, 1), jnp.float32),
             jnp.zeros((blk, dh), jnp.float32))
    carry = lax.fori_loop(0, qi, step, (init1, init1))
    row = lax.broadcasted_iota(jnp.int32, (blk, blk), 0)
    colm = lax.broadcasted_iota(jnp.int32, (blk, blk), 1)
    s = [jnp.where(colm <= row, scores(qi, e), NEG_INF) for e in range(2)]
    for e in range(2):
        _, l, acc = update(s[e], qi, e, *carry[e])
        o_ref[:, e * dh:(e + 1) * dh] = (acc / l).astype(o_ref.dtype)


def fox_attention(zb, cum_t, ybuf, bsz, seq):
    blk = FOX_BLK
    nq = seq // blk
    pw = 2 * FOX_HEAD_DIM
    ck = cum_t.reshape(bsz, LANES, nq, blk)
    return pl.pallas_call(
        _fox_body,
        grid=(bsz, FOX_HEADS // 2, nq),
        in_specs=[pl.BlockSpec((blk, pw), lambda b, h, i: (b * nq + i, ZB_CQ // pw + h)),
                  pl.BlockSpec((seq, pw), lambda b, h, i: (b, ZB_CK // pw + h)),
                  pl.BlockSpec((seq, pw), lambda b, h, i: (b, ZB_CV // pw + h)),
                  pl.BlockSpec((1, 8, nq, blk), lambda b, h, i: (b, 2, 0, 0)),
                  pl.BlockSpec(memory_space=pl.ANY)],
        out_specs=pl.BlockSpec((None, blk, pw), lambda b, h, i: (2, b * nq + i, h)),
        out_shape=jax.ShapeDtypeStruct(ybuf.shape, ybuf.dtype),
        input_output_aliases={4: 0},
        compiler_params=_cparams(("parallel", "arbitrary", "arbitrary"), 32 << 20),
        name="fox_attention",
    )(zb, zb, zb, ck, ybuf)


RG_COPY, RG_DUP, RG_GATES = 0, 1, 2
_SEG = dict(a_q=0, a_k=1, a_v=2, m_q=3, m_k=4, m_v=5, m_i=6, m_f=7, m_o=8,
            c_q=9, c_k=10, c_v=11, c_f=12, g_a=13, g_b=14, g_c=15)


def _regroup_table():
    tiles = []

    def copy(name):
        off, width = SEG_OFFS[_SEG[name]], SEG_WIDTHS[_SEG[name]]
        assert width % WT == 0 and off % SUBLANES == 0
        tiles.extend((off + k * WT, RG_COPY) for k in range(width // WT))

    copy("m_o"); copy("m_k")
    m_i, m_f, c_f = (SEG_OFFS[_SEG[n]] for n in ("m_i", "m_f", "c_f"))
    assert m_f == m_i + MLSTM_HEADS and m_i % SUBLANES == 0 and c_f % SUBLANES == 0
    tiles.append((m_i, RG_GATES))
    assert len(tiles) * WT == WA_ZB
    for name in ("a_q", "m_v", "c_q", "c_k", "c_v", "m_q"):
        copy(name)
    a_k, a_v = SEG_OFFS[_SEG["a_k"]], SEG_OFFS[_SEG["a_v"]]
    assert a_v == a_k + A_KV and a_k % SUBLANES == 0 and 4 * A_KV == WT
    tiles.append((a_k, RG_DUP))
    assert len(tiles) * WT == WA_N
    return tiles, c_f


def _regroup_body(base_ref, mode_ref, w_ref, aux_ref, o_ref):
    del base_ref
    mode = mode_ref[pl.program_id(0)]
    bf = jnp.bfloat16
    hd = SWA_HEAD_DIM

    @pl.when(mode == RG_COPY)
    def _():
        o_ref[...] = w_ref[0].astype(bf)

    @pl.when(mode == RG_DUP)
    def _():
        for i in range(2 * SWA_KV_HEADS):
            head = w_ref[0, i * hd:(i + 1) * hd, :].astype(bf)
            o_ref[2 * i * hd:(2 * i + 1) * hd, :] = head
            o_ref[(2 * i + 1) * hd:(2 * i + 2) * hd, :] = head

    @pl.when(mode == RG_GATES)
    def _():
        nh = MLSTM_HEADS
        o_ref[0:2 * nh, :] = w_ref[0, 0:2 * nh, :].astype(bf)
        o_ref[2 * nh:3 * nh, :] = aux_ref[0].astype(bf)
        o_ref[3 * nh:WT, :] = jnp.zeros((WT - 3 * nh, o_ref.shape[1]), bf)


def regroup_w_in(wt, l):
    tiles, c_f = _regroup_table()
    base = jnp.asarray([b for b, _ in tiles], jnp.int32)
    mode = jnp.asarray([m for _, m in tiles], jnp.int32)
    k = wt.shape[2]
    el = pl.Element
    in_specs = [pl.BlockSpec((el(1), el(WT), el(k)),
                             lambda t, base, mode: (l, pl.multiple_of(base[t], SUBLANES), 0)),
                pl.BlockSpec((el(1), el(FOX_HEADS), el(k)), lambda t, base, mode: (l, c_f, 0))]
    vmem = 2 * (WT * k * 4 + WT * k * 2) + (12 << 20)
    return pl.pallas_call(
        _regroup_body,
        grid_spec=pltpu.PrefetchScalarGridSpec(
            num_scalar_prefetch=2, grid=(len(tiles),), in_specs=in_specs,
            out_specs=pl.BlockSpec((WT, k), lambda t, base, mode: (t, 0))),
        out_shape=jax.ShapeDtypeStruct((WA_N, k), jnp.bfloat16),
        compiler_params=_cparams(("arbitrary",), vmem),
        name="regroup_w_in",
    )(base, mode, wt, wt)


def _cast_body(w_ref, o_ref):
    o_ref[...] = w_ref[...].astype(o_ref.dtype)


def cast_bf16(w, l=None, *, br=1024, bc=2048):
    r, c = w.shape[-2:]
    br, bc = min(br, r), min(bc, c)
    assert r % br == 0 and c % bc == 0
    if l is None:
        in_spec = pl.BlockSpec((br, bc), lambda i, j: (i, j))
    else:
        in_spec = pl.BlockSpec((None, br, bc), lambda i, j: (l, i, j))
    return pl.pallas_call(
        _cast_body,
        grid=(r // br, c // bc),
        in_specs=[in_spec],
        out_specs=pl.BlockSpec((br, bc), lambda i, j: (i, j)),
        out_shape=jax.ShapeDtypeStruct((r, c), jnp.bfloat16),
        compiler_params=_cparams(("parallel", "parallel"), 2 * br * bc * 6 + (8 << 20)),
        name="cast_bf16",
    )(w)


def kernel(x, w_in, b_mlstm_i, b_mlstm_f, b_fox_f, attn_sinks, w_up_swa, w_up_mlstm, w_up_fox,
           w_o, ln1_g, ln1_b, w_ff1, w_ff2, ln2_g, ln2_b):
    bsz, seq, d = x.shape
    t = bsz * seq
    assert d == D_MODEL and seq % FOX_BLK == 0 and seq % MLSTM_L == 0
    xf = x.reshape(t, d).astype(jnp.float32)
    xb = cast_bf16(xf)
    gate_pad = jnp.zeros((LANES - 3 * MLSTM_HEADS,), jnp.float32)
    ups = (w_up_swa, w_up_mlstm, w_up_fox)
    w_ups = [cast_bf16(w, 0) for w in ups]
    w_ob = cast_bf16(w_o, 0)
    w_int = jnp.swapaxes(w_in, 1, 2)
    for l in range(DEPTH):
        w_all = regroup_w_in(w_int, l)
        bias_row = jnp.concatenate([b_mlstm_i[l], b_mlstm_f[l], b_fox_f[l], gate_pad]).reshape(1, LANES)
        (zf,) = matmul(xb, w_all, n=ZF_N, w_row=WA_ZF, bm=512, bn=ZF_N, out_dtype=jnp.float32,
                       name="in_proj_f32")
        gates = Side(w_int, l, 128, row_off=SEG_OFFS[_SEG["g_a"]], n_rows=N_BRANCH * D_MODEL)
        zb, w_g = matmul(xb, w_all, n=ZB_N, w_row=WA_ZB, bm=1024, bn=512, out_dtype=jnp.bfloat16,
                         name="in_proj_bf16", sides=[gates])
        ybuf = jnp.zeros((3, t, A_Q), jnp.bfloat16)
        ybuf = swa_attention(zb, attn_sinks[l].astype(jnp.float32), ybuf, bsz, seq)
        ybuf, cum_t = mlstm(zb, zf, bias_row.astype(jnp.float32), ybuf, bsz, seq)
        ybuf = fox_attention(zb, cum_t, ybuf, bsz, seq)
        mix, w_1b = gated_merge(xb, w_g, ybuf, w_ups, bm=1024, bn=256, sides=[Side(w_ff1, l, 32)])
        (r1,) = matmul_residual(mix, w_ob, xf, bm=1024, bn=512, bk=D_MODEL, name="out_proj")
        xf, xb = layer_norm(r1, ln1_g[l], ln1_b[l], emit_f32=False)
        hid, w_2b = matmul(xb, w_1b, bm=1024, bn=1024, out_dtype=jnp.bfloat16, act="relu2",
                           name="ff1", sides=[Side(w_ff2, l, 128)])
        nxt = [Side(w, l + 1, 16) for w in (w_o,) + ups] if l + 1 < DEPTH else []
        r2, *cast_next = matmul_residual(hid, w_2b, xf, bm=1024, bn=1024, bk=2048, name="ff2", sides=nxt)
        if cast_next:
            w_ob, *w_ups = cast_next
        xf, xb = layer_norm(r2, ln2_g[l], ln2_b[l], emit_f32=(l + 1 == DEPTH))
    return xf.reshape(bsz, seq, d).astype(x.dtype)
```

```python
import functools
from typing import NamedTuple

import jax
import jax.numpy as jnp
from jax import lax
from jax.experimental import pallas as pl
from jax.experimental.pallas import tpu as pltpu

D_MODEL = 4096
SWA_HEADS, SWA_KV_HEADS, SWA_HEAD_DIM, SWA_WINDOW = 16, 2, 64, 128
MLSTM_HEADS, MLSTM_QK_DIM, MLSTM_V_DIM = 8, 64, 128
FOX_HEADS, FOX_HEAD_DIM = 8, 128
LN_EPS = 1e-5
DEPTH = 2
DN_ALPHA = (2 * DEPTH) ** 0.25

A_Q = SWA_HEADS * SWA_HEAD_DIM
A_KV = SWA_KV_HEADS * SWA_HEAD_DIM
B_QK = MLSTM_HEADS * MLSTM_QK_DIM
B_V = MLSTM_HEADS * MLSTM_V_DIM
C_W = FOX_HEADS * FOX_HEAD_DIM
SEG_WIDTHS = (A_Q, A_KV, A_KV, B_QK, B_QK, B_V, MLSTM_HEADS, MLSTM_HEADS, B_V,
              C_W, C_W, C_W, FOX_HEADS, D_MODEL, D_MODEL, D_MODEL)
SEG_OFFS = tuple(sum(SEG_WIDTHS[:i]) for i in range(len(SEG_WIDTHS)))

LANES = 128
SUBLANES = 8
VMEM_LIMIT_CAP = 56 * 1024 * 1024
VMEM_SLACK = 4 * 1024 * 1024
MIXER_VMEM = 32 * 1024 * 1024

ZB_AQ = 0
ZB_MV = 1024
ZB_CQ = 2048
ZB_CK = 3072
ZB_CV = 4096
ZB_MQ = 5120
ZB_AK = 5632
ZB_AV = 5888
ZB_N = 6144
ZF_MO = 0
ZF_MK = 1024
ZF_G = 1536
ZF_N = 1664
WT = 512
WA_ZF = 0
WA_ZB = 2048
WA_N = WA_ZB + ZB_N

TILES = dict(in_proj_f32=(512, ZF_N), in_proj_bf16=(1024, 1024), gated_merge=(1024, 256),
             out_proj=(1024, 512), ff1=(1024, 1024), ff2=(1024, 1024, 2048))
LN_ROWS = 256
SIDE_ROWS = dict(gates=128, w_ff1=32, w_ff2=128, next_layer=16)

MLSTM_L = 256
FOX_BLK = 512
SWA_BLK = 128
NEG_INF = float("-inf")
LOG2E = 1.4426950408889634


def _cparams(sem, vmem_bytes):
    return pltpu.CompilerParams(dimension_semantics=sem,
                                vmem_limit_bytes=int(min(vmem_bytes, VMEM_LIMIT_CAP)))


_NT = (((1,), (1,)), ((), ()))


class Side(NamedTuple):
    src: jax.Array
    layer: int
    rows: int
    row_off: int = 0
    n_rows: int = 0


def _side_plumbing(sides, grid):
    steps = 1
    for n in grid:
        steps *= n

    def lin(*g):
        s = g[0]
        for a, n in zip(g[1:], grid[1:]):
            s = s * n + a
        return s

    in_specs, out_specs, out_shapes, vmem = [], [], [], 0
    for sd in sides:
        c = sd.src.shape[-1]
        r = sd.n_rows or sd.src.shape[-2]
        rows = sd.rows
        while r // rows > steps:
            rows *= 2
        assert r % rows == 0 and sd.row_off % SUBLANES == 0
        last = r // rows - 1

        def blk(*g, last=last):
            return jnp.minimum(lin(*g), last)

        if sd.row_off or sd.n_rows:
            el = pl.Element
            in_specs.append(pl.BlockSpec(
                (el(1), el(rows), el(c)),
                lambda *g, blk=blk, sd=sd, rows=rows: (
                    sd.layer, pl.multiple_of(sd.row_off + blk(*g) * rows, SUBLANES), 0)))
        else:
            in_specs.append(pl.BlockSpec((None, rows, c),
                                         lambda *g, blk=blk, l=sd.layer: (l, blk(*g), 0)))
        out_specs.append(pl.BlockSpec((rows, c), lambda *g, blk=blk: (blk(*g), 0)))
        out_shapes.append(jax.ShapeDtypeStruct((r, c), jnp.bfloat16))
        vmem += 2 * rows * c * (4 + 2)
    return in_specs, out_specs, out_shapes, vmem


def _cast_sides(side_in, side_out):
    for si, so in zip(side_in, side_out):
        so[...] = si[...].reshape(so.shape).astype(so.dtype)


def _mm_body(x_ref, w_ref, *rest, act, w_t, n_side):
    side_in, o_ref, side_out = rest[:n_side], rest[n_side], rest[n_side + 1:]
    _cast_sides(side_in, side_out)
    if w_t:
        acc = lax.dot_general(x_ref[...], w_ref[...], _NT, preferred_element_type=jnp.float32)
    else:
        acc = jnp.dot(x_ref[...], w_ref[...], preferred_element_type=jnp.float32)
    if act == "relu2":
        acc = jnp.square(jnp.maximum(acc, 0.0))
    o_ref[...] = acc.astype(o_ref.dtype)


def _residual(res_refs):
    if len(res_refs) == 1:
        return res_refs[0][...]
    return _ln_apply(*(ref[...] for ref in res_refs))


def _mm_res_body(x_ref, w_ref, *rest):
    res_refs, o_ref = rest[:-1], rest[-1]
    acc = jnp.dot(x_ref[...], w_ref[...], preferred_element_type=jnp.float32)
    o_ref[...] = DN_ALPHA * _residual(res_refs) + acc


def _mm_res_k_body(x_ref, w_ref, *rest, n_res, n_side):
    res_refs, rest = rest[:n_res], rest[n_res:]
    side_in, o_ref, side_out = rest[:n_side], rest[n_side], rest[n_side + 1:]
    k = pl.program_id(2)

    @pl.when(k == 0)
    def _():
        _cast_sides(side_in, side_out)
        o_ref[...] = DN_ALPHA * _residual(res_refs) + jnp.dot(x_ref[...], w_ref[...],
                                                              preferred_element_type=jnp.float32)

    @pl.when(k > 0)
    def _():
        _cast_sides(side_in, side_out)
        o_ref[...] += jnp.dot(x_ref[...], w_ref[...], preferred_element_type=jnp.float32)


def matmul(x, w, *, bm, bn, out_dtype, act=None, name, n=None, w_row=None, sides=()):
    m, k = x.shape
    w_t = w_row is not None
    n = w.shape[1] if n is None else n
    bm, bn = min(bm, m), min(bn, n)
    assert m % bm == 0 and n % bn == 0
    osz = jnp.dtype(out_dtype).itemsize
    grid = (m // bm, n // bn)
    s_in, s_out, s_shapes, s_vmem = _side_plumbing(sides, grid)
    vmem = 2 * (bm * k * 2 + k * bn * 2 + bm * bn * osz) + bm * bn * 4 + VMEM_SLACK + s_vmem
    if w_t:
        assert w_row % bn == 0
        off = w_row // bn
        w_spec = pl.BlockSpec((bn, k), lambda i, j: (off + j, 0))
    else:
        w_spec = pl.BlockSpec((k, bn), lambda i, j: (0, j))
    return pl.pallas_call(
        functools.partial(_mm_body, act=act, w_t=w_t, n_side=len(sides)),
        grid=grid,
        in_specs=[pl.BlockSpec((bm, k), lambda i, j: (i, 0)), w_spec] + s_in,
        out_specs=[pl.BlockSpec((bm, bn), lambda i, j: (i, j))] + s_out,
        out_shape=[jax.ShapeDtypeStruct((m, n), out_dtype)] + s_shapes,
        compiler_params=_cparams(("arbitrary", "arbitrary"), vmem),
        name=name,
    )(x, w, *[sd.src for sd in sides])


def matmul_residual(x, w, res, *, bm, bn, bk, name, sides=()):
    m, k = x.shape
    n = w.shape[1]
    bm, bn, bk = min(bm, m), min(bn, n), min(bk, k)
    assert m % bm == 0 and n % bn == 0 and k % bk == 0
    res_args = tuple(res) if isinstance(res, LnRes) else (res,)
    res_dims = [(bm, bn, True, True), (bm, 1, True, False), (bm, 1, True, False),
                (1, bn, False, True), (1, bn, False, True)][:len(res_args)]

    def res_specs(ij):
        return [pl.BlockSpec((r, c), lambda *g, ri=ri, ci=ci: (ij(*g)[0] if ri else 0, ij(*g)[1] if ci else 0))
                for r, c, ri, ci in res_dims]

    if bk == k:
        assert not sides
        vmem = 2 * (bm * k * 2 + k * bn * 2 + 2 * bm * bn * 4) + bm * bn * 4 + VMEM_SLACK
        return [pl.pallas_call(
            _mm_res_body,
            grid=(m // bm, n // bn),
            in_specs=[pl.BlockSpec((bm, k), lambda i, j: (i, 0)),
                      pl.BlockSpec((k, bn), lambda i, j: (0, j))] + res_specs(lambda i, j: (i, j)),
            out_specs=pl.BlockSpec((bm, bn), lambda i, j: (i, j)),
            out_shape=jax.ShapeDtypeStruct((m, n), jnp.float32),
            compiler_params=_cparams(("parallel", "arbitrary"), vmem),
            name=name,
        )(x, w, *res_args)]
    grid = (m // bm, n // bn, k // bk)
    s_in, s_out, s_shapes, s_vmem = _side_plumbing(sides, grid)
    vmem = 2 * (bm * bk * 2 + bk * bn * 2 + 2 * bm * bn * 4) + 2 * bm * bn * 4 + VMEM_SLACK + s_vmem
    return pl.pallas_call(
        functools.partial(_mm_res_k_body, n_res=len(res_args), n_side=len(sides)),
        grid=grid,
        in_specs=[pl.BlockSpec((bm, bk), lambda i, j, kk: (i, kk)),
                  pl.BlockSpec((bk, bn), lambda i, j, kk: (kk, j))]
        + res_specs(lambda i, j, kk: (i, j)) + s_in,
        out_specs=[pl.BlockSpec((bm, bn), lambda i, j, kk: (i, j))] + s_out,
        out_shape=[jax.ShapeDtypeStruct((m, n), jnp.float32)] + s_shapes,
        compiler_params=_cparams(("arbitrary", "arbitrary", "arbitrary"), vmem),
        name=name,
    )(x, w, *res_args, *[sd.src for sd in sides])


N_BRANCH = 3


def _merge_body(x_ref, wg0_ref, wg1_ref, wg2_ref, y_ref, wu0_ref, wu1_ref, wu2_ref, *rest, n_side):
    side_in, o_ref, side_out = rest[:n_side], rest[n_side], rest[n_side + 1:]
    _cast_sides(side_in, side_out)
    mix = None
    for br, (wg_ref, wu_ref) in enumerate(((wg0_ref, wu0_ref), (wg1_ref, wu1_ref), (wg2_ref, wu2_ref))):
        g = lax.dot_general(x_ref[...], wg_ref[...], _NT, preferred_element_type=jnp.float32)
        u = jnp.dot(y_ref[br], wu_ref[...], preferred_element_type=jnp.float32)
        c = u * (1.0 / (1.0 + jnp.exp(-g)))
        mix = c if mix is None else mix + c
    o_ref[...] = mix.astype(o_ref.dtype)


def gated_merge(xb, wg, y, wus, *, bm, bn, sides=()):
    t, d = xb.shape
    nbr, _, kin = y.shape
    assert nbr == N_BRANCH == len(wus) and wg.shape == (nbr * d, d)
    bm, bn = min(bm, t), min(bn, d)
    assert d % bn == 0
    gstride = d // bn
    grid = (t // bm, d // bn)
    s_in, s_out, s_shapes, s_vmem = _side_plumbing(sides, grid)
    vmem = bm * d * 2 + 2 * (nbr * (d * bn * 2 + bm * kin * 2 + kin * bn * 2) + bm * bn * 2) \
        + 2 * nbr * bm * bn * 4 + 2 * VMEM_SLACK + s_vmem
    wg_specs = [pl.BlockSpec((bn, d), lambda i, j, b=b: (b * gstride + j, 0)) for b in range(nbr)]
    wu_spec = pl.BlockSpec((kin, bn), lambda i, j: (0, j))
    return pl.pallas_call(
        functools.partial(_merge_body, n_side=len(sides)),
        grid=grid,
        in_specs=[pl.BlockSpec((bm, d), lambda i, j: (i, 0), pipeline_mode=pl.Buffered(1))] + wg_specs
        + [pl.BlockSpec((nbr, bm, kin), lambda i, j: (0, i, 0)), wu_spec, wu_spec, wu_spec] + s_in,
        out_specs=[pl.BlockSpec((bm, bn), lambda i, j: (i, j))] + s_out,
        out_shape=[jax.ShapeDtypeStruct((t, d), jnp.bfloat16)] + s_shapes,
        compiler_params=_cparams(("arbitrary", "arbitrary"), vmem),
        name="gated_merge",
    )(xb, wg, wg, wg, y, *wus, *[sd.src for sd in sides])


class LnRes(NamedTuple):
    r: jax.Array
    mu: jax.Array
    rstd: jax.Array
    g: jax.Array
    b: jax.Array


def _ln_apply(r, mu, rstd, g, b):
    return (r - mu) * rstd * g + b


def _ln_body(r_ref, g_ref, b_ref, *outs, emit_f32):
    r = r_ref[...]
    mu = jnp.mean(r, axis=-1, keepdims=True)
    xc = r - mu
    rstd = lax.rsqrt(jnp.mean(xc * xc, axis=-1, keepdims=True) + LN_EPS)
    y = _ln_apply(r, mu, rstd, g_ref[...], b_ref[...])
    if emit_f32:
        of_ref, ob_ref = outs
        of_ref[...] = y
    else:
        ob_ref, mu_ref, rstd_ref = outs
        mu_ref[...] = mu
        rstd_ref[...] = rstd
    ob_ref[...] = y.astype(jnp.bfloat16)


def layer_norm(r, g, b, *, emit_f32, bm=LN_ROWS):
    t, d = r.shape
    bm = min(bm, t)
    g2, b2 = g.reshape(1, d), b.reshape(1, d)
    row = pl.BlockSpec((bm, d), lambda i: (i, 0))
    stat = pl.BlockSpec((bm, 1), lambda i: (i, 0))
    vmem = 2 * (bm * d * 4 * 2 + bm * d * 2) + 4 * bm * d * 4 + VMEM_SLACK
    if emit_f32:
        out_specs = [row, row]
        out_shape = [jax.ShapeDtypeStruct((t, d), jnp.float32), jax.ShapeDtypeStruct((t, d), jnp.bfloat16)]
    else:
        out_specs = [row, stat, stat]
        out_shape = [jax.ShapeDtypeStruct((t, d), jnp.bfloat16),
                     jax.ShapeDtypeStruct((t, 1), jnp.float32), jax.ShapeDtypeStruct((t, 1), jnp.float32)]
    outs = pl.pallas_call(
        functools.partial(_ln_body, emit_f32=emit_f32),
        grid=(t // bm,),
        in_specs=[row, pl.BlockSpec((1, d), lambda i: (0, 0)), pl.BlockSpec((1, d), lambda i: (0, 0))],
        out_specs=out_specs,
        out_shape=out_shape,
        compiler_params=_cparams(("parallel",), vmem),
        name="layer_norm",
    )(r, g2, b2)
    if emit_f32:
        return outs[0], outs[1]
    xb, mu, rstd = outs
    return LnRes(r, mu, rstd, g2, b2), xb


def _swa_body(sink_ref, bias_ref, q_ref, kp_ref, kc_ref, vp_ref, vc_ref, ybuf_ref, o_ref):
    del ybuf_ref
    blk = SWA_BLK
    c2 = SWA_HEAD_DIM ** -0.5 * LOG2E
    lo = lax.broadcasted_iota(jnp.int32, (2 * blk, LANES), 1) < SWA_HEAD_DIM
    group = SWA_HEADS // SWA_KV_HEADS
    pairs = SWA_HEADS // 2
    kz, vz = [], []
    for g in range(SWA_KV_HEADS):
        kd = jnp.concatenate([kp_ref[:, g * LANES:(g + 1) * LANES],
                              kc_ref[:, g * LANES:(g + 1) * LANES]], axis=0)
        vd = jnp.concatenate([vp_ref[:, g * LANES:(g + 1) * LANES],
                              vc_ref[:, g * LANES:(g + 1) * LANES]], axis=0)
        zero = jnp.zeros_like(kd)
        kz.append(jnp.concatenate([jnp.where(lo, kd, zero), jnp.where(lo, zero, kd)], axis=0))
        vz.append(jnp.concatenate([jnp.where(lo, vd, zero), jnp.where(lo, zero, vd)], axis=0))
    s2 = [lax.dot_general(q_ref[:, p * LANES:(p + 1) * LANES], kz[p // (group // 2)],
                          (((1,), (1,)), ((), ())), preferred_element_type=jnp.float32)
          for p in range(pairs)]
    for p in range(pairs):
        probs = []
        for e in range(2):
            h = 2 * p + e
            sink = sink_ref[h] * LOG2E
            s = s2[p][:, e * 2 * blk:(e + 1) * 2 * blk] * c2 + bias_ref[h]
            m = jnp.maximum(jnp.max(s, axis=-1, keepdims=True), sink)
            pe = jnp.exp2(s - m)
            den = jnp.sum(pe, axis=-1, keepdims=True) + jnp.exp2(sink - m)
            probs.append((pe * (1.0 / den)).astype(jnp.bfloat16))
        p2 = jnp.concatenate(probs, axis=1)
        o_ref[:, p * LANES:(p + 1) * LANES] = jnp.dot(
            p2, vz[p // (group // 2)], preferred_element_type=jnp.float32).astype(o_ref.dtype)


def _swa_bias_table():
    blk = SWA_BLK
    qi = lax.broadcasted_iota(jnp.int32, (blk, 2 * blk), 0)
    kj = lax.broadcasted_iota(jnp.int32, (blk, 2 * blk), 1)
    dist = qi + blk - kj
    window = (dist >= 0) & (dist < SWA_WINDOW)
    slopes = 2.0 ** (-8.0 * jnp.arange(1, SWA_HEADS + 1, dtype=jnp.float32) / SWA_HEADS)
    bias = -(LOG2E * slopes)[:, None, None] * dist.astype(jnp.float32)[None]
    later = jnp.where(window[None], bias, NEG_INF)
    first = jnp.where((window & (kj >= blk))[None], bias, NEG_INF)
    return jnp.stack([first, later])


def swa_attention(zb, sinks, ybuf, bsz, seq):
    blk = SWA_BLK
    nb = seq // blk
    kcol, vcol = ZB_AK // (2 * LANES), ZB_AV // (2 * LANES)

    def cur(b, n):
        return b * nb + n

    def prev(b, n):
        return b * nb + jnp.maximum(n - 1, 0)

    return pl.pallas_call(
        _swa_body,
        grid=(bsz, nb),
        in_specs=[pl.BlockSpec(memory_space=pltpu.SMEM),
                  pl.BlockSpec((None, SWA_HEADS, blk, 2 * blk),
                               lambda b, n: (jnp.minimum(n, 1), 0, 0, 0)),
                  pl.BlockSpec((blk, A_Q), lambda b, n: (cur(b, n), ZB_AQ // A_Q)),
                  pl.BlockSpec((blk, 2 * LANES), lambda b, n: (prev(b, n), kcol)),
                  pl.BlockSpec((blk, 2 * LANES), lambda b, n: (cur(b, n), kcol)),
                  pl.BlockSpec((blk, 2 * LANES), lambda b, n: (prev(b, n), vcol)),
                  pl.BlockSpec((blk, 2 * LANES), lambda b, n: (cur(b, n), vcol)),
                  pl.BlockSpec(memory_space=pl.ANY)],
        out_specs=pl.BlockSpec((None, blk, A_Q), lambda b, n: (0, cur(b, n), 0)),
        out_shape=jax.ShapeDtypeStruct(ybuf.shape, ybuf.dtype),
        input_output_aliases={7: 0},
        compiler_params=_cparams(("parallel", "arbitrary"), MIXER_VMEM),
        name="swa_attention",
    )(sinks, _swa_bias_table(), zb, zb, zb, zb, zb, ybuf)


def _mlstm_body(bias_ref, q_ref, k_ref, v_ref, og_ref, g_ref, ybuf_ref, y_ref, cum_ref,
                c_scr, n_scr, m_scr, carry_scr):
    del ybuf_ref
    c = pl.program_id(1)
    L = MLSTM_L
    dk, dv = MLSTM_QK_DIM, MLSTM_V_DIM

    @pl.when(c == 0)
    def _():
        c_scr[...] = jnp.zeros_like(c_scr)
        n_scr[...] = jnp.zeros_like(n_scr)
        m_scr[...] = jnp.zeros_like(m_scr)
        carry_scr[...] = jnp.zeros_like(carry_scr)

    nh = MLSTM_HEADS
    a = g_ref[...] + bias_ref[...]
    lf = jnp.minimum(a, 0.0) - jnp.log1p(jnp.exp(-jnp.abs(a)))
    row = lax.broadcasted_iota(jnp.int32, (L, L), 0)
    col = lax.broadcasted_iota(jnp.int32, (L, L), 1)
    causal = row >= col
    tri = causal.astype(jnp.float32)
    b_all = jnp.dot(tri, lf, preferred_element_type=jnp.float32,
                    precision=lax.Precision.HIGHEST)
    cum = b_all + carry_scr[0:1, :]
    carry_scr[0:1, :] = cum[L - 1:L, :]
    cum_ref[0] = cum.T

    b2 = pltpu.roll(b_all, LANES - nh, axis=1) * LOG2E
    g2 = a * LOG2E - b2
    rowi = lax.broadcasted_iota(jnp.int32, (L, LANES), 0)
    cm2 = g2
    d = 1
    while d < L:
        cm2 = jnp.maximum(cm2, jnp.where(rowi >= d, pltpu.roll(cm2, d, axis=0), NEG_INF))
        d *= 2
    m2_prev = m_scr[0:1, :]
    u2 = jnp.maximum(m2_prev, cm2)
    u2_last = u2[L - 1:L, :]
    w_inter_all = jnp.exp2(m2_prev - u2)
    floor_all = jnp.exp2(-(b2 + u2))
    wk_all = jnp.exp2(g2 - u2_last)
    decay_row = jnp.exp2(m2_prev - u2_last)
    m_scr[0:1, :] = b2[L - 1:L, :] + u2_last
    g2_t = g2.T

    lane = lax.broadcasted_iota(jnp.int32, (L, LANES), 1)
    lo = lane < dk
    crow = lax.broadcasted_iota(jnp.int32, (2 * dk, 2 * dv), 0)
    ccol = lax.broadcasted_iota(jnp.int32, (2 * dk, 2 * dv), 1)
    crow_lo = crow < dk
    blockdiag = crow_lo == (ccol < dv)
    nlane_lo = lax.broadcasted_iota(jnp.int32, (1, LANES), 1) < dk

    qps, kps, scs, cps, qcs = [], [], [], [], []
    for p in range(MLSTM_HEADS // 2):
        qp = q_ref[:, p * LANES:(p + 1) * LANES]
        kp = k_ref[:, p * LANES:(p + 1) * LANES] * (dk ** -0.5)
        kpb = kp.astype(jnp.bfloat16)
        zero = jnp.zeros_like(kpb)
        kz = jnp.concatenate([jnp.where(lo, kpb, zero), jnp.where(lo, zero, kpb)], axis=0)
        scs.append(lax.dot_general(qp, kz, (((1,), (1,)), ((), ())),
                                   preferred_element_type=jnp.float32))
        cp = c_scr[p]
        qcs.append(jnp.dot(qp, cp.astype(jnp.bfloat16), preferred_element_type=jnp.float32))
        qps.append(qp); kps.append(kp); cps.append(cp)

    for p in range(MLSTM_HEADS // 2):
        qp, kp, sc, cp, qc = qps[p], kps[p], scs[p], cps[p], qcs[p]
        n_row = n_scr[p:p + 1, :]
        qn_prod = qp.astype(jnp.float32) * n_row
        for e in range(2):
            h = 2 * p + e
            decay_mat = jnp.exp2(jnp.where(causal, g2_t[h:h + 1, :] - u2[:, h:h + 1], NEG_INF))
            smat = sc[:, e * L:(e + 1) * L] * decay_mat
            w_inter = w_inter_all[:, h:h + 1]
            qn = jnp.sum(jnp.where(lo == (e == 0), qn_prod, 0.0), axis=-1, keepdims=True)
            num = w_inter * qc[:, e * dv:(e + 1) * dv] + jnp.dot(
                smat.astype(jnp.bfloat16), v_ref[:, h * dv:(h + 1) * dv],
                preferred_element_type=jnp.float32)
            den = w_inter * qn + jnp.sum(smat, axis=-1, keepdims=True)
            rden = 1.0 / jnp.maximum(jnp.abs(den), floor_all[:, h:h + 1])
            og = og_ref[:, h * dv:(h + 1) * dv]
            y_ref[:, h * dv:(h + 1) * dv] = (num * rden * (1.0 / (1.0 + jnp.exp(-og)))).astype(y_ref.dtype)
        decays = [decay_row[:, 2 * p + e:2 * p + e + 1] for e in range(2)]
        kw = kp * jnp.where(lo, wk_all[:, 2 * p:2 * p + 1], wk_all[:, 2 * p + 1:2 * p + 2])
        n_scr[p:p + 1, :] = (jnp.where(nlane_lo, decays[0], decays[1]) * n_row
                             + jnp.sum(kw, axis=0, keepdims=True))
        upd = jnp.dot(kw.T.astype(jnp.bfloat16), v_ref[:, 2 * p * dv:(2 * p + 2) * dv],
                      preferred_element_type=jnp.float32)
        c_scr[p] = jnp.where(crow_lo, decays[0], decays[1]) * cp + jnp.where(blockdiag, upd, 0.0)


def mlstm(zb, zf, bias_row, ybuf, bsz, seq):
    L = MLSTM_L
    nc = seq // L

    def rows(b, c):
        return b * nc + c

    return pl.pallas_call(
        _mlstm_body,
        grid=(bsz, nc),
        in_specs=[pl.BlockSpec((1, LANES), lambda b, c: (0, 0)),
                  pl.BlockSpec((L, B_QK), lambda b, c: (rows(b, c), ZB_MQ // B_QK)),
                  pl.BlockSpec((L, B_QK), lambda b, c: (rows(b, c), ZF_MK // B_QK)),
                  pl.BlockSpec((L, B_V), lambda b, c: (rows(b, c), ZB_MV // B_V)),
                  pl.BlockSpec((L, B_V), lambda b, c: (rows(b, c), ZF_MO // B_V)),
                  pl.BlockSpec((L, LANES), lambda b, c: (rows(b, c), ZF_G // LANES)),
                  pl.BlockSpec(memory_space=pl.ANY)],
        out_specs=[pl.BlockSpec((None, L, B_V), lambda b, c: (1, rows(b, c), 0)),
                   pl.BlockSpec((1, LANES, L), lambda b, c: (b, 0, c))],
        out_shape=[jax.ShapeDtypeStruct(ybuf.shape, ybuf.dtype),
                   jax.ShapeDtypeStruct((bsz, LANES, seq), jnp.float32)],
        input_output_aliases={6: 0},
        scratch_shapes=[pltpu.VMEM((MLSTM_HEADS // 2, 2 * MLSTM_QK_DIM, 2 * MLSTM_V_DIM), jnp.float32),
                        pltpu.VMEM((8, LANES), jnp.float32),
                        pltpu.VMEM((8, LANES), jnp.float32),
                        pltpu.VMEM((8, LANES), jnp.float32)],
        compiler_params=_cparams(("arbitrary", "arbitrary"), MIXER_VMEM),
        name="mlstm",
    )(bias_row, zb, zf, zb, zf, zf, ybuf)


def _fox_body(q_ref, k_ref, v_ref, ck_ref, ybuf_ref, o_ref):
    del ybuf_ref
    hp = pl.program_id(1)
    qi = pl.program_id(2)
    blk = FOX_BLK
    dh = FOX_HEAD_DIM
    c2 = FOX_HEAD_DIM ** -0.5 * LOG2E

    def scores(j, e):
        start = pl.multiple_of(j * blk, blk)
        s = lax.dot_general(q_ref[:, e * dh:(e + 1) * dh], k_ref[pl.ds(start, blk), e * dh:(e + 1) * dh],
                            (((1,), (1,)), ((), ())), preferred_element_type=jnp.float32)
        return s * c2 - ck_ref[0, 2 * hp + e, pl.ds(j, 1), :] * LOG2E

    def update(s, j, e, m, l, acc):
        start = pl.multiple_of(j * blk, blk)
        m_new = jnp.maximum(m, jnp.max(s, axis=-1, keepdims=True))
        alpha = jnp.exp2(m - m_new)
        p = jnp.exp2(s - m_new)
        l = alpha * l + jnp.sum(p, axis=-1, keepdims=True)
        acc = alpha * acc + jnp.dot(p.astype(jnp.bfloat16),
                                    v_ref[pl.ds(start, blk), e * dh:(e + 1) * dh],
                                    preferred_element_type=jnp.float32)
        return m_new, l, acc

    def step(j, carry):
        s = [scores(j, e) for e in range(2)]
        return tuple(update(s[e], j, e, *carry[e]) for e in range(2))

    init1 = (jnp.full((blk, 1), NEG_INF, jnp.float32), jnp.zeros((blk, 1), jnp.float32),
             jnp.zeros((blk, dh), jnp.float32))
    carry = lax.fori_loop(0, qi, step, (init1, init1))
    row = lax.broadcasted_iota(jnp.int32, (blk, blk), 0)
    colm = lax.broadcasted_iota(jnp.int32, (blk, blk), 1)
    s = [jnp.where(colm <= row, scores(qi, e), NEG_INF) for e in range(2)]
    for e in range(2):
        _, l, acc = update(s[e], qi, e, *carry[e])
        o_ref[:, e * dh:(e + 1) * dh] = (acc / l).astype(o_ref.dtype)


def fox_attention(zb, cum_t, ybuf, bsz, seq):
    blk = FOX_BLK
    nq = seq // blk
    pw = 2 * FOX_HEAD_DIM
    ck = cum_t.reshape(bsz, LANES, nq, blk)
    return pl.pallas_call(
        _fox_body,
        grid=(bsz, FOX_HEADS // 2, nq),
        in_specs=[pl.BlockSpec((blk, pw), lambda b, h, i: (b * nq + i, ZB_CQ // pw + h)),
                  pl.BlockSpec((seq, pw), lambda b, h, i: (b, ZB_CK // pw + h)),
                  pl.BlockSpec((seq, pw), lambda b, h, i: (b, ZB_CV // pw + h)),
                  pl.BlockSpec((1, 8, nq, blk), lambda b, h, i: (b, 2, 0, 0)),
                  pl.BlockSpec(memory_space=pl.ANY)],
        out_specs=pl.BlockSpec((None, blk, pw), lambda b, h, i: (2, b * nq + i, h)),
        out_shape=jax.ShapeDtypeStruct(ybuf.shape, ybuf.dtype),
        input_output_aliases={4: 0},
        compiler_params=_cparams(("parallel", "arbitrary", "arbitrary"), MIXER_VMEM),
        name="fox_attention",
    )(zb, zb, zb, ck, ybuf)


RG_COPY, RG_DUP, RG_GATES = 0, 1, 2
_SEG = dict(a_q=0, a_k=1, a_v=2, m_q=3, m_k=4, m_v=5, m_i=6, m_f=7, m_o=8,
            c_q=9, c_k=10, c_v=11, c_f=12, g_a=13, g_b=14, g_c=15)


def _regroup_table():
    tiles = []

    def copy(name):
        off, width = SEG_OFFS[_SEG[name]], SEG_WIDTHS[_SEG[name]]
        assert width % WT == 0 and off % SUBLANES == 0
        tiles.extend((off + k * WT, RG_COPY) for k in range(width // WT))

    copy("m_o"); copy("m_k")
    m_i, m_f, c_f = (SEG_OFFS[_SEG[n]] for n in ("m_i", "m_f", "c_f"))
    assert m_f == m_i + MLSTM_HEADS and m_i % SUBLANES == 0 and c_f % SUBLANES == 0
    tiles.append((m_i, RG_GATES))
    assert len(tiles) * WT == WA_ZB
    for name in ("a_q", "m_v", "c_q", "c_k", "c_v", "m_q"):
        copy(name)
    a_k, a_v = SEG_OFFS[_SEG["a_k"]], SEG_OFFS[_SEG["a_v"]]
    assert a_v == a_k + A_KV and a_k % SUBLANES == 0 and 4 * A_KV == WT
    tiles.append((a_k, RG_DUP))
    assert len(tiles) * WT == WA_N
    return tiles, c_f


def _regroup_body(base_ref, mode_ref, w_ref, aux_ref, o_ref):
    del base_ref
    mode = mode_ref[pl.program_id(0)]
    bf = jnp.bfloat16
    hd = SWA_HEAD_DIM

    @pl.when(mode == RG_COPY)
    def _():
        o_ref[...] = w_ref[0].astype(bf)

    @pl.when(mode == RG_DUP)
    def _():
        for i in range(2 * SWA_KV_HEADS):
            head = w_ref[0, i * hd:(i + 1) * hd, :].astype(bf)
            o_ref[2 * i * hd:(2 * i + 1) * hd, :] = head
            o_ref[(2 * i + 1) * hd:(2 * i + 2) * hd, :] = head

    @pl.when(mode == RG_GATES)
    def _():
        nh = MLSTM_HEADS
        o_ref[0:2 * nh, :] = w_ref[0, 0:2 * nh, :].astype(bf)
        o_ref[2 * nh:3 * nh, :] = aux_ref[0].astype(bf)
        o_ref[3 * nh:WT, :] = jnp.zeros((WT - 3 * nh, o_ref.shape[1]), bf)


def regroup_w_in(wt, l):
    tiles, c_f = _regroup_table()
    base = jnp.asarray([b for b, _ in tiles], jnp.int32)
    mode = jnp.asarray([m for _, m in tiles], jnp.int32)
    k = wt.shape[2]
    el = pl.Element
    in_specs = [pl.BlockSpec((el(1), el(WT), el(k)),
                             lambda t, base, mode: (l, pl.multiple_of(base[t], SUBLANES), 0)),
                pl.BlockSpec((el(1), el(FOX_HEADS), el(k)), lambda t, base, mode: (l, c_f, 0))]
    vmem = 2 * (WT * k * 4 + WT * k * 2) + 3 * VMEM_SLACK
    return pl.pallas_call(
        _regroup_body,
        grid_spec=pltpu.PrefetchScalarGridSpec(
            num_scalar_prefetch=2, grid=(len(tiles),), in_specs=in_specs,
            out_specs=pl.BlockSpec((WT, k), lambda t, base, mode: (t, 0))),
        out_shape=jax.ShapeDtypeStruct((WA_N, k), jnp.bfloat16),
        compiler_params=_cparams(("arbitrary",), vmem),
        name="regroup_w_in",
    )(base, mode, wt, wt)


def _cast_body(w_ref, o_ref):
    o_ref[...] = w_ref[...].astype(o_ref.dtype)


def cast_bf16(w, l=None, *, br=1024, bc=2048):
    r, c = w.shape[-2:]
    br, bc = min(br, r), min(bc, c)
    assert r % br == 0 and c % bc == 0
    if l is None:
        in_spec = pl.BlockSpec((br, bc), lambda i, j: (i, j))
    else:
        in_spec = pl.BlockSpec((None, br, bc), lambda i, j: (l, i, j))
    return pl.pallas_call(
        _cast_body,
        grid=(r // br, c // bc),
        in_specs=[in_spec],
        out_specs=pl.BlockSpec((br, bc), lambda i, j: (i, j)),
        out_shape=jax.ShapeDtypeStruct((r, c), jnp.bfloat16),
        compiler_params=_cparams(("parallel", "parallel"), 2 * br * bc * 6 + 2 * VMEM_SLACK),
        name="cast_bf16",
    )(w)


def kernel(x, w_in, b_mlstm_i, b_mlstm_f, b_fox_f, attn_sinks, w_up_swa, w_up_mlstm, w_up_fox,
           w_o, ln1_g, ln1_b, w_ff1, w_ff2, ln2_g, ln2_b):
    bsz, seq, d = x.shape
    t = bsz * seq
    assert d == D_MODEL and seq % FOX_BLK == 0 and seq % MLSTM_L == 0
    xf = x.reshape(t, d).astype(jnp.float32)
    xb = cast_bf16(xf)
    gate_pad = jnp.zeros((LANES - 3 * MLSTM_HEADS,), jnp.float32)
    ups = (w_up_swa, w_up_mlstm, w_up_fox)
    w_ups = [cast_bf16(w, 0) for w in ups]
    w_ob = cast_bf16(w_o, 0)
    w_int = jnp.swapaxes(w_in, 1, 2)
    for l in range(DEPTH):
        w_all = regroup_w_in(w_int, l)
        bias_row = jnp.concatenate([b_mlstm_i[l], b_mlstm_f[l], b_fox_f[l], gate_pad]).reshape(1, LANES)
        bm, bn = TILES["in_proj_f32"]
        (zf,) = matmul(xb, w_all, n=ZF_N, w_row=WA_ZF, bm=bm, bn=bn, out_dtype=jnp.float32,
                       name="in_proj_f32")
        gates = Side(w_int, l, SIDE_ROWS["gates"], row_off=SEG_OFFS[_SEG["g_a"]],
                     n_rows=N_BRANCH * D_MODEL)
        bm, bn = TILES["in_proj_bf16"]
        zb, w_g = matmul(xb, w_all, n=ZB_N, w_row=WA_ZB, bm=bm, bn=bn, out_dtype=jnp.bfloat16,
                         name="in_proj_bf16", sides=[gates])
        ybuf = jnp.zeros((3, t, A_Q), jnp.bfloat16)
        ybuf = swa_attention(zb, attn_sinks[l].astype(jnp.float32), ybuf, bsz, seq)
        ybuf, cum_t = mlstm(zb, zf, bias_row.astype(jnp.float32), ybuf, bsz, seq)
        ybuf = fox_attention(zb, cum_t, ybuf, bsz, seq)
        bm, bn = TILES["gated_merge"]
        mix, w_1b = gated_merge(xb, w_g, ybuf, w_ups, bm=bm, bn=bn,
                                sides=[Side(w_ff1, l, SIDE_ROWS["w_ff1"])])
        bm, bn = TILES["out_proj"]
        (r1,) = matmul_residual(mix, w_ob, xf, bm=bm, bn=bn, bk=D_MODEL, name="out_proj")
        xf, xb = layer_norm(r1, ln1_g[l], ln1_b[l], emit_f32=False)
        bm, bn = TILES["ff1"]
        hid, w_2b = matmul(xb, w_1b, bm=bm, bn=bn, out_dtype=jnp.bfloat16, act="relu2",
                           name="ff1", sides=[Side(w_ff2, l, SIDE_ROWS["w_ff2"])])
        nxt = [Side(w, l + 1, SIDE_ROWS["next_layer"]) for w in (w_o,) + ups] if l + 1 < DEPTH else []
        bm, bn, bk = TILES["ff2"]
        r2, *cast_next = matmul_residual(hid, w_2b, xf, bm=bm, bn=bn, bk=bk, name="ff2", sides=nxt)
        if cast_next:
            w_ob, *w_ups = cast_next
        xf, xb = layer_norm(r2, ln2_g[l], ln2_b[l], emit_f32=(l + 1 == DEPTH))
    return xf.reshape(bsz, seq, d).astype(x.dtype)
```

```python
import functools
from typing import NamedTuple

import jax
import jax.numpy as jnp
from jax import lax
from jax.experimental import pallas as pl
from jax.experimental.pallas import tpu as pltpu

D_MODEL = 4096
SWA_HEADS, SWA_KV_HEADS, SWA_HEAD_DIM, SWA_WINDOW = 16, 2, 64, 128
MLSTM_HEADS, MLSTM_QK_DIM, MLSTM_V_DIM = 8, 64, 128
FOX_HEADS, FOX_HEAD_DIM = 8, 128
LN_EPS = 1e-5
DEPTH = 2
DN_ALPHA = (2 * DEPTH) ** 0.25

A_Q = SWA_HEADS * SWA_HEAD_DIM
A_KV = SWA_KV_HEADS * SWA_HEAD_DIM
B_QK = MLSTM_HEADS * MLSTM_QK_DIM
B_V = MLSTM_HEADS * MLSTM_V_DIM
C_W = FOX_HEADS * FOX_HEAD_DIM
SEG_WIDTHS = (A_Q, A_KV, A_KV, B_QK, B_QK, B_V, MLSTM_HEADS, MLSTM_HEADS, B_V,
              C_W, C_W, C_W, FOX_HEADS, D_MODEL, D_MODEL, D_MODEL)
SEG_OFFS = tuple(sum(SEG_WIDTHS[:i]) for i in range(len(SEG_WIDTHS)))

LANES = 128
SUBLANES = 8
VMEM_LIMIT_CAP = 56 * 1024 * 1024
VMEM_SLACK = 4 * 1024 * 1024
MIXER_VMEM = 32 * 1024 * 1024

ZB_AQ = 0
ZB_MV = 1024
ZB_CQ = 2048
ZB_CK = 3072
ZB_CV = 4096
ZB_MQ = 5120
ZB_AK = 5632
ZB_AV = 5888
ZB_N = 6144
ZF_MO = 0
ZF_MK = 1024
ZF_G = 1536
ZF_N = 1664
WT = 512
WA_ZF = 0
WA_ZB = 2048
WA_N = WA_ZB + ZB_N

TILES = dict(in_proj_f32=(1024, ZF_N), in_proj_bf16=(1024, 1024), gated_merge=(1024, 256),
             out_proj=(1024, 512), ff1=(1024, 1024), ff2=(1024, 1024, 2048))
LN_ROWS = 256
SIDE_ROWS = dict(gates=128, w_ff1=32, w_ff2=128, next_layer=16)

MLSTM_L = 256
FOX_BLK = 512
SWA_BLK = 128
NEG_INF = float("-inf")
LOG2E = 1.4426950408889634


def _cparams(sem, vmem_bytes):
    return pltpu.CompilerParams(dimension_semantics=sem,
                                vmem_limit_bytes=int(min(vmem_bytes, VMEM_LIMIT_CAP)))


_NT = (((1,), (1,)), ((), ()))


class Side(NamedTuple):
    src: jax.Array
    layer: int
    rows: int
    row_off: int = 0
    n_rows: int = 0


def _side_plumbing(sides, grid):
    steps = 1
    for n in grid:
        steps *= n

    def lin(*g):
        s = g[0]
        for a, n in zip(g[1:], grid[1:]):
            s = s * n + a
        return s

    in_specs, out_specs, out_shapes, vmem = [], [], [], 0
    for sd in sides:
        c = sd.src.shape[-1]
        r = sd.n_rows or sd.src.shape[-2]
        rows = sd.rows
        while r // rows > steps:
            rows *= 2
        assert r % rows == 0 and sd.row_off % SUBLANES == 0
        last = r // rows - 1

        def blk(*g, last=last):
            return jnp.minimum(lin(*g), last)

        if sd.row_off or sd.n_rows:
            el = pl.Element
            in_specs.append(pl.BlockSpec(
                (el(1), el(rows), el(c)),
                lambda *g, blk=blk, sd=sd, rows=rows: (
                    sd.layer, pl.multiple_of(sd.row_off + blk(*g) * rows, SUBLANES), 0)))
        else:
            in_specs.append(pl.BlockSpec((None, rows, c),
                                         lambda *g, blk=blk, l=sd.layer: (l, blk(*g), 0)))
        out_specs.append(pl.BlockSpec((rows, c), lambda *g, blk=blk: (blk(*g), 0)))
        out_shapes.append(jax.ShapeDtypeStruct((r, c), jnp.bfloat16))
        vmem += 2 * rows * c * (4 + 2)
    return in_specs, out_specs, out_shapes, vmem


def _cast_sides(side_in, side_out):
    for si, so in zip(side_in, side_out):
        so[...] = si[...].reshape(so.shape).astype(so.dtype)


def _mm_body(x_ref, w_ref, *rest, act, w_t, n_side):
    side_in, o_ref, side_out = rest[:n_side], rest[n_side], rest[n_side + 1:]
    _cast_sides(side_in, side_out)
    if w_t:
        acc = lax.dot_general(x_ref[...], w_ref[...], _NT, preferred_element_type=jnp.float32)
    else:
        acc = jnp.dot(x_ref[...], w_ref[...], preferred_element_type=jnp.float32)
    if act == "relu2":
        acc = jnp.square(jnp.maximum(acc, 0.0))
    o_ref[...] = acc.astype(o_ref.dtype)


def _residual(res_refs):
    if len(res_refs) == 1:
        return res_refs[0][...]
    return _ln_apply(*(ref[...] for ref in res_refs))


def _mm_res_body(x_ref, w_ref, *rest):
    res_refs, o_ref = rest[:-1], rest[-1]
    acc = jnp.dot(x_ref[...], w_ref[...], preferred_element_type=jnp.float32)
    o_ref[...] = DN_ALPHA * _residual(res_refs) + acc


def _mm_res_k_body(x_ref, w_ref, *rest, n_res, n_side):
    res_refs, rest = rest[:n_res], rest[n_res:]
    side_in, o_ref, side_out = rest[:n_side], rest[n_side], rest[n_side + 1:]
    k = pl.program_id(2)

    @pl.when(k == 0)
    def _():
        _cast_sides(side_in, side_out)
        o_ref[...] = DN_ALPHA * _residual(res_refs) + jnp.dot(x_ref[...], w_ref[...],
                                                              preferred_element_type=jnp.float32)

    @pl.when(k > 0)
    def _():
        _cast_sides(side_in, side_out)
        o_ref[...] += jnp.dot(x_ref[...], w_ref[...], preferred_element_type=jnp.float32)


def matmul(x, w, *, bm, bn, out_dtype, act=None, name, n=None, w_row=None, sides=()):
    m, k = x.shape
    w_t = w_row is not None
    n = w.shape[1] if n is None else n
    bm, bn = min(bm, m), min(bn, n)
    assert m % bm == 0 and n % bn == 0
    osz = jnp.dtype(out_dtype).itemsize
    grid = (m // bm, n // bn)
    s_in, s_out, s_shapes, s_vmem = _side_plumbing(sides, grid)
    w_bufs, w_mode = (1, dict(pipeline_mode=pl.Buffered(1))) if grid[1] == 1 else (2, {})
    vmem = 2 * (bm * k * 2 + bm * bn * osz) + w_bufs * k * bn * 2 + bm * bn * 4 + VMEM_SLACK + s_vmem
    if w_t:
        assert w_row % bn == 0
        off = w_row // bn
        w_spec = pl.BlockSpec((bn, k), lambda i, j: (off + j, 0), **w_mode)
    else:
        w_spec = pl.BlockSpec((k, bn), lambda i, j: (0, j), **w_mode)
    return pl.pallas_call(
        functools.partial(_mm_body, act=act, w_t=w_t, n_side=len(sides)),
        grid=grid,
        in_specs=[pl.BlockSpec((bm, k), lambda i, j: (i, 0)), w_spec] + s_in,
        out_specs=[pl.BlockSpec((bm, bn), lambda i, j: (i, j))] + s_out,
        out_shape=[jax.ShapeDtypeStruct((m, n), out_dtype)] + s_shapes,
        compiler_params=_cparams(("arbitrary", "arbitrary"), vmem),
        name=name,
    )(x, w, *[sd.src for sd in sides])


def matmul_residual(x, w, res, *, bm, bn, bk, name, sides=()):
    m, k = x.shape
    n = w.shape[1]
    bm, bn, bk = min(bm, m), min(bn, n), min(bk, k)
    assert m % bm == 0 and n % bn == 0 and k % bk == 0
    res_args = tuple(res) if isinstance(res, LnRes) else (res,)
    res_dims = [(bm, bn, True, True), (bm, 1, True, False), (bm, 1, True, False),
                (1, bn, False, True), (1, bn, False, True)][:len(res_args)]

    def res_specs(ij):
        return [pl.BlockSpec((r, c), lambda *g, ri=ri, ci=ci: (ij(*g)[0] if ri else 0, ij(*g)[1] if ci else 0))
                for r, c, ri, ci in res_dims]

    if bk == k:
        assert not sides
        vmem = 2 * (bm * k * 2 + k * bn * 2 + 2 * bm * bn * 4) + bm * bn * 4 + VMEM_SLACK
        return [pl.pallas_call(
            _mm_res_body,
            grid=(m // bm, n // bn),
            in_specs=[pl.BlockSpec((bm, k), lambda i, j: (i, 0)),
                      pl.BlockSpec((k, bn), lambda i, j: (0, j))] + res_specs(lambda i, j: (i, j)),
            out_specs=pl.BlockSpec((bm, bn), lambda i, j: (i, j)),
            out_shape=jax.ShapeDtypeStruct((m, n), jnp.float32),
            compiler_params=_cparams(("parallel", "arbitrary"), vmem),
            name=name,
        )(x, w, *res_args)]
    grid = (m // bm, n // bn, k // bk)
    s_in, s_out, s_shapes, s_vmem = _side_plumbing(sides, grid)
    vmem = 2 * (bm * bk * 2 + bk * bn * 2 + 2 * bm * bn * 4) + 2 * bm * bn * 4 + VMEM_SLACK + s_vmem
    return pl.pallas_call(
        functools.partial(_mm_res_k_body, n_res=len(res_args), n_side=len(sides)),
        grid=grid,
        in_specs=[pl.BlockSpec((bm, bk), lambda i, j, kk: (i, kk)),
                  pl.BlockSpec((bk, bn), lambda i, j, kk: (kk, j))]
        + res_specs(lambda i, j, kk: (i, j)) + s_in,
        out_specs=[pl.BlockSpec((bm, bn), lambda i, j, kk: (i, j))] + s_out,
        out_shape=[jax.ShapeDtypeStruct((m, n), jnp.float32)] + s_shapes,
        compiler_params=_cparams(("arbitrary", "arbitrary", "arbitrary"), vmem),
        name=name,
    )(x, w, *res_args, *[sd.src for sd in sides])


N_BRANCH = 3


def _merge_body(x_ref, wg0_ref, wg1_ref, wg2_ref, y_ref, wu0_ref, wu1_ref, wu2_ref, *rest, n_side):
    side_in, o_ref, side_out = rest[:n_side], rest[n_side], rest[n_side + 1:]
    _cast_sides(side_in, side_out)
    mix = None
    for br, (wg_ref, wu_ref) in enumerate(((wg0_ref, wu0_ref), (wg1_ref, wu1_ref), (wg2_ref, wu2_ref))):
        g = lax.dot_general(x_ref[...], wg_ref[...], _NT, preferred_element_type=jnp.float32)
        u = jnp.dot(y_ref[br], wu_ref[...], preferred_element_type=jnp.float32)
        c = u * (1.0 / (1.0 + jnp.exp(-g)))
        mix = c if mix is None else mix + c
    o_ref[...] = mix.astype(o_ref.dtype)


def gated_merge(xb, wg, y, wus, *, bm, bn, sides=()):
    t, d = xb.shape
    nbr, _, kin = y.shape
    assert nbr == N_BRANCH == len(wus) and wg.shape == (nbr * d, d)
    bm, bn = min(bm, t), min(bn, d)
    assert d % bn == 0
    gstride = d // bn
    grid = (t // bm, d // bn)
    s_in, s_out, s_shapes, s_vmem = _side_plumbing(sides, grid)
    vmem = bm * d * 2 + 2 * (nbr * (d * bn * 2 + bm * kin * 2 + kin * bn * 2) + bm * bn * 2) \
        + 2 * nbr * bm * bn * 4 + 2 * VMEM_SLACK + s_vmem
    wg_specs = [pl.BlockSpec((bn, d), lambda i, j, b=b: (b * gstride + j, 0)) for b in range(nbr)]
    wu_spec = pl.BlockSpec((kin, bn), lambda i, j: (0, j))
    return pl.pallas_call(
        functools.partial(_merge_body, n_side=len(sides)),
        grid=grid,
        in_specs=[pl.BlockSpec((bm, d), lambda i, j: (i, 0), pipeline_mode=pl.Buffered(1))] + wg_specs
        + [pl.BlockSpec((nbr, bm, kin), lambda i, j: (0, i, 0)), wu_spec, wu_spec, wu_spec] + s_in,
        out_specs=[pl.BlockSpec((bm, bn), lambda i, j: (i, j))] + s_out,
        out_shape=[jax.ShapeDtypeStruct((t, d), jnp.bfloat16)] + s_shapes,
        compiler_params=_cparams(("arbitrary", "arbitrary"), vmem),
        name="gated_merge",
    )(xb, wg, wg, wg, y, *wus, *[sd.src for sd in sides])


class LnRes(NamedTuple):
    r: jax.Array
    mu: jax.Array
    rstd: jax.Array
    g: jax.Array
    b: jax.Array


def _ln_apply(r, mu, rstd, g, b):
    return (r - mu) * rstd * g + b


def _ln_body(r_ref, g_ref, b_ref, *outs, emit_f32):
    r = r_ref[...]
    mu = jnp.mean(r, axis=-1, keepdims=True)
    xc = r - mu
    rstd = lax.rsqrt(jnp.mean(xc * xc, axis=-1, keepdims=True) + LN_EPS)
    y = _ln_apply(r, mu, rstd, g_ref[...], b_ref[...])
    if emit_f32:
        (of_ref,) = outs
        of_ref[...] = y
    else:
        ob_ref, mu_ref, rstd_ref = outs
        mu_ref[...] = mu
        rstd_ref[...] = rstd
        ob_ref[...] = y.astype(jnp.bfloat16)


def layer_norm(r, g, b, *, emit_f32, bm=LN_ROWS):
    t, d = r.shape
    bm = min(bm, t)
    g2, b2 = g.reshape(1, d), b.reshape(1, d)
    row = pl.BlockSpec((bm, d), lambda i: (i, 0))
    stat = pl.BlockSpec((bm, 1), lambda i: (i, 0))
    vmem = 2 * (bm * d * 4 * 2 + bm * d * 2) + 4 * bm * d * 4 + VMEM_SLACK
    if emit_f32:
        out_specs = [row]
        out_shape = [jax.ShapeDtypeStruct((t, d), jnp.float32)]
    else:
        out_specs = [row, stat, stat]
        out_shape = [jax.ShapeDtypeStruct((t, d), jnp.bfloat16),
                     jax.ShapeDtypeStruct((t, 1), jnp.float32), jax.ShapeDtypeStruct((t, 1), jnp.float32)]
    outs = pl.pallas_call(
        functools.partial(_ln_body, emit_f32=emit_f32),
        grid=(t // bm,),
        in_specs=[row, pl.BlockSpec((1, d), lambda i: (0, 0)), pl.BlockSpec((1, d), lambda i: (0, 0))],
        out_specs=out_specs,
        out_shape=out_shape,
        compiler_params=_cparams(("parallel",), vmem),
        name="layer_norm",
    )(r, g2, b2)
    if emit_f32:
        return outs[0], None
    xb, mu, rstd = outs
    return LnRes(r, mu, rstd, g2, b2), xb


def _swa_body(sink_ref, bias_ref, q_ref, kp_ref, kc_ref, vp_ref, vc_ref, ybuf_ref, o_ref):
    del ybuf_ref
    blk = SWA_BLK
    c2 = SWA_HEAD_DIM ** -0.5 * LOG2E
    lo = lax.broadcasted_iota(jnp.int32, (2 * blk, LANES), 1) < SWA_HEAD_DIM
    group = SWA_HEADS // SWA_KV_HEADS
    pairs = SWA_HEADS // 2
    kz, vz = [], []
    for g in range(SWA_KV_HEADS):
        kd = jnp.concatenate([kp_ref[:, g * LANES:(g + 1) * LANES],
                              kc_ref[:, g * LANES:(g + 1) * LANES]], axis=0)
        vd = jnp.concatenate([vp_ref[:, g * LANES:(g + 1) * LANES],
                              vc_ref[:, g * LANES:(g + 1) * LANES]], axis=0)
        zero = jnp.zeros_like(kd)
        kz.append(jnp.concatenate([jnp.where(lo, kd, zero), jnp.where(lo, zero, kd)], axis=0))
        vz.append(jnp.concatenate([jnp.where(lo, vd, zero), jnp.where(lo, zero, vd)], axis=0))
    s2 = [lax.dot_general(q_ref[:, p * LANES:(p + 1) * LANES], kz[p // (group // 2)],
                          (((1,), (1,)), ((), ())), preferred_element_type=jnp.float32)
          for p in range(pairs)]
    for p in range(pairs):
        probs = []
        for e in range(2):
            h = 2 * p + e
            sink = sink_ref[h] * LOG2E
            s = s2[p][:, e * 2 * blk:(e + 1) * 2 * blk] * c2 + bias_ref[h]
            m = jnp.maximum(jnp.max(s, axis=-1, keepdims=True), sink)
            pe = jnp.exp2(s - m)
            den = jnp.sum(pe, axis=-1, keepdims=True) + jnp.exp2(sink - m)
            probs.append((pe * (1.0 / den)).astype(jnp.bfloat16))
        p2 = jnp.concatenate(probs, axis=1)
        o_ref[:, p * LANES:(p + 1) * LANES] = jnp.dot(
            p2, vz[p // (group // 2)], preferred_element_type=jnp.float32).astype(o_ref.dtype)


def _swa_bias_table():
    blk = SWA_BLK
    qi = lax.broadcasted_iota(jnp.int32, (blk, 2 * blk), 0)
    kj = lax.broadcasted_iota(jnp.int32, (blk, 2 * blk), 1)
    dist = qi + blk - kj
    window = (dist >= 0) & (dist < SWA_WINDOW)
    slopes = 2.0 ** (-8.0 * jnp.arange(1, SWA_HEADS + 1, dtype=jnp.float32) / SWA_HEADS)
    bias = -(LOG2E * slopes)[:, None, None] * dist.astype(jnp.float32)[None]
    later = jnp.where(window[None], bias, NEG_INF)
    first = jnp.where((window & (kj >= blk))[None], bias, NEG_INF)
    return jnp.stack([first, later])


def swa_attention(zb, sinks, ybuf, bsz, seq):
    blk = SWA_BLK
    nb = seq // blk
    kcol, vcol = ZB_AK // (2 * LANES), ZB_AV // (2 * LANES)

    def cur(b, n):
        return b * nb + n

    def prev(b, n):
        return b * nb + jnp.maximum(n - 1, 0)

    return pl.pallas_call(
        _swa_body,
        grid=(bsz, nb),
        in_specs=[pl.BlockSpec(memory_space=pltpu.SMEM),
                  pl.BlockSpec((None, SWA_HEADS, blk, 2 * blk),
                               lambda b, n: (jnp.minimum(n, 1), 0, 0, 0)),
                  pl.BlockSpec((blk, A_Q), lambda b, n: (cur(b, n), ZB_AQ // A_Q)),
                  pl.BlockSpec((blk, 2 * LANES), lambda b, n: (prev(b, n), kcol)),
                  pl.BlockSpec((blk, 2 * LANES), lambda b, n: (cur(b, n), kcol)),
                  pl.BlockSpec((blk, 2 * LANES), lambda b, n: (prev(b, n), vcol)),
                  pl.BlockSpec((blk, 2 * LANES), lambda b, n: (cur(b, n), vcol)),
                  pl.BlockSpec(memory_space=pl.ANY)],
        out_specs=pl.BlockSpec((None, blk, A_Q), lambda b, n: (0, cur(b, n), 0)),
        out_shape=jax.ShapeDtypeStruct(ybuf.shape, ybuf.dtype),
        input_output_aliases={7: 0},
        compiler_params=_cparams(("parallel", "arbitrary"), MIXER_VMEM),
        name="swa_attention",
    )(sinks, _swa_bias_table(), zb, zb, zb, zb, zb, ybuf)


def _mlstm_body(bias_ref, q_ref, k_ref, v_ref, og_ref, g_ref, ybuf_ref, y_ref, cum_ref,
                c_scr, n_scr, m_scr, carry_scr):
    del ybuf_ref
    c = pl.program_id(1)
    L = MLSTM_L
    dk, dv = MLSTM_QK_DIM, MLSTM_V_DIM

    @pl.when(c == 0)
    def _():
        c_scr[...] = jnp.zeros_like(c_scr)
        n_scr[...] = jnp.zeros_like(n_scr)
        m_scr[...] = jnp.zeros_like(m_scr)
        carry_scr[...] = jnp.zeros_like(carry_scr)

    nh = MLSTM_HEADS
    a = g_ref[...] + bias_ref[...]
    lf = jnp.minimum(a, 0.0) - jnp.log1p(jnp.exp(-jnp.abs(a)))
    row = lax.broadcasted_iota(jnp.int32, (L, L), 0)
    col = lax.broadcasted_iota(jnp.int32, (L, L), 1)
    causal = row >= col
    tri = causal.astype(jnp.float32)
    b_all = jnp.dot(tri, lf, preferred_element_type=jnp.float32,
                    precision=lax.Precision.HIGHEST)
    cum = b_all + carry_scr[0:1, :]
    carry_scr[0:1, :] = cum[L - 1:L, :]
    cum_ref[0] = cum.T

    b2 = pltpu.roll(b_all, LANES - nh, axis=1) * LOG2E
    g2 = a * LOG2E - b2
    rowi = lax.broadcasted_iota(jnp.int32, (L, LANES), 0)
    cm2 = g2
    d = 1
    while d < L:
        cm2 = jnp.maximum(cm2, jnp.where(rowi >= d, pltpu.roll(cm2, d, axis=0), NEG_INF))
        d *= 2
    m2_prev = m_scr[0:1, :]
    u2 = jnp.maximum(m2_prev, cm2)
    u2_last = u2[L - 1:L, :]
    w_inter_all = jnp.exp2(m2_prev - u2)
    floor_all = jnp.exp2(-(b2 + u2))
    wk_all = jnp.exp2(g2 - u2_last)
    decay_row = jnp.exp2(m2_prev - u2_last)
    m_scr[0:1, :] = b2[L - 1:L, :] + u2_last
    g2_t = g2.T

    lane = lax.broadcasted_iota(jnp.int32, (L, LANES), 1)
    lo = lane < dk
    crow = lax.broadcasted_iota(jnp.int32, (2 * dk, 2 * dv), 0)
    ccol = lax.broadcasted_iota(jnp.int32, (2 * dk, 2 * dv), 1)
    crow_lo = crow < dk
    blockdiag = crow_lo == (ccol < dv)
    nlane_lo = lax.broadcasted_iota(jnp.int32, (1, LANES), 1) < dk

    qps, kps, scs, cps, qcs = [], [], [], [], []
    for p in range(MLSTM_HEADS // 2):
        qp = q_ref[:, p * LANES:(p + 1) * LANES]
        kp = k_ref[:, p * LANES:(p + 1) * LANES] * (dk ** -0.5)
        kpb = kp.astype(jnp.bfloat16)
        zero = jnp.zeros_like(kpb)
        kz = jnp.concatenate([jnp.where(lo, kpb, zero), jnp.where(lo, zero, kpb)], axis=0)
        scs.append(lax.dot_general(qp, kz, (((1,), (1,)), ((), ())),
                                   preferred_element_type=jnp.float32))
        cp = c_scr[p]
        qcs.append(jnp.dot(qp, cp.astype(jnp.bfloat16), preferred_element_type=jnp.float32))
        qps.append(qp); kps.append(kp); cps.append(cp)

    for p in range(MLSTM_HEADS // 2):
        qp, kp, sc, cp, qc = qps[p], kps[p], scs[p], cps[p], qcs[p]
        n_row = n_scr[p:p + 1, :]
        qn_prod = qp.astype(jnp.float32) * n_row
        for e in range(2):
            h = 2 * p + e
            decay_mat = jnp.exp2(jnp.where(causal, g2_t[h:h + 1, :] - u2[:, h:h + 1], NEG_INF))
            smat = sc[:, e * L:(e + 1) * L] * decay_mat
            w_inter = w_inter_all[:, h:h + 1]
            qn = jnp.sum(jnp.where(lo == (e == 0), qn_prod, 0.0), axis=-1, keepdims=True)
            num = w_inter * qc[:, e * dv:(e + 1) * dv] + jnp.dot(
                smat.astype(jnp.bfloat16), v_ref[:, h * dv:(h + 1) * dv],
                preferred_element_type=jnp.float32)
            den = w_inter * qn + jnp.sum(smat, axis=-1, keepdims=True)
            rden = 1.0 / jnp.maximum(jnp.abs(den), floor_all[:, h:h + 1])
            og = og_ref[:, h * dv:(h + 1) * dv]
            y_ref[:, h * dv:(h + 1) * dv] = (num * rden * (1.0 / (1.0 + jnp.exp(-og)))).astype(y_ref.dtype)
        decays = [decay_row[:, 2 * p + e:2 * p + e + 1] for e in range(2)]
        kw = kp * jnp.where(lo, wk_all[:, 2 * p:2 * p + 1], wk_all[:, 2 * p + 1:2 * p + 2])
        n_scr[p:p + 1, :] = (jnp.where(nlane_lo, decays[0], decays[1]) * n_row
                             + jnp.sum(kw, axis=0, keepdims=True))
        upd = jnp.dot(kw.T.astype(jnp.bfloat16), v_ref[:, 2 * p * dv:(2 * p + 2) * dv],
                      preferred_element_type=jnp.float32)
        c_scr[p] = jnp.where(crow_lo, decays[0], decays[1]) * cp + jnp.where(blockdiag, upd, 0.0)


def mlstm(zb, zf, bias_row, ybuf, bsz, seq):
    L = MLSTM_L
    nc = seq // L

    def rows(b, c):
        return b * nc + c

    return pl.pallas_call(
        _mlstm_body,
        grid=(bsz, nc),
        in_specs=[pl.BlockSpec((1, LANES), lambda b, c: (0, 0)),
                  pl.BlockSpec((L, B_QK), lambda b, c: (rows(b, c), ZB_MQ // B_QK)),
                  pl.BlockSpec((L, B_QK), lambda b, c: (rows(b, c), ZF_MK // B_QK)),
                  pl.BlockSpec((L, B_V), lambda b, c: (rows(b, c), ZB_MV // B_V)),
                  pl.BlockSpec((L, B_V), lambda b, c: (rows(b, c), ZF_MO // B_V)),
                  pl.BlockSpec((L, LANES), lambda b, c: (rows(b, c), ZF_G // LANES)),
                  pl.BlockSpec(memory_space=pl.ANY)],
        out_specs=[pl.BlockSpec((None, L, B_V), lambda b, c: (1, rows(b, c), 0)),
                   pl.BlockSpec((1, LANES, L), lambda b, c: (b, 0, c))],
        out_shape=[jax.ShapeDtypeStruct(ybuf.shape, ybuf.dtype),
                   jax.ShapeDtypeStruct((bsz, LANES, seq), jnp.float32)],
        input_output_aliases={6: 0},
        scratch_shapes=[pltpu.VMEM((MLSTM_HEADS // 2, 2 * MLSTM_QK_DIM, 2 * MLSTM_V_DIM), jnp.float32),
                        pltpu.VMEM((8, LANES), jnp.float32),
                        pltpu.VMEM((8, LANES), jnp.float32),
                        pltpu.VMEM((8, LANES), jnp.float32)],
        compiler_params=_cparams(("arbitrary", "arbitrary"), MIXER_VMEM),
        name="mlstm",
    )(bias_row, zb, zf, zb, zf, zf, ybuf)


def _fox_body(q_ref, k_ref, v_ref, ck_ref, ybuf_ref, o_ref):
    del ybuf_ref
    hp = pl.program_id(1)
    qi = pl.program_id(2)
    blk = FOX_BLK
    dh = FOX_HEAD_DIM
    c2 = FOX_HEAD_DIM ** -0.5 * LOG2E

    def scores(j, e):
        start = pl.multiple_of(j * blk, blk)
        s = lax.dot_general(q_ref[:, e * dh:(e + 1) * dh], k_ref[pl.ds(start, blk), e * dh:(e + 1) * dh],
                            (((1,), (1,)), ((), ())), preferred_element_type=jnp.float32)
        return s * c2 - ck_ref[0, 2 * hp + e, pl.ds(j, 1), :] * LOG2E

    def update(s, j, e, m, l, acc):
        start = pl.multiple_of(j * blk, blk)
        m_new = jnp.maximum(m, jnp.max(s, axis=-1, keepdims=True))
        alpha = jnp.exp2(m - m_new)
        p = jnp.exp2(s - m_new)
        l = alpha * l + jnp.sum(p, axis=-1, keepdims=True)
        acc = alpha * acc + jnp.dot(p.astype(jnp.bfloat16),
                                    v_ref[pl.ds(start, blk), e * dh:(e + 1) * dh],
                                    preferred_element_type=jnp.float32)
        return m_new, l, acc

    def step(j, carry):
        s = [scores(j, e) for e in range(2)]
        return tuple(update(s[e], j, e, *carry[e]) for e in range(2))

    init1 = (jnp.full((blk, 1), NEG_INF, jnp.float32), jnp.zeros((blk, 1), jnp.float32),
             jnp.zeros((blk, dh), jnp.float32))
    carry = lax.fori_loop(0, qi, step, (init1, init1))
    row = lax.broadcasted_iota(jnp.int32, (blk, blk), 0)
    colm = lax.broadcasted_iota(jnp.int32, (blk, blk), 1)
    s = [jnp.where(colm <= row, scores(qi, e), NEG_INF) for e in range(2)]
    for e in range(2):
        _, l, acc = update(s[e], qi, e, *carry[e])
        o_ref[:, e * dh:(e + 1) * dh] = (acc / l).astype(o_ref.dtype)


def fox_attention(zb, cum_t, ybuf, bsz, seq):
    blk = FOX_BLK
    nq = seq // blk
    pw = 2 * FOX_HEAD_DIM
    ck = cum_t.reshape(bsz, LANES, nq, blk)
    return pl.pallas_call(
        _fox_body,
        grid=(bsz, FOX_HEADS // 2, nq),
        in_specs=[pl.BlockSpec((blk, pw), lambda b, h, i: (b * nq + i, ZB_CQ // pw + h)),
                  pl.BlockSpec((seq, pw), lambda b, h, i: (b, ZB_CK // pw + h)),
                  pl.BlockSpec((seq, pw), lambda b, h, i: (b, ZB_CV // pw + h)),
                  pl.BlockSpec((1, 8, nq, blk), lambda b, h, i: (b, 2, 0, 0)),
                  pl.BlockSpec(memory_space=pl.ANY)],
        out_specs=pl.BlockSpec((None, blk, pw), lambda b, h, i: (2, b * nq + i, h)),
        out_shape=jax.ShapeDtypeStruct(ybuf.shape, ybuf.dtype),
        input_output_aliases={4: 0},
        compiler_params=_cparams(("parallel", "arbitrary", "arbitrary"), MIXER_VMEM),
        name="fox_attention",
    )(zb, zb, zb, ck, ybuf)


RG_COPY, RG_DUP, RG_GATES = 0, 1, 2
_SEG = dict(a_q=0, a_k=1, a_v=2, m_q=3, m_k=4, m_v=5, m_i=6, m_f=7, m_o=8,
            c_q=9, c_k=10, c_v=11, c_f=12, g_a=13, g_b=14, g_c=15)


def _regroup_table():
    tiles = []

    def copy(name):
        off, width = SEG_OFFS[_SEG[name]], SEG_WIDTHS[_SEG[name]]
        assert width % WT == 0 and off % SUBLANES == 0
        tiles.extend((off + k * WT, RG_COPY) for k in range(width // WT))

    copy("m_o"); copy("m_k")
    m_i, m_f, c_f = (SEG_OFFS[_SEG[n]] for n in ("m_i", "m_f", "c_f"))
    assert m_f == m_i + MLSTM_HEADS and m_i % SUBLANES == 0 and c_f % SUBLANES == 0
    tiles.append((m_i, RG_GATES))
    assert len(tiles) * WT == WA_ZB
    for name in ("a_q", "m_v", "c_q", "c_k", "c_v", "m_q"):
        copy(name)
    a_k, a_v = SEG_OFFS[_SEG["a_k"]], SEG_OFFS[_SEG["a_v"]]
    assert a_v == a_k + A_KV and a_k % SUBLANES == 0 and 4 * A_KV == WT
    tiles.append((a_k, RG_DUP))
    assert len(tiles) * WT == WA_N
    return tiles, c_f


def _regroup_body(base_ref, mode_ref, w_ref, aux_ref, o_ref):
    del base_ref
    mode = mode_ref[pl.program_id(0)]
    bf = jnp.bfloat16
    hd = SWA_HEAD_DIM

    @pl.when(mode == RG_COPY)
    def _():
        o_ref[...] = w_ref[0].astype(bf)

    @pl.when(mode == RG_DUP)
    def _():
        for i in range(2 * SWA_KV_HEADS):
            head = w_ref[0, i * hd:(i + 1) * hd, :].astype(bf)
            o_ref[2 * i * hd:(2 * i + 1) * hd, :] = head
            o_ref[(2 * i + 1) * hd:(2 * i + 2) * hd, :] = head

    @pl.when(mode == RG_GATES)
    def _():
        nh = MLSTM_HEADS
        o_ref[0:2 * nh, :] = w_ref[0, 0:2 * nh, :].astype(bf)
        o_ref[2 * nh:3 * nh, :] = aux_ref[0].astype(bf)
        o_ref[3 * nh:WT, :] = jnp.zeros((WT - 3 * nh, o_ref.shape[1]), bf)


def regroup_w_in(wt, l):
    tiles, c_f = _regroup_table()
    base = jnp.asarray([b for b, _ in tiles], jnp.int32)
    mode = jnp.asarray([m for _, m in tiles], jnp.int32)
    k = wt.shape[2]
    el = pl.Element
    in_specs = [pl.BlockSpec((el(1), el(WT), el(k)),
                             lambda t, base, mode: (l, pl.multiple_of(base[t], SUBLANES), 0)),
                pl.BlockSpec((el(1), el(FOX_HEADS), el(k)), lambda t, base, mode: (l, c_f, 0))]
    vmem = 2 * (WT * k * 4 + WT * k * 2) + 3 * VMEM_SLACK
    return pl.pallas_call(
        _regroup_body,
        grid_spec=pltpu.PrefetchScalarGridSpec(
            num_scalar_prefetch=2, grid=(len(tiles),), in_specs=in_specs,
            out_specs=pl.BlockSpec((WT, k), lambda t, base, mode: (t, 0))),
        out_shape=jax.ShapeDtypeStruct((WA_N, k), jnp.bfloat16),
        compiler_params=_cparams(("arbitrary",), vmem),
        name="regroup_w_in",
    )(base, mode, wt, wt)


def _cast_body(w_ref, o_ref):
    o_ref[...] = w_ref[...].astype(o_ref.dtype)


def cast_bf16(w, l=None, *, br=1024, bc=2048):
    r, c = w.shape[-2:]
    br, bc = min(br, r), min(bc, c)
    assert r % br == 0 and c % bc == 0
    if l is None:
        in_spec = pl.BlockSpec((br, bc), lambda i, j: (i, j))
    else:
        in_spec = pl.BlockSpec((None, br, bc), lambda i, j: (l, i, j))
    return pl.pallas_call(
        _cast_body,
        grid=(r // br, c // bc),
        in_specs=[in_spec],
        out_specs=pl.BlockSpec((br, bc), lambda i, j: (i, j)),
        out_shape=jax.ShapeDtypeStruct((r, c), jnp.bfloat16),
        compiler_params=_cparams(("parallel", "parallel"), 2 * br * bc * 6 + 2 * VMEM_SLACK),
        name="cast_bf16",
    )(w)


def kernel(x, w_in, b_mlstm_i, b_mlstm_f, b_fox_f, attn_sinks, w_up_swa, w_up_mlstm, w_up_fox,
           w_o, ln1_g, ln1_b, w_ff1, w_ff2, ln2_g, ln2_b):
    bsz, seq, d = x.shape
    t = bsz * seq
    assert d == D_MODEL and seq % FOX_BLK == 0 and seq % MLSTM_L == 0
    xf = x.reshape(t, d).astype(jnp.float32)
    xb = cast_bf16(xf)
    gate_pad = jnp.zeros((LANES - 3 * MLSTM_HEADS,), jnp.float32)
    ups = (w_up_swa, w_up_mlstm, w_up_fox)
    w_ups = [cast_bf16(w, 0) for w in ups]
    w_ob = None
    w_int = jnp.swapaxes(w_in, 1, 2)
    for l in range(DEPTH):
        w_all = regroup_w_in(w_int, l)
        bias_row = jnp.concatenate([b_mlstm_i[l], b_mlstm_f[l], b_fox_f[l], gate_pad]).reshape(1, LANES)
        bm, bn = TILES["in_proj_f32"]
        (zf,) = matmul(xb, w_all, n=ZF_N, w_row=WA_ZF, bm=bm, bn=bn, out_dtype=jnp.float32,
                       name="in_proj_f32")
        gates = Side(w_int, l, SIDE_ROWS["gates"], row_off=SEG_OFFS[_SEG["g_a"]],
                     n_rows=N_BRANCH * D_MODEL)
        bm, bn = TILES["in_proj_bf16"]
        zb, w_g = matmul(xb, w_all, n=ZB_N, w_row=WA_ZB, bm=bm, bn=bn, out_dtype=jnp.bfloat16,
                         name="in_proj_bf16", sides=[gates])
        ybuf = jnp.zeros((3, t, A_Q), jnp.bfloat16)
        ybuf = swa_attention(zb, attn_sinks[l].astype(jnp.float32), ybuf, bsz, seq)
        ybuf, cum_t = mlstm(zb, zf, bias_row.astype(jnp.float32), ybuf, bsz, seq)
        ybuf = fox_attention(zb, cum_t, ybuf, bsz, seq)
        bm, bn = TILES["gated_merge"]
        merge_sides = [Side(w_ff1, l, SIDE_ROWS["w_ff1"])]
        if w_ob is None:
            merge_sides.append(Side(w_o, l, SIDE_ROWS["next_layer"]))
        mix, w_1b, *w_ob_cast = gated_merge(xb, w_g, ybuf, w_ups, bm=bm, bn=bn, sides=merge_sides)
        if w_ob_cast:
            (w_ob,) = w_ob_cast
        bm, bn = TILES["out_proj"]
        (r1,) = matmul_residual(mix, w_ob, xf, bm=bm, bn=bn, bk=D_MODEL, name="out_proj")
        xf, xb = layer_norm(r1, ln1_g[l], ln1_b[l], emit_f32=False)
        bm, bn = TILES["ff1"]
        hid, w_2b = matmul(xb, w_1b, bm=bm, bn=bn, out_dtype=jnp.bfloat16, act="relu2",
                           name="ff1", sides=[Side(w_ff2, l, SIDE_ROWS["w_ff2"])])
        nxt = [Side(w, l + 1, SIDE_ROWS["next_layer"]) for w in (w_o,) + ups] if l + 1 < DEPTH else []
        bm, bn, bk = TILES["ff2"]
        r2, *cast_next = matmul_residual(hid, w_2b, xf, bm=bm, bn=bn, bk=bk, name="ff2", sides=nxt)
        if cast_next:
            w_ob, *w_ups = cast_next
        xf, xb = layer_norm(r2, ln2_g[l], ln2_b[l], emit_f32=(l + 1 == DEPTH))
    return xf.reshape(bsz, seq, d).astype(x.dtype)
```

```python
import functools
from typing import NamedTuple

import jax
import jax.numpy as jnp
from jax import lax
from jax.experimental import pallas as pl
from jax.experimental.pallas import tpu as pltpu

D_MODEL = 4096
SWA_HEADS, SWA_KV_HEADS, SWA_HEAD_DIM, SWA_WINDOW = 16, 2, 64, 128
MLSTM_HEADS, MLSTM_QK_DIM, MLSTM_V_DIM = 8, 64, 128
FOX_HEADS, FOX_HEAD_DIM = 8, 128
LN_EPS = 1e-5
DEPTH = 2
DN_ALPHA = (2 * DEPTH) ** 0.25

A_Q = SWA_HEADS * SWA_HEAD_DIM
A_KV = SWA_KV_HEADS * SWA_HEAD_DIM
B_QK = MLSTM_HEADS * MLSTM_QK_DIM
B_V = MLSTM_HEADS * MLSTM_V_DIM
C_W = FOX_HEADS * FOX_HEAD_DIM
SEG_WIDTHS = (A_Q, A_KV, A_KV, B_QK, B_QK, B_V, MLSTM_HEADS, MLSTM_HEADS, B_V,
              C_W, C_W, C_W, FOX_HEADS, D_MODEL, D_MODEL, D_MODEL)
SEG_OFFS = tuple(sum(SEG_WIDTHS[:i]) for i in range(len(SEG_WIDTHS)))

LANES = 128
SUBLANES = 8
VMEM_LIMIT_CAP = 56 * 1024 * 1024
VMEM_SLACK = 4 * 1024 * 1024
MIXER_VMEM = 32 * 1024 * 1024

ZB_AQ = 0
ZB_MV = 1024
ZB_CQ = 2048
ZB_CK = 3072
ZB_CV = 4096
ZB_MQ = 5120
ZB_AK = 5632
ZB_AV = 5888
ZB_N = 6144
ZF_MO = 0
ZF_MK = 1024
ZF_G = 1536
ZF_N = 1664
WT = 512
WA_ZF = 0
WA_ZB = 2048
WA_N = WA_ZB + ZB_N

TILES = dict(in_proj_f32=(1024, ZF_N), in_proj_bf16=(1024, 1024), gated_merge=(1024, 256),
             out_proj=(1024, 512), ff1=(1024, 1024), ff2=(1024, 1024, 2048))
LN_ROWS = 256
SIDE_ROWS = dict(gates=128, w_ff1=32, w_ff2=128, next_layer=16)

MLSTM_L = 256
FOX_BLK = 512
SWA_BLK = 128
NEG_INF = float("-inf")
LOG2E = 1.4426950408889634


def _cparams(sem, vmem_bytes):
    return pltpu.CompilerParams(dimension_semantics=sem,
                                vmem_limit_bytes=int(min(vmem_bytes, VMEM_LIMIT_CAP)))


_NT = (((1,), (1,)), ((), ()))


class Side(NamedTuple):
    src: jax.Array
    layer: int
    rows: int
    row_off: int = 0
    n_rows: int = 0


def _side_plumbing(sides, grid):
    steps = 1
    for n in grid:
        steps *= n

    def lin(*g):
        s = g[0]
        for a, n in zip(g[1:], grid[1:]):
            s = s * n + a
        return s

    in_specs, out_specs, out_shapes, vmem = [], [], [], 0
    for sd in sides:
        c = sd.src.shape[-1]
        r = sd.n_rows or sd.src.shape[-2]
        rows = sd.rows
        while r // rows > steps:
            rows *= 2
        assert r % rows == 0 and sd.row_off % SUBLANES == 0
        last = r // rows - 1

        def blk(*g, last=last):
            return jnp.minimum(lin(*g), last)

        if sd.row_off or sd.n_rows:
            el = pl.Element
            in_specs.append(pl.BlockSpec(
                (el(1), el(rows), el(c)),
                lambda *g, blk=blk, sd=sd, rows=rows: (
                    sd.layer, pl.multiple_of(sd.row_off + blk(*g) * rows, SUBLANES), 0)))
        else:
            in_specs.append(pl.BlockSpec((None, rows, c),
                                         lambda *g, blk=blk, l=sd.layer: (l, blk(*g), 0)))
        out_specs.append(pl.BlockSpec((rows, c), lambda *g, blk=blk: (blk(*g), 0)))
        out_shapes.append(jax.ShapeDtypeStruct((r, c), jnp.bfloat16))
        vmem += 2 * rows * c * (4 + 2)
    return in_specs, out_specs, out_shapes, vmem


def _cast_sides(side_in, side_out):
    for si, so in zip(side_in, side_out):
        so[...] = si[...].reshape(so.shape).astype(so.dtype)


def _mm_body(x_ref, w_ref, *rest, act, w_t, n_side):
    side_in, o_ref, side_out = rest[:n_side], rest[n_side], rest[n_side + 1:]
    _cast_sides(side_in, side_out)
    if w_t:
        acc = lax.dot_general(x_ref[...], w_ref[...], _NT, preferred_element_type=jnp.float32)
    else:
        acc = jnp.dot(x_ref[...], w_ref[...], preferred_element_type=jnp.float32)
    if act == "relu2":
        acc = jnp.square(jnp.maximum(acc, 0.0))
    o_ref[...] = acc.astype(o_ref.dtype)


def _residual(res_refs):
    if len(res_refs) == 1:
        return res_refs[0][...]
    return _ln_apply(*(ref[...] for ref in res_refs))


def _mm_res_body(x_ref, w_ref, *rest):
    res_refs, o_ref = rest[:-1], rest[-1]
    acc = jnp.dot(x_ref[...], w_ref[...], preferred_element_type=jnp.float32)
    o_ref[...] = DN_ALPHA * _residual(res_refs) + acc


def _mm_res_k_body(x_ref, w_ref, *rest, n_res, n_side):
    res_refs, rest = rest[:n_res], rest[n_res:]
    side_in, o_ref, side_out = rest[:n_side], rest[n_side], rest[n_side + 1:]
    k = pl.program_id(2)

    @pl.when(k == 0)
    def _():
        _cast_sides(side_in, side_out)
        o_ref[...] = DN_ALPHA * _residual(res_refs) + jnp.dot(x_ref[...], w_ref[...],
                                                              preferred_element_type=jnp.float32)

    @pl.when(k > 0)
    def _():
        _cast_sides(side_in, side_out)
        o_ref[...] += jnp.dot(x_ref[...], w_ref[...], preferred_element_type=jnp.float32)


def matmul(x, w, *, bm, bn, out_dtype, act=None, name, n=None, w_row=None, sides=()):
    m, k = x.shape
    w_t = w_row is not None
    n = w.shape[1] if n is None else n
    bm, bn = min(bm, m), min(bn, n)
    assert m % bm == 0 and n % bn == 0
    osz = jnp.dtype(out_dtype).itemsize
    grid = (m // bm, n // bn)
    s_in, s_out, s_shapes, s_vmem = _side_plumbing(sides, grid)
    w_bufs, w_mode = (1, dict(pipeline_mode=pl.Buffered(1))) if grid[1] == 1 else (2, {})
    vmem = 2 * (bm * k * 2 + bm * bn * osz) + w_bufs * k * bn * 2 + bm * bn * 4 + VMEM_SLACK + s_vmem
    if w_t:
        assert w_row % bn == 0
        off = w_row // bn
        w_spec = pl.BlockSpec((bn, k), lambda i, j: (off + j, 0), **w_mode)
    else:
        w_spec = pl.BlockSpec((k, bn), lambda i, j: (0, j), **w_mode)
    return pl.pallas_call(
        functools.partial(_mm_body, act=act, w_t=w_t, n_side=len(sides)),
        grid=grid,
        in_specs=[pl.BlockSpec((bm, k), lambda i, j: (i, 0)), w_spec] + s_in,
        out_specs=[pl.BlockSpec((bm, bn), lambda i, j: (i, j))] + s_out,
        out_shape=[jax.ShapeDtypeStruct((m, n), out_dtype)] + s_shapes,
        compiler_params=_cparams(("arbitrary", "arbitrary"), vmem),
        name=name,
    )(x, w, *[sd.src for sd in sides])


def matmul_residual(x, w, res, *, bm, bn, bk, name, sides=()):
    m, k = x.shape
    n = w.shape[1]
    bm, bn, bk = min(bm, m), min(bn, n), min(bk, k)
    assert m % bm == 0 and n % bn == 0 and k % bk == 0
    res_args = tuple(res) if isinstance(res, LnRes) else (res,)
    res_dims = [(bm, bn, True, True), (bm, 1, True, False), (bm, 1, True, False),
                (1, bn, False, True), (1, bn, False, True)][:len(res_args)]

    def res_specs(ij):
        return [pl.BlockSpec((r, c), lambda *g, ri=ri, ci=ci: (ij(*g)[0] if ri else 0, ij(*g)[1] if ci else 0))
                for r, c, ri, ci in res_dims]

    if bk == k:
        assert not sides
        vmem = 2 * (bm * k * 2 + k * bn * 2 + 2 * bm * bn * 4) + bm * bn * 4 + VMEM_SLACK
        return [pl.pallas_call(
            _mm_res_body,
            grid=(m // bm, n // bn),
            in_specs=[pl.BlockSpec((bm, k), lambda i, j: (i, 0)),
                      pl.BlockSpec((k, bn), lambda i, j: (0, j))] + res_specs(lambda i, j: (i, j)),
            out_specs=pl.BlockSpec((bm, bn), lambda i, j: (i, j)),
            out_shape=jax.ShapeDtypeStruct((m, n), jnp.float32),
            compiler_params=_cparams(("parallel", "arbitrary"), vmem),
            name=name,
        )(x, w, *res_args)]
    grid = (m // bm, n // bn, k // bk)
    s_in, s_out, s_shapes, s_vmem = _side_plumbing(sides, grid)
    vmem = 2 * (bm * bk * 2 + bk * bn * 2 + 2 * bm * bn * 4) + 2 * bm * bn * 4 + VMEM_SLACK + s_vmem
    return pl.pallas_call(
        functools.partial(_mm_res_k_body, n_res=len(res_args), n_side=len(sides)),
        grid=grid,
        in_specs=[pl.BlockSpec((bm, bk), lambda i, j, kk: (i, kk)),
                  pl.BlockSpec((bk, bn), lambda i, j, kk: (kk, j))]
        + res_specs(lambda i, j, kk: (i, j)) + s_in,
        out_specs=[pl.BlockSpec((bm, bn), lambda i, j, kk: (i, j))] + s_out,
        out_shape=[jax.ShapeDtypeStruct((m, n), jnp.float32)] + s_shapes,
        compiler_params=_cparams(("arbitrary", "arbitrary", "arbitrary"), vmem),
        name=name,
    )(x, w, *res_args, *[sd.src for sd in sides])


N_BRANCH = 3


def _merge_body(x_ref, wg0_ref, wg1_ref, wg2_ref, y_ref, wu0_ref, wu1_ref, wu2_ref, *rest, n_side):
    side_in, o_ref, side_out = rest[:n_side], rest[n_side], rest[n_side + 1:]
    _cast_sides(side_in, side_out)
    mix = None
    for br, (wg_ref, wu_ref) in enumerate(((wg0_ref, wu0_ref), (wg1_ref, wu1_ref), (wg2_ref, wu2_ref))):
        g = lax.dot_general(x_ref[...], wg_ref[...], _NT, preferred_element_type=jnp.float32)
        u = jnp.dot(y_ref[br], wu_ref[...], preferred_element_type=jnp.float32)
        c = u * (1.0 / (1.0 + jnp.exp(-g)))
        mix = c if mix is None else mix + c
    o_ref[...] = mix.astype(o_ref.dtype)


def gated_merge(xb, wg, y, wus, *, bm, bn, sides=()):
    t, d = xb.shape
    nbr, _, kin = y.shape
    assert nbr == N_BRANCH == len(wus) and wg.shape == (nbr * d, d)
    bm, bn = min(bm, t), min(bn, d)
    assert d % bn == 0
    gstride = d // bn
    grid = (t // bm, d // bn)
    s_in, s_out, s_shapes, s_vmem = _side_plumbing(sides, grid)
    vmem = bm * d * 2 + 2 * (nbr * (d * bn * 2 + bm * kin * 2 + kin * bn * 2) + bm * bn * 2) \
        + 2 * nbr * bm * bn * 4 + 2 * VMEM_SLACK + s_vmem
    wg_specs = [pl.BlockSpec((bn, d), lambda i, j, b=b: (b * gstride + j, 0)) for b in range(nbr)]
    wu_spec = pl.BlockSpec((kin, bn), lambda i, j: (0, j))
    return pl.pallas_call(
        functools.partial(_merge_body, n_side=len(sides)),
        grid=grid,
        in_specs=[pl.BlockSpec((bm, d), lambda i, j: (i, 0), pipeline_mode=pl.Buffered(1))] + wg_specs
        + [pl.BlockSpec((nbr, bm, kin), lambda i, j: (0, i, 0)), wu_spec, wu_spec, wu_spec] + s_in,
        out_specs=[pl.BlockSpec((bm, bn), lambda i, j: (i, j))] + s_out,
        out_shape=[jax.ShapeDtypeStruct((t, d), jnp.bfloat16)] + s_shapes,
        compiler_params=_cparams(("arbitrary", "arbitrary"), vmem),
        name="gated_merge",
    )(xb, wg, wg, wg, y, *wus, *[sd.src for sd in sides])


class LnRes(NamedTuple):
    r: jax.Array
    mu: jax.Array
    rstd: jax.Array
    g: jax.Array
    b: jax.Array


def _ln_apply(r, mu, rstd, g, b):
    return (r - mu) * rstd * g + b


LN_GROUP = 16
LN_UNROLL = 8


def _ln_body(r_ref, g_ref, b_ref, *outs, emit_f32):
    g, b = g_ref[...], b_ref[...]

    def rows(i, carry):
        rs = pl.ds(pl.multiple_of(i * LN_GROUP, LN_GROUP), LN_GROUP)
        r = r_ref[rs, :]
        mu = jnp.mean(r, axis=-1, keepdims=True)
        xc = r - mu
        rstd = lax.rsqrt(jnp.mean(xc * xc, axis=-1, keepdims=True) + LN_EPS)
        y = _ln_apply(r, mu, rstd, g, b)
        if emit_f32:
            (of_ref,) = outs
            of_ref[rs, :] = y
        else:
            ob_ref, mu_ref, rstd_ref = outs
            mu_ref[rs, :] = mu
            rstd_ref[rs, :] = rstd
            ob_ref[rs, :] = y.astype(jnp.bfloat16)
        return carry

    n_groups = r_ref.shape[0] // LN_GROUP
    lax.fori_loop(0, n_groups, rows, 0, unroll=min(LN_UNROLL, n_groups))


def layer_norm(r, g, b, *, emit_f32, bm=LN_ROWS):
    t, d = r.shape
    bm = min(bm, t)
    g2, b2 = g.reshape(1, d), b.reshape(1, d)
    row = pl.BlockSpec((bm, d), lambda i: (i, 0))
    stat = pl.BlockSpec((bm, 1), lambda i: (i, 0))
    vmem = 2 * (bm * d * 4 * 2 + bm * d * 2) + 4 * bm * d * 4 + VMEM_SLACK
    if emit_f32:
        out_specs = [row]
        out_shape = [jax.ShapeDtypeStruct((t, d), jnp.float32)]
    else:
        out_specs = [row, stat, stat]
        out_shape = [jax.ShapeDtypeStruct((t, d), jnp.bfloat16),
                     jax.ShapeDtypeStruct((t, 1), jnp.float32), jax.ShapeDtypeStruct((t, 1), jnp.float32)]
    outs = pl.pallas_call(
        functools.partial(_ln_body, emit_f32=emit_f32),
        grid=(t // bm,),
        in_specs=[row, pl.BlockSpec((1, d), lambda i: (0, 0)), pl.BlockSpec((1, d), lambda i: (0, 0))],
        out_specs=out_specs,
        out_shape=out_shape,
        compiler_params=_cparams(("parallel",), vmem),
        name="layer_norm",
    )(r, g2, b2)
    if emit_f32:
        return outs[0], None
    xb, mu, rstd = outs
    return LnRes(r, mu, rstd, g2, b2), xb


def _swa_body(sink_ref, bias_ref, q_ref, kp_ref, kc_ref, vp_ref, vc_ref, ybuf_ref, o_ref):
    del ybuf_ref
    blk = SWA_BLK
    c2 = SWA_HEAD_DIM ** -0.5 * LOG2E
    lo = lax.broadcasted_iota(jnp.int32, (2 * blk, LANES), 1) < SWA_HEAD_DIM
    group = SWA_HEADS // SWA_KV_HEADS
    pairs = SWA_HEADS // 2
    kz, vz = [], []
    for g in range(SWA_KV_HEADS):
        kd = jnp.concatenate([kp_ref[:, g * LANES:(g + 1) * LANES],
                              kc_ref[:, g * LANES:(g + 1) * LANES]], axis=0)
        vd = jnp.concatenate([vp_ref[:, g * LANES:(g + 1) * LANES],
                              vc_ref[:, g * LANES:(g + 1) * LANES]], axis=0)
        zero = jnp.zeros_like(kd)
        kz.append(jnp.concatenate([jnp.where(lo, kd, zero), jnp.where(lo, zero, kd)], axis=0))
        vz.append(jnp.concatenate([jnp.where(lo, vd, zero), jnp.where(lo, zero, vd)], axis=0))
    s2 = [lax.dot_general(q_ref[:, p * LANES:(p + 1) * LANES], kz[p // (group // 2)],
                          (((1,), (1,)), ((), ())), preferred_element_type=jnp.float32)
          for p in range(pairs)]
    for p in range(pairs):
        probs = []
        for e in range(2):
            h = 2 * p + e
            sink = sink_ref[h] * LOG2E
            s = s2[p][:, e * 2 * blk:(e + 1) * 2 * blk] * c2 + bias_ref[h]
            m = jnp.maximum(jnp.max(s, axis=-1, keepdims=True), sink)
            pe = jnp.exp2(s - m)
            den = jnp.sum(pe, axis=-1, keepdims=True) + jnp.exp2(sink - m)
            probs.append((pe * (1.0 / den)).astype(jnp.bfloat16))
        p2 = jnp.concatenate(probs, axis=1)
        o_ref[:, p * LANES:(p + 1) * LANES] = jnp.dot(
            p2, vz[p // (group // 2)], preferred_element_type=jnp.float32).astype(o_ref.dtype)


def _swa_bias_table():
    blk = SWA_BLK
    qi = lax.broadcasted_iota(jnp.int32, (blk, 2 * blk), 0)
    kj = lax.broadcasted_iota(jnp.int32, (blk, 2 * blk), 1)
    dist = qi + blk - kj
    window = (dist >= 0) & (dist < SWA_WINDOW)
    slopes = 2.0 ** (-8.0 * jnp.arange(1, SWA_HEADS + 1, dtype=jnp.float32) / SWA_HEADS)
    bias = -(LOG2E * slopes)[:, None, None] * dist.astype(jnp.float32)[None]
    later = jnp.where(window[None], bias, NEG_INF)
    first = jnp.where((window & (kj >= blk))[None], bias, NEG_INF)
    return jnp.stack([first, later])


def swa_attention(zb, sinks, ybuf, bsz, seq):
    blk = SWA_BLK
    nb = seq // blk
    kcol, vcol = ZB_AK // (2 * LANES), ZB_AV // (2 * LANES)

    def cur(b, n):
        return b * nb + n

    def prev(b, n):
        return b * nb + jnp.maximum(n - 1, 0)

    return pl.pallas_call(
        _swa_body,
        grid=(bsz, nb),
        in_specs=[pl.BlockSpec(memory_space=pltpu.SMEM),
                  pl.BlockSpec((None, SWA_HEADS, blk, 2 * blk),
                               lambda b, n: (jnp.minimum(n, 1), 0, 0, 0)),
                  pl.BlockSpec((blk, A_Q), lambda b, n: (cur(b, n), ZB_AQ // A_Q)),
                  pl.BlockSpec((blk, 2 * LANES), lambda b, n: (prev(b, n), kcol)),
                  pl.BlockSpec((blk, 2 * LANES), lambda b, n: (cur(b, n), kcol)),
                  pl.BlockSpec((blk, 2 * LANES), lambda b, n: (prev(b, n), vcol)),
                  pl.BlockSpec((blk, 2 * LANES), lambda b, n: (cur(b, n), vcol)),
                  pl.BlockSpec(memory_space=pl.ANY)],
        out_specs=pl.BlockSpec((None, blk, A_Q), lambda b, n: (0, cur(b, n), 0)),
        out_shape=jax.ShapeDtypeStruct(ybuf.shape, ybuf.dtype),
        input_output_aliases={7: 0},
        compiler_params=_cparams(("parallel", "arbitrary"), MIXER_VMEM),
        name="swa_attention",
    )(sinks, _swa_bias_table(), zb, zb, zb, zb, zb, ybuf)


def _mlstm_body(bias_ref, q_ref, k_ref, v_ref, og_ref, g_ref, ybuf_ref, y_ref, cum_ref,
                c_scr, n_scr, m_scr, carry_scr):
    del ybuf_ref
    c = pl.program_id(1)
    L = MLSTM_L
    dk, dv = MLSTM_QK_DIM, MLSTM_V_DIM

    @pl.when(c == 0)
    def _():
        c_scr[...] = jnp.zeros_like(c_scr)
        n_scr[...] = jnp.zeros_like(n_scr)
        m_scr[...] = jnp.zeros_like(m_scr)
        carry_scr[...] = jnp.zeros_like(carry_scr)

    nh = MLSTM_HEADS
    a = g_ref[...] + bias_ref[...]
    lf = jnp.minimum(a, 0.0) - jnp.log1p(jnp.exp(-jnp.abs(a)))
    row = lax.broadcasted_iota(jnp.int32, (L, L), 0)
    col = lax.broadcasted_iota(jnp.int32, (L, L), 1)
    causal = row >= col
    tri = causal.astype(jnp.float32)
    b_all = jnp.dot(tri, lf, preferred_element_type=jnp.float32,
                    precision=lax.Precision.HIGHEST)
    cum = b_all + carry_scr[0:1, :]
    carry_scr[0:1, :] = cum[L - 1:L, :]
    cum_ref[0] = cum.T

    b2 = pltpu.roll(b_all, LANES - nh, axis=1) * LOG2E
    g2 = a * LOG2E - b2
    rowi = lax.broadcasted_iota(jnp.int32, (L, LANES), 0)
    cm2 = g2
    d = 1
    while d < L:
        cm2 = jnp.maximum(cm2, jnp.where(rowi >= d, pltpu.roll(cm2, d, axis=0), NEG_INF))
        d *= 2
    m2_prev = m_scr[0:1, :]
    u2 = jnp.maximum(m2_prev, cm2)
    u2_last = u2[L - 1:L, :]
    w_inter_all = jnp.exp2(m2_prev - u2)
    floor_all = jnp.exp2(-(b2 + u2))
    wk_all = jnp.exp2(g2 - u2_last)
    decay_row = jnp.exp2(m2_prev - u2_last)
    m_scr[0:1, :] = b2[L - 1:L, :] + u2_last
    g2_t = g2.T

    lane = lax.broadcasted_iota(jnp.int32, (L, LANES), 1)
    lo = lane < dk
    crow = lax.broadcasted_iota(jnp.int32, (2 * dk, 2 * dv), 0)
    ccol = lax.broadcasted_iota(jnp.int32, (2 * dk, 2 * dv), 1)
    crow_lo = crow < dk
    blockdiag = crow_lo == (ccol < dv)
    nlane_lo = lax.broadcasted_iota(jnp.int32, (1, LANES), 1) < dk

    qps, kps, scs, cps, qcs = [], [], [], [], []
    for p in range(MLSTM_HEADS // 2):
        qp = q_ref[:, p * LANES:(p + 1) * LANES]
        kp = k_ref[:, p * LANES:(p + 1) * LANES] * (dk ** -0.5)
        kpb = kp.astype(jnp.bfloat16)
        zero = jnp.zeros_like(kpb)
        kz = jnp.concatenate([jnp.where(lo, kpb, zero), jnp.where(lo, zero, kpb)], axis=0)
        scs.append(lax.dot_general(qp, kz, (((1,), (1,)), ((), ())),
                                   preferred_element_type=jnp.float32))
        cp = c_scr[p]
        qcs.append(jnp.dot(qp, cp.astype(jnp.bfloat16), preferred_element_type=jnp.float32))
        qps.append(qp); kps.append(kp); cps.append(cp)

    for p in range(MLSTM_HEADS // 2):
        qp, kp, sc, cp, qc = qps[p], kps[p], scs[p], cps[p], qcs[p]
        n_row = n_scr[p:p + 1, :]
        qn_prod = qp.astype(jnp.float32) * n_row
        for e in range(2):
            h = 2 * p + e
            decay_mat = jnp.exp2(jnp.where(causal, g2_t[h:h + 1, :] - u2[:, h:h + 1], NEG_INF))
            smat = sc[:, e * L:(e + 1) * L] * decay_mat
            w_inter = w_inter_all[:, h:h + 1]
            qn = jnp.sum(jnp.where(lo == (e == 0), qn_prod, 0.0), axis=-1, keepdims=True)
            num = w_inter * qc[:, e * dv:(e + 1) * dv] + jnp.dot(
                smat.astype(jnp.bfloat16), v_ref[:, h * dv:(h + 1) * dv],
                preferred_element_type=jnp.float32)
            den = w_inter * qn + jnp.sum(smat, axis=-1, keepdims=True)
            rden = 1.0 / jnp.maximum(jnp.abs(den), floor_all[:, h:h + 1])
            og = og_ref[:, h * dv:(h + 1) * dv]
            y_ref[:, h * dv:(h + 1) * dv] = (num * rden * (1.0 / (1.0 + jnp.exp(-og)))).astype(y_ref.dtype)
        decays = [decay_row[:, 2 * p + e:2 * p + e + 1] for e in range(2)]
        kw = kp * jnp.where(lo, wk_all[:, 2 * p:2 * p + 1], wk_all[:, 2 * p + 1:2 * p + 2])
        n_scr[p:p + 1, :] = (jnp.where(nlane_lo, decays[0], decays[1]) * n_row
                             + jnp.sum(kw, axis=0, keepdims=True))
        upd = jnp.dot(kw.T.astype(jnp.bfloat16), v_ref[:, 2 * p * dv:(2 * p + 2) * dv],
                      preferred_element_type=jnp.float32)
        c_scr[p] = jnp.where(crow_lo, decays[0], decays[1]) * cp + jnp.where(blockdiag, upd, 0.0)


def mlstm(zb, zf, bias_row, ybuf, bsz, seq):
    L = MLSTM_L
    nc = seq // L

    def rows(b, c):
        return b * nc + c

    return pl.pallas_call(
        _mlstm_body,
        grid=(bsz, nc),
        in_specs=[pl.BlockSpec((1, LANES), lambda b, c: (0, 0)),
                  pl.BlockSpec((L, B_QK), lambda b, c: (rows(b, c), ZB_MQ // B_QK)),
                  pl.BlockSpec((L, B_QK), lambda b, c: (rows(b, c), ZF_MK // B_QK)),
                  pl.BlockSpec((L, B_V), lambda b, c: (rows(b, c), ZB_MV // B_V)),
                  pl.BlockSpec((L, B_V), lambda b, c: (rows(b, c), ZF_MO // B_V)),
                  pl.BlockSpec((L, LANES), lambda b, c: (rows(b, c), ZF_G // LANES)),
                  pl.BlockSpec(memory_space=pl.ANY)],
        out_specs=[pl.BlockSpec((None, L, B_V), lambda b, c: (1, rows(b, c), 0)),
                   pl.BlockSpec((1, LANES, L), lambda b, c: (b, 0, c))],
        out_shape=[jax.ShapeDtypeStruct(ybuf.shape, ybuf.dtype),
                   jax.ShapeDtypeStruct((bsz, LANES, seq), jnp.float32)],
        input_output_aliases={6: 0},
        scratch_shapes=[pltpu.VMEM((MLSTM_HEADS // 2, 2 * MLSTM_QK_DIM, 2 * MLSTM_V_DIM), jnp.float32),
                        pltpu.VMEM((8, LANES), jnp.float32),
                        pltpu.VMEM((8, LANES), jnp.float32),
                        pltpu.VMEM((8, LANES), jnp.float32)],
        compiler_params=_cparams(("arbitrary", "arbitrary"), MIXER_VMEM),
        name="mlstm",
    )(bias_row, zb, zf, zb, zf, zf, ybuf)


def _fox_body(q_ref, k_ref, v_ref, ck_ref, ybuf_ref, o_ref):
    del ybuf_ref
    hp = pl.program_id(1)
    qi = pl.program_id(2)
    blk = FOX_BLK
    dh = FOX_HEAD_DIM
    c2 = FOX_HEAD_DIM ** -0.5 * LOG2E

    def scores(j, e):
        start = pl.multiple_of(j * blk, blk)
        s = lax.dot_general(q_ref[:, e * dh:(e + 1) * dh], k_ref[pl.ds(start, blk), e * dh:(e + 1) * dh],
                            (((1,), (1,)), ((), ())), preferred_element_type=jnp.float32)
        return s * c2 - ck_ref[0, 2 * hp + e, pl.ds(j, 1), :] * LOG2E

    def update(s, j, e, m, l, acc):
        start = pl.multiple_of(j * blk, blk)
        m_new = jnp.maximum(m, jnp.max(s, axis=-1, keepdims=True))
        alpha = jnp.exp2(m - m_new)
        p = jnp.exp2(s - m_new)
        l = alpha * l + jnp.sum(p, axis=-1, keepdims=True)
        acc = alpha * acc + jnp.dot(p.astype(jnp.bfloat16),
                                    v_ref[pl.ds(start, blk), e * dh:(e + 1) * dh],
                                    preferred_element_type=jnp.float32)
        return m_new, l, acc

    def step(j, carry):
        s = [scores(j, e) for e in range(2)]
        return tuple(update(s[e], j, e, *carry[e]) for e in range(2))

    init1 = (jnp.full((blk, 1), NEG_INF, jnp.float32), jnp.zeros((blk, 1), jnp.float32),
             jnp.zeros((blk, dh), jnp.float32))
    carry = lax.fori_loop(0, qi, step, (init1, init1))
    row = lax.broadcasted_iota(jnp.int32, (blk, blk), 0)
    colm = lax.broadcasted_iota(jnp.int32, (blk, blk), 1)
    s = [jnp.where(colm <= row, scores(qi, e), NEG_INF) for e in range(2)]
    for e in range(2):
        _, l, acc = update(s[e], qi, e, *carry[e])
        o_ref[:, e * dh:(e + 1) * dh] = (acc / l).astype(o_ref.dtype)


def fox_attention(zb, cum_t, ybuf, bsz, seq):
    blk = FOX_BLK
    nq = seq // blk
    pw = 2 * FOX_HEAD_DIM
    ck = cum_t.reshape(bsz, LANES, nq, blk)
    return pl.pallas_call(
        _fox_body,
        grid=(bsz, FOX_HEADS // 2, nq),
        in_specs=[pl.BlockSpec((blk, pw), lambda b, h, i: (b * nq + i, ZB_CQ // pw + h)),
                  pl.BlockSpec((seq, pw), lambda b, h, i: (b, ZB_CK // pw + h)),
                  pl.BlockSpec((seq, pw), lambda b, h, i: (b, ZB_CV // pw + h)),
                  pl.BlockSpec((1, 8, nq, blk), lambda b, h, i: (b, 2, 0, 0)),
                  pl.BlockSpec(memory_space=pl.ANY)],
        out_specs=pl.BlockSpec((None, blk, pw), lambda b, h, i: (2, b * nq + i, h)),
        out_shape=jax.ShapeDtypeStruct(ybuf.shape, ybuf.dtype),
        input_output_aliases={4: 0},
        compiler_params=_cparams(("parallel", "arbitrary", "arbitrary"), MIXER_VMEM),
        name="fox_attention",
    )(zb, zb, zb, ck, ybuf)


RG_COPY, RG_DUP, RG_GATES = 0, 1, 2
_SEG = dict(a_q=0, a_k=1, a_v=2, m_q=3, m_k=4, m_v=5, m_i=6, m_f=7, m_o=8,
            c_q=9, c_k=10, c_v=11, c_f=12, g_a=13, g_b=14, g_c=15)


def _regroup_table():
    tiles = []

    def copy(name):
        off, width = SEG_OFFS[_SEG[name]], SEG_WIDTHS[_SEG[name]]
        assert width % WT == 0 and off % SUBLANES == 0
        tiles.extend((off + k * WT, RG_COPY) for k in range(width // WT))

    copy("m_o"); copy("m_k")
    m_i, m_f, c_f = (SEG_OFFS[_SEG[n]] for n in ("m_i", "m_f", "c_f"))
    assert m_f == m_i + MLSTM_HEADS and m_i % SUBLANES == 0 and c_f % SUBLANES == 0
    tiles.append((m_i, RG_GATES))
    assert len(tiles) * WT == WA_ZB
    for name in ("a_q", "m_v", "c_q", "c_k", "c_v", "m_q"):
        copy(name)
    a_k, a_v = SEG_OFFS[_SEG["a_k"]], SEG_OFFS[_SEG["a_v"]]
    assert a_v == a_k + A_KV and a_k % SUBLANES == 0 and 4 * A_KV == WT
    tiles.append((a_k, RG_DUP))
    assert len(tiles) * WT == WA_N
    return tiles, c_f


def _regroup_body(base_ref, mode_ref, w_ref, aux_ref, o_ref):
    del base_ref
    mode = mode_ref[pl.program_id(0)]
    bf = jnp.bfloat16
    hd = SWA_HEAD_DIM

    @pl.when(mode == RG_COPY)
    def _():
        o_ref[...] = w_ref[0].astype(bf)

    @pl.when(mode == RG_DUP)
    def _():
        for i in range(2 * SWA_KV_HEADS):
            head = w_ref[0, i * hd:(i + 1) * hd, :].astype(bf)
            o_ref[2 * i * hd:(2 * i + 1) * hd, :] = head
            o_ref[(2 * i + 1) * hd:(2 * i + 2) * hd, :] = head

    @pl.when(mode == RG_GATES)
    def _():
        nh = MLSTM_HEADS
        o_ref[0:2 * nh, :] = w_ref[0, 0:2 * nh, :].astype(bf)
        o_ref[2 * nh:3 * nh, :] = aux_ref[0].astype(bf)
        o_ref[3 * nh:WT, :] = jnp.zeros((WT - 3 * nh, o_ref.shape[1]), bf)


def regroup_w_in(wt, l):
    tiles, c_f = _regroup_table()
    base = jnp.asarray([b for b, _ in tiles], jnp.int32)
    mode = jnp.asarray([m for _, m in tiles], jnp.int32)
    k = wt.shape[2]
    el = pl.Element
    in_specs = [pl.BlockSpec((el(1), el(WT), el(k)),
                             lambda t, base, mode: (l, pl.multiple_of(base[t], SUBLANES), 0)),
                pl.BlockSpec((el(1), el(FOX_HEADS), el(k)), lambda t, base, mode: (l, c_f, 0))]
    vmem = 2 * (WT * k * 4 + WT * k * 2) + 3 * VMEM_SLACK
    return pl.pallas_call(
        _regroup_body,
        grid_spec=pltpu.PrefetchScalarGridSpec(
            num_scalar_prefetch=2, grid=(len(tiles),), in_specs=in_specs,
            out_specs=pl.BlockSpec((WT, k), lambda t, base, mode: (t, 0))),
        out_shape=jax.ShapeDtypeStruct((WA_N, k), jnp.bfloat16),
        compiler_params=_cparams(("arbitrary",), vmem),
        name="regroup_w_in",
    )(base, mode, wt, wt)


def _cast_body(w_ref, o_ref):
    o_ref[...] = w_ref[...].astype(o_ref.dtype)


def cast_bf16(w, l=None, *, br=1024, bc=2048):
    r, c = w.shape[-2:]
    br, bc = min(br, r), min(bc, c)
    assert r % br == 0 and c % bc == 0
    if l is None:
        in_spec = pl.BlockSpec((br, bc), lambda i, j: (i, j))
    else:
        in_spec = pl.BlockSpec((None, br, bc), lambda i, j: (l, i, j))
    return pl.pallas_call(
        _cast_body,
        grid=(r // br, c // bc),
        in_specs=[in_spec],
        out_specs=pl.BlockSpec((br, bc), lambda i, j: (i, j)),
        out_shape=jax.ShapeDtypeStruct((r, c), jnp.bfloat16),
        compiler_params=_cparams(("parallel", "parallel"), 2 * br * bc * 6 + 2 * VMEM_SLACK),
        name="cast_bf16",
    )(w)


def kernel(x, w_in, b_mlstm_i, b_mlstm_f, b_fox_f, attn_sinks, w_up_swa, w_up_mlstm, w_up_fox,
           w_o, ln1_g, ln1_b, w_ff1, w_ff2, ln2_g, ln2_b):
    bsz, seq, d = x.shape
    t = bsz * seq
    assert d == D_MODEL and seq % FOX_BLK == 0 and seq % MLSTM_L == 0
    xf = x.reshape(t, d).astype(jnp.float32)
    xb = cast_bf16(xf)
    gate_pad = jnp.zeros((LANES - 3 * MLSTM_HEADS,), jnp.float32)
    ups = (w_up_swa, w_up_mlstm, w_up_fox)
    w_ups = [cast_bf16(w, 0) for w in ups]
    w_ob = None
    w_int = jnp.swapaxes(w_in, 1, 2)
    for l in range(DEPTH):
        w_all = regroup_w_in(w_int, l)
        bias_row = jnp.concatenate([b_mlstm_i[l], b_mlstm_f[l], b_fox_f[l], gate_pad]).reshape(1, LANES)
        bm, bn = TILES["in_proj_f32"]
        (zf,) = matmul(xb, w_all, n=ZF_N, w_row=WA_ZF, bm=bm, bn=bn, out_dtype=jnp.float32,
                       name="in_proj_f32")
        gates = Side(w_int, l, SIDE_ROWS["gates"], row_off=SEG_OFFS[_SEG["g_a"]],
                     n_rows=N_BRANCH * D_MODEL)
        bm, bn = TILES["in_proj_bf16"]
        zb, w_g = matmul(xb, w_all, n=ZB_N, w_row=WA_ZB, bm=bm, bn=bn, out_dtype=jnp.bfloat16,
                         name="in_proj_bf16", sides=[gates])
        ybuf = jnp.zeros((3, t, A_Q), jnp.bfloat16)
        ybuf = swa_attention(zb, attn_sinks[l].astype(jnp.float32), ybuf, bsz, seq)
        ybuf, cum_t = mlstm(zb, zf, bias_row.astype(jnp.float32), ybuf, bsz, seq)
        ybuf = fox_attention(zb, cum_t, ybuf, bsz, seq)
        bm, bn = TILES["gated_merge"]
        merge_sides = [Side(w_ff1, l, SIDE_ROWS["w_ff1"])]
        if w_ob is None:
            merge_sides.append(Side(w_o, l, SIDE_ROWS["next_layer"]))
        mix, w_1b, *w_ob_cast = gated_merge(xb, w_g, ybuf, w_ups, bm=bm, bn=bn, sides=merge_sides)
        if w_ob_cast:
            (w_ob,) = w_ob_cast
        bm, bn = TILES["out_proj"]
        (r1,) = matmul_residual(mix, w_ob, xf, bm=bm, bn=bn, bk=D_MODEL, name="out_proj")
        xf, xb = layer_norm(r1, ln1_g[l], ln1_b[l], emit_f32=False)
        bm, bn = TILES["ff1"]
        hid, w_2b = matmul(xb, w_1b, bm=bm, bn=bn, out_dtype=jnp.bfloat16, act="relu2",
                           name="ff1", sides=[Side(w_ff2, l, SIDE_ROWS["w_ff2"])])
        nxt = [Side(w, l + 1, SIDE_ROWS["next_layer"]) for w in (w_o,) + ups] if l + 1 < DEPTH else []
        bm, bn, bk = TILES["ff2"]
        r2, *cast_next = matmul_residual(hid, w_2b, xf, bm=bm, bn=bn, bk=bk, name="ff2", sides=nxt)
        if cast_next:
            w_ob, *w_ups = cast_next
        xf, xb = layer_norm(r2, ln2_g[l], ln2_b[l], emit_f32=(l + 1 == DEPTH))
    return xf.reshape(bsz, seq, d).astype(x.dtype)
```

```python
import functools
from typing import NamedTuple

import jax
import jax.numpy as jnp
from jax import lax
from jax.experimental import pallas as pl
from jax.experimental.pallas import tpu as pltpu

D_MODEL = 4096
SWA_HEADS, SWA_KV_HEADS, SWA_HEAD_DIM, SWA_WINDOW = 16, 2, 64, 128
MLSTM_HEADS, MLSTM_QK_DIM, MLSTM_V_DIM = 8, 64, 128
FOX_HEADS, FOX_HEAD_DIM = 8, 128
LN_EPS = 1e-5
DEPTH = 2
DN_ALPHA = (2 * DEPTH) ** 0.25

A_Q = SWA_HEADS * SWA_HEAD_DIM
A_KV = SWA_KV_HEADS * SWA_HEAD_DIM
B_QK = MLSTM_HEADS * MLSTM_QK_DIM
B_V = MLSTM_HEADS * MLSTM_V_DIM
C_W = FOX_HEADS * FOX_HEAD_DIM
SEG_WIDTHS = (A_Q, A_KV, A_KV, B_QK, B_QK, B_V, MLSTM_HEADS, MLSTM_HEADS, B_V,
              C_W, C_W, C_W, FOX_HEADS, D_MODEL, D_MODEL, D_MODEL)
SEG_OFFS = tuple(sum(SEG_WIDTHS[:i]) for i in range(len(SEG_WIDTHS)))

LANES = 128
SUBLANES = 8
VMEM_LIMIT_CAP = 56 * 1024 * 1024
VMEM_SLACK = 4 * 1024 * 1024
MIXER_VMEM = 32 * 1024 * 1024

ZB_AQ = 0
ZB_MV = 1024
ZB_CQ = 2048
ZB_CK = 3072
ZB_CV = 4096
ZB_MQ = 5120
ZB_AK = 5632
ZB_AV = 5888
ZB_N = 6144
ZF_MO = 0
ZF_MK = 1024
ZF_G = 1536
ZF_N = 1664
WT = 512
WA_ZF = 0
WA_ZB = 2048
WA_N = WA_ZB + ZB_N

TILES = dict(in_proj_f32=(1024, ZF_N), in_proj_f32_first=(512, ZF_N), in_proj_bf16=(1024, 1024),
             gated_merge=(1024, 256),
             out_proj=(1024, 512), ff1=(1024, 1024), ff2=(1024, 1024, 2048))
LN_ROWS = 256
SIDE_ROWS = dict(gates=128, w_ff1=32, w_ff2=128, next_layer=16)

MLSTM_L = 256
FOX_BLK = 512
SWA_BLK = 128
NEG_INF = float("-inf")
LOG2E = 1.4426950408889634


def _cparams(sem, vmem_bytes):
    return pltpu.CompilerParams(dimension_semantics=sem,
                                vmem_limit_bytes=int(min(vmem_bytes, VMEM_LIMIT_CAP)))


_NT = (((1,), (1,)), ((), ()))


class Side(NamedTuple):
    src: jax.Array
    layer: int
    rows: int
    row_off: int = 0
    n_rows: int = 0


def _side_plumbing(sides, grid):
    steps = 1
    for n in grid:
        steps *= n

    def lin(*g):
        s = g[0]
        for a, n in zip(g[1:], grid[1:]):
            s = s * n + a
        return s

    in_specs, out_specs, out_shapes, vmem = [], [], [], 0
    for sd in sides:
        c = sd.src.shape[-1]
        r = sd.n_rows or sd.src.shape[-2]
        rows = sd.rows
        while r // rows > steps:
            rows *= 2
        assert r % rows == 0 and sd.row_off % SUBLANES == 0
        last = r // rows - 1

        def blk(*g, last=last):
            return jnp.minimum(lin(*g), last)

        if sd.row_off or sd.n_rows:
            el = pl.Element
            in_specs.append(pl.BlockSpec(
                (el(1), el(rows), el(c)),
                lambda *g, blk=blk, sd=sd, rows=rows: (
                    sd.layer, pl.multiple_of(sd.row_off + blk(*g) * rows, SUBLANES), 0)))
        else:
            in_specs.append(pl.BlockSpec((None, rows, c),
                                         lambda *g, blk=blk, l=sd.layer: (l, blk(*g), 0)))
        out_specs.append(pl.BlockSpec((rows, c), lambda *g, blk=blk: (blk(*g), 0)))
        out_shapes.append(jax.ShapeDtypeStruct((r, c), jnp.bfloat16))
        vmem += 2 * rows * c * (4 + 2)
    return in_specs, out_specs, out_shapes, vmem


def _cast_sides(side_in, side_out):
    for si, so in zip(side_in, side_out):
        so[...] = si[...].reshape(so.shape).astype(so.dtype)


def _mm_body(x_ref, w_ref, *rest, act, w_t, n_side, emit_xb):
    side_in, o_ref, rest = rest[:n_side], rest[n_side], rest[n_side + 1:]
    _cast_sides(side_in, rest[emit_xb:])
    x = x_ref[...]
    if emit_xb:
        x = x.astype(jnp.bfloat16)
        rest[0][...] = x
    if w_t:
        acc = lax.dot_general(x, w_ref[...], _NT, preferred_element_type=jnp.float32)
    else:
        acc = jnp.dot(x, w_ref[...], preferred_element_type=jnp.float32)
    if act == "relu2":
        acc = jnp.square(jnp.maximum(acc, 0.0))
    o_ref[...] = acc.astype(o_ref.dtype)


def _residual(res_refs):
    if len(res_refs) == 1:
        return res_refs[0][...]
    return _ln_apply(*(ref[...] for ref in res_refs))


def _mm_res_body(x_ref, w_ref, *rest):
    res_refs, o_ref = rest[:-1], rest[-1]
    acc = jnp.dot(x_ref[...], w_ref[...], preferred_element_type=jnp.float32)
    o_ref[...] = DN_ALPHA * _residual(res_refs) + acc


def _mm_res_k_body(x_ref, w_ref, *rest, n_res, n_side):
    res_refs, rest = rest[:n_res], rest[n_res:]
    side_in, o_ref, side_out = rest[:n_side], rest[n_side], rest[n_side + 1:]
    k = pl.program_id(2)

    @pl.when(k == 0)
    def _():
        _cast_sides(side_in, side_out)
        o_ref[...] = DN_ALPHA * _residual(res_refs) + jnp.dot(x_ref[...], w_ref[...],
                                                              preferred_element_type=jnp.float32)

    @pl.when(k > 0)
    def _():
        _cast_sides(side_in, side_out)
        o_ref[...] += jnp.dot(x_ref[...], w_ref[...], preferred_element_type=jnp.float32)


def matmul(x, w, *, bm, bn, out_dtype, act=None, name, n=None, w_row=None, sides=()):
    m, k = x.shape
    w_t = w_row is not None
    emit_xb = x.dtype == jnp.float32
    n = w.shape[1] if n is None else n
    bm, bn = min(bm, m), min(bn, n)
    assert m % bm == 0 and n % bn == 0
    osz = jnp.dtype(out_dtype).itemsize
    grid = (m // bm, n // bn)
    s_in, s_out, s_shapes, s_vmem = _side_plumbing(sides, grid)
    xb_spec = [pl.BlockSpec((bm, k), lambda i, j: (i, 0))] if emit_xb else []
    xb_shape = [jax.ShapeDtypeStruct((m, k), jnp.bfloat16)] if emit_xb else []
    w_bufs, w_mode = (1, dict(pipeline_mode=pl.Buffered(1))) if grid[1] == 1 else (2, {})
    vmem = 2 * (bm * k * (6 if emit_xb else 2) + bm * bn * osz) + w_bufs * k * bn * 2 + bm * bn * 4 \
        + VMEM_SLACK + s_vmem
    if w_t:
        assert w_row % bn == 0
        off = w_row // bn
        w_spec = pl.BlockSpec((bn, k), lambda i, j: (off + j, 0), **w_mode)
    else:
        w_spec = pl.BlockSpec((k, bn), lambda i, j: (0, j), **w_mode)
    return pl.pallas_call(
        functools.partial(_mm_body, act=act, w_t=w_t, n_side=len(sides), emit_xb=emit_xb),
        grid=grid,
        in_specs=[pl.BlockSpec((bm, k), lambda i, j: (i, 0)), w_spec] + s_in,
        out_specs=[pl.BlockSpec((bm, bn), lambda i, j: (i, j))] + xb_spec + s_out,
        out_shape=[jax.ShapeDtypeStruct((m, n), out_dtype)] + xb_shape + s_shapes,
        compiler_params=_cparams(("arbitrary", "arbitrary"), vmem),
        name=name,
    )(x, w, *[sd.src for sd in sides])


def matmul_residual(x, w, res, *, bm, bn, bk, name, sides=()):
    m, k = x.shape
    n = w.shape[1]
    bm, bn, bk = min(bm, m), min(bn, n), min(bk, k)
    assert m % bm == 0 and n % bn == 0 and k % bk == 0
    res_args = tuple(res) if isinstance(res, LnRes) else (res,)
    res_dims = [(bm, bn, True, True), (bm, 1, True, False), (bm, 1, True, False),
                (1, bn, False, True), (1, bn, False, True)][:len(res_args)]

    def res_specs(ij):
        return [pl.BlockSpec((r, c), lambda *g, ri=ri, ci=ci: (ij(*g)[0] if ri else 0, ij(*g)[1] if ci else 0))
                for r, c, ri, ci in res_dims]

    if bk == k:
        assert not sides
        vmem = 2 * (bm * k * 2 + k * bn * 2 + 2 * bm * bn * 4) + bm * bn * 4 + VMEM_SLACK
        return [pl.pallas_call(
            _mm_res_body,
            grid=(m // bm, n // bn),
            in_specs=[pl.BlockSpec((bm, k), lambda i, j: (i, 0)),
                      pl.BlockSpec((k, bn), lambda i, j: (0, j))] + res_specs(lambda i, j: (i, j)),
            out_specs=pl.BlockSpec((bm, bn), lambda i, j: (i, j)),
            out_shape=jax.ShapeDtypeStruct((m, n), jnp.float32),
            compiler_params=_cparams(("parallel", "arbitrary"), vmem),
            name=name,
        )(x, w, *res_args)]
    grid = (m // bm, n // bn, k // bk)
    s_in, s_out, s_shapes, s_vmem = _side_plumbing(sides, grid)
    vmem = 2 * (bm * bk * 2 + bk * bn * 2 + 2 * bm * bn * 4) + 2 * bm * bn * 4 + VMEM_SLACK + s_vmem
    return pl.pallas_call(
        functools.partial(_mm_res_k_body, n_res=len(res_args), n_side=len(sides)),
        grid=grid,
        in_specs=[pl.BlockSpec((bm, bk), lambda i, j, kk: (i, kk)),
                  pl.BlockSpec((bk, bn), lambda i, j, kk: (kk, j))]
        + res_specs(lambda i, j, kk: (i, j)) + s_in,
        out_specs=[pl.BlockSpec((bm, bn), lambda i, j, kk: (i, j))] + s_out,
        out_shape=[jax.ShapeDtypeStruct((m, n), jnp.float32)] + s_shapes,
        compiler_params=_cparams(("arbitrary", "arbitrary", "arbitrary"), vmem),
        name=name,
    )(x, w, *res_args, *[sd.src for sd in sides])


N_BRANCH = 3


def _merge_body(x_ref, wg0_ref, wg1_ref, wg2_ref, y_ref, wu0_ref, wu1_ref, wu2_ref, *rest, n_side):
    side_in, o_ref, side_out = rest[:n_side], rest[n_side], rest[n_side + 1:]
    _cast_sides(side_in, side_out)
    mix = None
    for br, (wg_ref, wu_ref) in enumerate(((wg0_ref, wu0_ref), (wg1_ref, wu1_ref), (wg2_ref, wu2_ref))):
        g = lax.dot_general(x_ref[...], wg_ref[...], _NT, preferred_element_type=jnp.float32)
        u = jnp.dot(y_ref[br], wu_ref[...], preferred_element_type=jnp.float32)
        c = u * (1.0 / (1.0 + jnp.exp(-g)))
        mix = c if mix is None else mix + c
    o_ref[...] = mix.astype(o_ref.dtype)


def gated_merge(xb, wg, y, wus, *, bm, bn, sides=()):
    t, d = xb.shape
    nbr, _, kin = y.shape
    assert nbr == N_BRANCH == len(wus) and wg.shape == (nbr * d, d)
    bm, bn = min(bm, t), min(bn, d)
    assert d % bn == 0
    gstride = d // bn
    grid = (t // bm, d // bn)
    s_in, s_out, s_shapes, s_vmem = _side_plumbing(sides, grid)
    vmem = bm * d * 2 + 2 * (nbr * (d * bn * 2 + bm * kin * 2 + kin * bn * 2) + bm * bn * 2) \
        + 2 * nbr * bm * bn * 4 + 2 * VMEM_SLACK + s_vmem
    wg_specs = [pl.BlockSpec((bn, d), lambda i, j, b=b: (b * gstride + j, 0)) for b in range(nbr)]
    wu_spec = pl.BlockSpec((kin, bn), lambda i, j: (0, j))
    return pl.pallas_call(
        functools.partial(_merge_body, n_side=len(sides)),
        grid=grid,
        in_specs=[pl.BlockSpec((bm, d), lambda i, j: (i, 0), pipeline_mode=pl.Buffered(1))] + wg_specs
        + [pl.BlockSpec((nbr, bm, kin), lambda i, j: (0, i, 0)), wu_spec, wu_spec, wu_spec] + s_in,
        out_specs=[pl.BlockSpec((bm, bn), lambda i, j: (i, j))] + s_out,
        out_shape=[jax.ShapeDtypeStruct((t, d), jnp.bfloat16)] + s_shapes,
        compiler_params=_cparams(("arbitrary", "arbitrary"), vmem),
        name="gated_merge",
    )(xb, wg, wg, wg, y, *wus, *[sd.src for sd in sides])


class LnRes(NamedTuple):
    r: jax.Array
    mu: jax.Array
    rstd: jax.Array
    g: jax.Array
    b: jax.Array


def _ln_apply(r, mu, rstd, g, b):
    return (r - mu) * rstd * g + b


def _ln_body(r_ref, g_ref, b_ref, *outs, emit_f32):
    r = r_ref[...]
    mu = jnp.mean(r, axis=-1, keepdims=True)
    xc = r - mu
    rstd = lax.rsqrt(jnp.mean(xc * xc, axis=-1, keepdims=True) + LN_EPS)
    y = _ln_apply(r, mu, rstd, g_ref[...], b_ref[...])
    if emit_f32:
        (of_ref,) = outs
        of_ref[...] = y
    else:
        ob_ref, mu_ref, rstd_ref = outs
        mu_ref[...] = mu
        rstd_ref[...] = rstd
        ob_ref[...] = y.astype(jnp.bfloat16)


def layer_norm(r, g, b, *, emit_f32, bm=LN_ROWS):
    t, d = r.shape
    bm = min(bm, t)
    g2, b2 = g.reshape(1, d), b.reshape(1, d)
    row = pl.BlockSpec((bm, d), lambda i: (i, 0))
    stat = pl.BlockSpec((bm, 1), lambda i: (i, 0))
    vmem = 2 * (bm * d * 4 * 2 + bm * d * 2) + 4 * bm * d * 4 + VMEM_SLACK
    if emit_f32:
        out_specs = [row]
        out_shape = [jax.ShapeDtypeStruct((t, d), jnp.float32)]
    else:
        out_specs = [row, stat, stat]
        out_shape = [jax.ShapeDtypeStruct((t, d), jnp.bfloat16),
                     jax.ShapeDtypeStruct((t, 1), jnp.float32), jax.ShapeDtypeStruct((t, 1), jnp.float32)]
    outs = pl.pallas_call(
        functools.partial(_ln_body, emit_f32=emit_f32),
        grid=(t // bm,),
        in_specs=[row, pl.BlockSpec((1, d), lambda i: (0, 0)), pl.BlockSpec((1, d), lambda i: (0, 0))],
        out_specs=out_specs,
        out_shape=out_shape,
        compiler_params=_cparams(("parallel",), vmem),
        name="layer_norm",
    )(r, g2, b2)
    if emit_f32:
        return outs[0], None
    xb, mu, rstd = outs
    return LnRes(r, mu, rstd, g2, b2), xb


def _swa_body(sink_ref, bias_ref, q_ref, kp_ref, kc_ref, vp_ref, vc_ref, ybuf_ref, o_ref):
    del ybuf_ref
    blk = SWA_BLK
    c2 = SWA_HEAD_DIM ** -0.5 * LOG2E
    lo = lax.broadcasted_iota(jnp.int32, (2 * blk, LANES), 1) < SWA_HEAD_DIM
    group = SWA_HEADS // SWA_KV_HEADS
    pairs = SWA_HEADS // 2
    kz, vz = [], []
    for g in range(SWA_KV_HEADS):
        kd = jnp.concatenate([kp_ref[:, g * LANES:(g + 1) * LANES],
                              kc_ref[:, g * LANES:(g + 1) * LANES]], axis=0)
        vd = jnp.concatenate([vp_ref[:, g * LANES:(g + 1) * LANES],
                              vc_ref[:, g * LANES:(g + 1) * LANES]], axis=0)
        zero = jnp.zeros_like(kd)
        kz.append(jnp.concatenate([jnp.where(lo, kd, zero), jnp.where(lo, zero, kd)], axis=0))
        vz.append(jnp.concatenate([jnp.where(lo, vd, zero), jnp.where(lo, zero, vd)], axis=0))
    s2 = [lax.dot_general(q_ref[:, p * LANES:(p + 1) * LANES], kz[p // (group // 2)],
                          (((1,), (1,)), ((), ())), preferred_element_type=jnp.float32)
          for p in range(pairs)]
    for p in range(pairs):
        probs = []
        for e in range(2):
            h = 2 * p + e
            sink = sink_ref[h] * LOG2E
            s = s2[p][:, e * 2 * blk:(e + 1) * 2 * blk] * c2 + bias_ref[h]
            m = jnp.maximum(jnp.max(s, axis=-1, keepdims=True), sink)
            pe = jnp.exp2(s - m)
            den = jnp.sum(pe, axis=-1, keepdims=True) + jnp.exp2(sink - m)
            probs.append((pe * (1.0 / den)).astype(jnp.bfloat16))
        p2 = jnp.concatenate(probs, axis=1)
        o_ref[:, p * LANES:(p + 1) * LANES] = jnp.dot(
            p2, vz[p // (group // 2)], preferred_element_type=jnp.float32).astype(o_ref.dtype)


def _swa_bias_table():
    blk = SWA_BLK
    qi = lax.broadcasted_iota(jnp.int32, (blk, 2 * blk), 0)
    kj = lax.broadcasted_iota(jnp.int32, (blk, 2 * blk), 1)
    dist = qi + blk - kj
    window = (dist >= 0) & (dist < SWA_WINDOW)
    slopes = 2.0 ** (-8.0 * jnp.arange(1, SWA_HEADS + 1, dtype=jnp.float32) / SWA_HEADS)
    bias = -(LOG2E * slopes)[:, None, None] * dist.astype(jnp.float32)[None]
    later = jnp.where(window[None], bias, NEG_INF)
    first = jnp.where((window & (kj >= blk))[None], bias, NEG_INF)
    return jnp.stack([first, later])


def swa_attention(zb, sinks, ybuf, bsz, seq):
    blk = SWA_BLK
    nb = seq // blk
    kcol, vcol = ZB_AK // (2 * LANES), ZB_AV // (2 * LANES)

    def cur(b, n):
        return b * nb + n

    def prev(b, n):
        return b * nb + jnp.maximum(n - 1, 0)

    return pl.pallas_call(
        _swa_body,
        grid=(bsz, nb),
        in_specs=[pl.BlockSpec(memory_space=pltpu.SMEM),
                  pl.BlockSpec((None, SWA_HEADS, blk, 2 * blk),
                               lambda b, n: (jnp.minimum(n, 1), 0, 0, 0)),
                  pl.BlockSpec((blk, A_Q), lambda b, n: (cur(b, n), ZB_AQ // A_Q)),
                  pl.BlockSpec((blk, 2 * LANES), lambda b, n: (prev(b, n), kcol)),
                  pl.BlockSpec((blk, 2 * LANES), lambda b, n: (cur(b, n), kcol)),
                  pl.BlockSpec((blk, 2 * LANES), lambda b, n: (prev(b, n), vcol)),
                  pl.BlockSpec((blk, 2 * LANES), lambda b, n: (cur(b, n), vcol)),
                  pl.BlockSpec(memory_space=pl.ANY)],
        out_specs=pl.BlockSpec((None, blk, A_Q), lambda b, n: (0, cur(b, n), 0)),
        out_shape=jax.ShapeDtypeStruct(ybuf.shape, ybuf.dtype),
        input_output_aliases={7: 0},
        compiler_params=_cparams(("parallel", "arbitrary"), MIXER_VMEM),
        name="swa_attention",
    )(sinks, _swa_bias_table(), zb, zb, zb, zb, zb, ybuf)


def _mlstm_body(bias_ref, q_ref, k_ref, v_ref, og_ref, g_ref, ybuf_ref, y_ref, cum_ref,
                c_scr, n_scr, m_scr, carry_scr):
    del ybuf_ref
    c = pl.program_id(1)
    L = MLSTM_L
    dk, dv = MLSTM_QK_DIM, MLSTM_V_DIM

    @pl.when(c == 0)
    def _():
        c_scr[...] = jnp.zeros_like(c_scr)
        n_scr[...] = jnp.zeros_like(n_scr)
        m_scr[...] = jnp.zeros_like(m_scr)
        carry_scr[...] = jnp.zeros_like(carry_scr)

    nh = MLSTM_HEADS
    a = g_ref[...] + bias_ref[...]
    lf = jnp.minimum(a, 0.0) - jnp.log1p(jnp.exp(-jnp.abs(a)))
    row = lax.broadcasted_iota(jnp.int32, (L, L), 0)
    col = lax.broadcasted_iota(jnp.int32, (L, L), 1)
    causal = row >= col
    tri = causal.astype(jnp.float32)
    b_all = jnp.dot(tri, lf, preferred_element_type=jnp.float32,
                    precision=lax.Precision.HIGHEST)
    cum = b_all + carry_scr[0:1, :]
    carry_scr[0:1, :] = cum[L - 1:L, :]
    cum_ref[0] = cum.T

    b2 = pltpu.roll(b_all, LANES - nh, axis=1) * LOG2E
    g2 = a * LOG2E - b2
    rowi = lax.broadcasted_iota(jnp.int32, (L, LANES), 0)
    cm2 = g2
    d = 1
    while d < L:
        cm2 = jnp.maximum(cm2, jnp.where(rowi >= d, pltpu.roll(cm2, d, axis=0), NEG_INF))
        d *= 2
    m2_prev = m_scr[0:1, :]
    u2 = jnp.maximum(m2_prev, cm2)
    u2_last = u2[L - 1:L, :]
    w_inter_all = jnp.exp2(m2_prev - u2)
    floor_all = jnp.exp2(-(b2 + u2))
    wk_all = jnp.exp2(g2 - u2_last)
    decay_row = jnp.exp2(m2_prev - u2_last)
    m_scr[0:1, :] = b2[L - 1:L, :] + u2_last
    g2_t = g2.T

    lane = lax.broadcasted_iota(jnp.int32, (L, LANES), 1)
    lo = lane < dk
    crow = lax.broadcasted_iota(jnp.int32, (2 * dk, 2 * dv), 0)
    ccol = lax.broadcasted_iota(jnp.int32, (2 * dk, 2 * dv), 1)
    crow_lo = crow < dk
    blockdiag = crow_lo == (ccol < dv)
    nlane_lo = lax.broadcasted_iota(jnp.int32, (1, LANES), 1) < dk

    qps, kps, scs, cps, qcs = [], [], [], [], []
    for p in range(MLSTM_HEADS // 2):
        qp = q_ref[:, p * LANES:(p + 1) * LANES]
        kp = k_ref[:, p * LANES:(p + 1) * LANES] * (dk ** -0.5)
        kpb = kp.astype(jnp.bfloat16)
        zero = jnp.zeros_like(kpb)
        kz = jnp.concatenate([jnp.where(lo, kpb, zero), jnp.where(lo, zero, kpb)], axis=0)
        scs.append(lax.dot_general(qp, kz, (((1,), (1,)), ((), ())),
                                   preferred_element_type=jnp.float32))
        cp = c_scr[p]
        qcs.append(jnp.dot(qp, cp.astype(jnp.bfloat16), preferred_element_type=jnp.float32))
        qps.append(qp); kps.append(kp); cps.append(cp)

    for p in range(MLSTM_HEADS // 2):
        qp, kp, sc, cp, qc = qps[p], kps[p], scs[p], cps[p], qcs[p]
        n_row = n_scr[p:p + 1, :]
        qn_prod = qp.astype(jnp.float32) * n_row
        for e in range(2):
            h = 2 * p + e
            decay_mat = jnp.exp2(jnp.where(causal, g2_t[h:h + 1, :] - u2[:, h:h + 1], NEG_INF))
            smat = sc[:, e * L:(e + 1) * L] * decay_mat
            w_inter = w_inter_all[:, h:h + 1]
            qn = jnp.sum(jnp.where(lo == (e == 0), qn_prod, 0.0), axis=-1, keepdims=True)
            num = w_inter * qc[:, e * dv:(e + 1) * dv] + jnp.dot(
                smat.astype(jnp.bfloat16), v_ref[:, h * dv:(h + 1) * dv],
                preferred_element_type=jnp.float32)
            den = w_inter * qn + jnp.sum(smat, axis=-1, keepdims=True)
            rden = 1.0 / jnp.maximum(jnp.abs(den), floor_all[:, h:h + 1])
            og = og_ref[:, h * dv:(h + 1) * dv]
            y_ref[:, h * dv:(h + 1) * dv] = (num * rden * (1.0 / (1.0 + jnp.exp(-og)))).astype(y_ref.dtype)
        decays = [decay_row[:, 2 * p + e:2 * p + e + 1] for e in range(2)]
        kw = kp * jnp.where(lo, wk_all[:, 2 * p:2 * p + 1], wk_all[:, 2 * p + 1:2 * p + 2])
        n_scr[p:p + 1, :] = (jnp.where(nlane_lo, decays[0], decays[1]) * n_row
                             + jnp.sum(kw, axis=0, keepdims=True))
        upd = jnp.dot(kw.T.astype(jnp.bfloat16), v_ref[:, 2 * p * dv:(2 * p + 2) * dv],
                      preferred_element_type=jnp.float32)
        c_scr[p] = jnp.where(crow_lo, decays[0], decays[1]) * cp + jnp.where(blockdiag, upd, 0.0)


def mlstm(zb, zf, bias_row, ybuf, bsz, seq):
    L = MLSTM_L
    nc = seq // L

    def rows(b, c):
        return b * nc + c

    return pl.pallas_call(
        _mlstm_body,
        grid=(bsz, nc),
        in_specs=[pl.BlockSpec((1, LANES), lambda b, c: (0, 0)),
                  pl.BlockSpec((L, B_QK), lambda b, c: (rows(b, c), ZB_MQ // B_QK)),
                  pl.BlockSpec((L, B_QK), lambda b, c: (rows(b, c), ZF_MK // B_QK)),
                  pl.BlockSpec((L, B_V), lambda b, c: (rows(b, c), ZB_MV // B_V)),
                  pl.BlockSpec((L, B_V), lambda b, c: (rows(b, c), ZF_MO // B_V)),
                  pl.BlockSpec((L, LANES), lambda b, c: (rows(b, c), ZF_G // LANES)),
                  pl.BlockSpec(memory_space=pl.ANY)],
        out_specs=[pl.BlockSpec((None, L, B_V), lambda b, c: (1, rows(b, c), 0)),
                   pl.BlockSpec((1, LANES, L), lambda b, c: (b, 0, c))],
        out_shape=[jax.ShapeDtypeStruct(ybuf.shape, ybuf.dtype),
                   jax.ShapeDtypeStruct((bsz, LANES, seq), jnp.float32)],
        input_output_aliases={6: 0},
        scratch_shapes=[pltpu.VMEM((MLSTM_HEADS // 2, 2 * MLSTM_QK_DIM, 2 * MLSTM_V_DIM), jnp.float32),
                        pltpu.VMEM((8, LANES), jnp.float32),
                        pltpu.VMEM((8, LANES), jnp.float32),
                        pltpu.VMEM((8, LANES), jnp.float32)],
        compiler_params=_cparams(("arbitrary", "arbitrary"), MIXER_VMEM),
        name="mlstm",
    )(bias_row, zb, zf, zb, zf, zf, ybuf)


def _fox_body(q_ref, k_ref, v_ref, ck_ref, ybuf_ref, o_ref):
    del ybuf_ref
    hp = pl.program_id(1)
    qi = pl.program_id(2)
    blk = FOX_BLK
    dh = FOX_HEAD_DIM
    c2 = FOX_HEAD_DIM ** -0.5 * LOG2E

    def scores(j, e):
        start = pl.multiple_of(j * blk, blk)
        s = lax.dot_general(q_ref[:, e * dh:(e + 1) * dh], k_ref[pl.ds(start, blk), e * dh:(e + 1) * dh],
                            (((1,), (1,)), ((), ())), preferred_element_type=jnp.float32)
        return s * c2 - ck_ref[0, 2 * hp + e, pl.ds(j, 1), :] * LOG2E

    def update(s, j, e, m, l, acc):
        start = pl.multiple_of(j * blk, blk)
        m_new = jnp.maximum(m, jnp.max(s, axis=-1, keepdims=True))
        alpha = jnp.exp2(m - m_new)
        p = jnp.exp2(s - m_new)
        l = alpha * l + jnp.sum(p, axis=-1, keepdims=True)
        acc = alpha * acc + jnp.dot(p.astype(jnp.bfloat16),
                                    v_ref[pl.ds(start, blk), e * dh:(e + 1) * dh],
                                    preferred_element_type=jnp.float32)
        return m_new, l, acc

    def step(j, carry):
        s = [scores(j, e) for e in range(2)]
        return tuple(update(s[e], j, e, *carry[e]) for e in range(2))

    init1 = (jnp.full((blk, 1), NEG_INF, jnp.float32), jnp.zeros((blk, 1), jnp.float32),
             jnp.zeros((blk, dh), jnp.float32))
    carry = lax.fori_loop(0, qi, step, (init1, init1))
    row = lax.broadcasted_iota(jnp.int32, (blk, blk), 0)
    colm = lax.broadcasted_iota(jnp.int32, (blk, blk), 1)
    s = [jnp.where(colm <= row, scores(qi, e), NEG_INF) for e in range(2)]
    for e in range(2):
        _, l, acc = update(s[e], qi, e, *carry[e])
        o_ref[:, e * dh:(e + 1) * dh] = (acc / l).astype(o_ref.dtype)


def fox_attention(zb, cum_t, ybuf, bsz, seq):
    blk = FOX_BLK
    nq = seq // blk
    pw = 2 * FOX_HEAD_DIM
    ck = cum_t.reshape(bsz, LANES, nq, blk)
    return pl.pallas_call(
        _fox_body,
        grid=(bsz, FOX_HEADS // 2, nq),
        in_specs=[pl.BlockSpec((blk, pw), lambda b, h, i: (b * nq + i, ZB_CQ // pw + h)),
                  pl.BlockSpec((seq, pw), lambda b, h, i: (b, ZB_CK // pw + h)),
                  pl.BlockSpec((seq, pw), lambda b, h, i: (b, ZB_CV // pw + h)),
                  pl.BlockSpec((1, 8, nq, blk), lambda b, h, i: (b, 2, 0, 0)),
                  pl.BlockSpec(memory_space=pl.ANY)],
        out_specs=pl.BlockSpec((None, blk, pw), lambda b, h, i: (2, b * nq + i, h)),
        out_shape=jax.ShapeDtypeStruct(ybuf.shape, ybuf.dtype),
        input_output_aliases={4: 0},
        compiler_params=_cparams(("parallel", "arbitrary", "arbitrary"), MIXER_VMEM),
        name="fox_attention",
    )(zb, zb, zb, ck, ybuf)


RG_COPY, RG_DUP, RG_GATES = 0, 1, 2
_SEG = dict(a_q=0, a_k=1, a_v=2, m_q=3, m_k=4, m_v=5, m_i=6, m_f=7, m_o=8,
            c_q=9, c_k=10, c_v=11, c_f=12, g_a=13, g_b=14, g_c=15)


def _regroup_table():
    tiles = []

    def copy(name):
        off, width = SEG_OFFS[_SEG[name]], SEG_WIDTHS[_SEG[name]]
        assert width % WT == 0 and off % SUBLANES == 0
        tiles.extend((off + k * WT, RG_COPY) for k in range(width // WT))

    copy("m_o"); copy("m_k")
    m_i, m_f, c_f = (SEG_OFFS[_SEG[n]] for n in ("m_i", "m_f", "c_f"))
    assert m_f == m_i + MLSTM_HEADS and m_i % SUBLANES == 0 and c_f % SUBLANES == 0
    tiles.append((m_i, RG_GATES))
    assert len(tiles) * WT == WA_ZB
    for name in ("a_q", "m_v", "c_q", "c_k", "c_v", "m_q"):
        copy(name)
    a_k, a_v = SEG_OFFS[_SEG["a_k"]], SEG_OFFS[_SEG["a_v"]]
    assert a_v == a_k + A_KV and a_k % SUBLANES == 0 and 4 * A_KV == WT
    tiles.append((a_k, RG_DUP))
    assert len(tiles) * WT == WA_N
    return tiles, c_f


def _regroup_body(base_ref, mode_ref, w_ref, aux_ref, o_ref):
    del base_ref
    mode = mode_ref[pl.program_id(0)]
    bf = jnp.bfloat16
    hd = SWA_HEAD_DIM

    @pl.when(mode == RG_COPY)
    def _():
        o_ref[...] = w_ref[0].astype(bf)

    @pl.when(mode == RG_DUP)
    def _():
        for i in range(2 * SWA_KV_HEADS):
            head = w_ref[0, i * hd:(i + 1) * hd, :].astype(bf)
            o_ref[2 * i * hd:(2 * i + 1) * hd, :] = head
            o_ref[(2 * i + 1) * hd:(2 * i + 2) * hd, :] = head

    @pl.when(mode == RG_GATES)
    def _():
        nh = MLSTM_HEADS
        o_ref[0:2 * nh, :] = w_ref[0, 0:2 * nh, :].astype(bf)
        o_ref[2 * nh:3 * nh, :] = aux_ref[0].astype(bf)
        o_ref[3 * nh:WT, :] = jnp.zeros((WT - 3 * nh, o_ref.shape[1]), bf)


def regroup_w_in(wt, l):
    tiles, c_f = _regroup_table()
    base = jnp.asarray([b for b, _ in tiles], jnp.int32)
    mode = jnp.asarray([m for _, m in tiles], jnp.int32)
    k = wt.shape[2]
    el = pl.Element
    in_specs = [pl.BlockSpec((el(1), el(WT), el(k)),
                             lambda t, base, mode: (l, pl.multiple_of(base[t], SUBLANES), 0)),
                pl.BlockSpec((el(1), el(FOX_HEADS), el(k)), lambda t, base, mode: (l, c_f, 0))]
    vmem = 2 * (WT * k * 4 + WT * k * 2) + 3 * VMEM_SLACK
    return pl.pallas_call(
        _regroup_body,
        grid_spec=pltpu.PrefetchScalarGridSpec(
            num_scalar_prefetch=2, grid=(len(tiles),), in_specs=in_specs,
            out_specs=pl.BlockSpec((WT, k), lambda t, base, mode: (t, 0))),
        out_shape=jax.ShapeDtypeStruct((WA_N, k), jnp.bfloat16),
        compiler_params=_cparams(("arbitrary",), vmem),
        name="regroup_w_in",
    )(base, mode, wt, wt)


def _cast_body(w_ref, o_ref):
    o_ref[...] = w_ref[...].astype(o_ref.dtype)


def cast_bf16(w, l=None, *, br=1024, bc=2048):
    r, c = w.shape[-2:]
    br, bc = min(br, r), min(bc, c)
    assert r % br == 0 and c % bc == 0
    if l is None:
        in_spec = pl.BlockSpec((br, bc), lambda i, j: (i, j))
    else:
        in_spec = pl.BlockSpec((None, br, bc), lambda i, j: (l, i, j))
    return pl.pallas_call(
        _cast_body,
        grid=(r // br, c // bc),
        in_specs=[in_spec],
        out_specs=pl.BlockSpec((br, bc), lambda i, j: (i, j)),
        out_shape=jax.ShapeDtypeStruct((r, c), jnp.bfloat16),
        compiler_params=_cparams(("parallel", "parallel"), 2 * br * bc * 6 + 2 * VMEM_SLACK),
        name="cast_bf16",
    )(w)


def kernel(x, w_in, b_mlstm_i, b_mlstm_f, b_fox_f, attn_sinks, w_up_swa, w_up_mlstm, w_up_fox,
           w_o, ln1_g, ln1_b, w_ff1, w_ff2, ln2_g, ln2_b):
    bsz, seq, d = x.shape
    t = bsz * seq
    assert d == D_MODEL and seq % FOX_BLK == 0 and seq % MLSTM_L == 0
    xf = x.reshape(t, d).astype(jnp.float32)
    xb = None
    gate_pad = jnp.zeros((LANES - 3 * MLSTM_HEADS,), jnp.float32)
    ups = (w_up_swa, w_up_mlstm, w_up_fox)
    w_ups = [cast_bf16(w, 0) for w in ups]
    w_ob = None
    w_int = jnp.swapaxes(w_in, 1, 2)
    for l in range(DEPTH):
        w_all = regroup_w_in(w_int, l)
        bias_row = jnp.concatenate([b_mlstm_i[l], b_mlstm_f[l], b_fox_f[l], gate_pad]).reshape(1, LANES)
        if xb is None:
            bm, bn = TILES["in_proj_f32_first"]
            zf, xb = matmul(xf, w_all, n=ZF_N, w_row=WA_ZF, bm=bm, bn=bn, out_dtype=jnp.float32,
                            name="in_proj_f32")
        else:
            bm, bn = TILES["in_proj_f32"]
            (zf,) = matmul(xb, w_all, n=ZF_N, w_row=WA_ZF, bm=bm, bn=bn, out_dtype=jnp.float32,
                           name="in_proj_f32")
        gates = Side(w_int, l, SIDE_ROWS["gates"], row_off=SEG_OFFS[_SEG["g_a"]],
                     n_rows=N_BRANCH * D_MODEL)
        bm, bn = TILES["in_proj_bf16"]
        zb, w_g = matmul(xb, w_all, n=ZB_N, w_row=WA_ZB, bm=bm, bn=bn, out_dtype=jnp.bfloat16,
                         name="in_proj_bf16", sides=[gates])
        ybuf = jnp.zeros((3, t, A_Q), jnp.bfloat16)
        ybuf = swa_attention(zb, attn_sinks[l].astype(jnp.float32), ybuf, bsz, seq)
        ybuf, cum_t = mlstm(zb, zf, bias_row.astype(jnp.float32), ybuf, bsz, seq)
        ybuf = fox_attention(zb, cum_t, ybuf, bsz, seq)
        bm, bn = TILES["gated_merge"]
        merge_sides = [Side(w_ff1, l, SIDE_ROWS["w_ff1"])]
        if w_ob is None:
            merge_sides.append(Side(w_o, l, SIDE_ROWS["next_layer"]))
        mix, w_1b, *w_ob_cast = gated_merge(xb, w_g, ybuf, w_ups, bm=bm, bn=bn, sides=merge_sides)
        if w_ob_cast:
            (w_ob,) = w_ob_cast
        bm, bn = TILES["out_proj"]
        (r1,) = matmul_residual(mix, w_ob, xf, bm=bm, bn=bn, bk=D_MODEL, name="out_proj")
        xf, xb = layer_norm(r1, ln1_g[l], ln1_b[l], emit_f32=False)
        bm, bn = TILES["ff1"]
        hid, w_2b = matmul(xb, w_1b, bm=bm, bn=bn, out_dtype=jnp.bfloat16, act="relu2",
                           name="ff1", sides=[Side(w_ff2, l, SIDE_ROWS["w_ff2"])])
        nxt = [Side(w, l + 1, SIDE_ROWS["next_layer"]) for w in (w_o,) + ups] if l + 1 < DEPTH else []
        bm, bn, bk = TILES["ff2"]
        r2, *cast_next = matmul_residual(hid, w_2b, xf, bm=bm, bn=bn, bk=bk, name="ff2", sides=nxt)
        if cast_next:
            w_ob, *w_ups = cast_next
        xf, xb = layer_norm(r2, ln2_g[l], ln2_b[l], emit_f32=(l + 1 == DEPTH))
    return xf.reshape(bsz, seq, d).astype(x.dtype)
```
